```python
import math
import jax, jax.numpy as jnp
from jax import lax
import numpy as np

D_MODEL = 1024
BATCH = 2
SEQ = 16384
DEPTH = 4

N_META = 16
CHUNK = 64
CONV_K = 4
N_BRANCH = 4
BRANCH_W = D_MODEL // 2

GDN_DK = 128
GDN_DV = 128
GDN_HEADS = BRANCH_W // GDN_DV

M2_HEADDIM = 64
M2_HEADS = BRANCH_W // M2_HEADDIM
M2_GROUPS = 2
M2_HPG = M2_HEADS // M2_GROUPS
M2_DSTATE = 128

HG_DK = 128
HG_HEADS = BRANCH_W // HG_DK
HG_DV = BRANCH_W // HG_HEADS

S5_GROUP = 16
S5_NG = BRANCH_W // S5_GROUP
S5_P = 64

ALPHA = (2 * DEPTH) ** 0.25
BETA_INIT = (8 * DEPTH) ** -0.25
LN_EPS = 1e-5
RMS_EPS = 1e-6

IN_SIZES = (
    3 * BRANCH_W,
    BRANCH_W,
    GDN_HEADS,
    GDN_HEADS,
    BRANCH_W + 2 * M2_GROUPS * M2_DSTATE,
    BRANCH_W,
    M2_HEADS,
    BRANCH_W,
    BRANCH_W,
    BRANCH_W,
    BRANCH_W,
    BRANCH_W,
    BRANCH_W,
    N_BRANCH * D_MODEL,
)
N_IN = sum(IN_SIZES)
IN_OFFSETS = tuple(sum(IN_SIZES[:i + 1]) for i in range(len(IN_SIZES) - 1))

kernel_name = 'hybrid_gdn_ssd_hgrn2_s5_parallel_gated'


def layer_norm(x, g, b):
    xf = x.astype(jnp.float32)
    mu = jnp.mean(xf, axis=-1, keepdims=True)
    var = jnp.mean(jnp.square(xf - mu), axis=-1, keepdims=True)
    return ((xf - mu) * lax.rsqrt(var + LN_EPS) * g + b).astype(x.dtype)


def rms_norm(t):
    return t * lax.rsqrt(jnp.mean(jnp.square(t), axis=-1, keepdims=True) + RMS_EPS)


def l2_norm(t):
    return t * lax.rsqrt(jnp.sum(jnp.square(t), axis=-1, keepdims=True) + RMS_EPS)


def heads(t, n):
    return t.reshape(t.shape[:-1] + (n, t.shape[-1] // n))


def causal_conv(x, w):
    L = x.shape[1]
    xp = jnp.pad(x, ((0, 0), (CONV_K - 1, 0), (0, 0)))
    out = xp[:, CONV_K - 1:] * w[CONV_K - 1]
    for j in range(CONV_K - 1):
        out = out + xp[:, j:j + L] * w[j]
    return out


def front_pad_chunks(t, pad):
    t = jnp.pad(t, ((0, 0), (pad, 0)) + ((0, 0),) * (t.ndim - 2))
    return t.reshape((t.shape[0], -1, CHUNK) + t.shape[2:])


def causal_mask(strict=False):
    i = jnp.arange(CHUNK)
    return (i[:, None] > i[None, :]) if strict else (i[:, None] >= i[None, :])


def gated_delta_rule(q, k, v, g, beta):
    L = q.shape[1]
    pad = (-L) % CHUNK
    prep = lambda t: jnp.moveaxis(front_pad_chunks(t, pad), 3, 1)
    q, k, v, g, beta = prep(q), prep(k), prep(v), prep(g), prep(beta)
    gc = jnp.cumsum(g, axis=-1)
    decay = jnp.exp(jnp.where(causal_mask(), gc[..., :, None] - gc[..., None, :], -jnp.inf))
    kb = k * beta[..., None]
    a = jnp.where(causal_mask(True), jnp.einsum('bhnid,bhnjd->bhnij', kb, k) * decay, 0.0)
    system = a + jnp.eye(CHUNK, dtype=a.dtype)
    u = lax.linalg.triangular_solve(system, v * beta[..., None], left_side=True, lower=True, unit_diagonal=True)
    w = lax.linalg.triangular_solve(system, kb * jnp.exp(gc)[..., None], left_side=True, lower=True, unit_diagonal=True)
    qk = jnp.einsum('bhnid,bhnjd->bhnij', q, k) * decay
    q_dec = q * jnp.exp(gc)[..., None]
    k_dec = k * jnp.exp(gc[..., -1:] - gc)[..., None]
    g_tot = jnp.exp(gc[..., -1])

    def step(S, inp):
        u_n, w_n, qk_n, qd_n, kd_n, gt_n = inp
        v_new = u_n - jnp.einsum('bhck,bhkv->bhcv', w_n, S)
        o_n = jnp.einsum('bhck,bhkv->bhcv', qd_n, S) + jnp.einsum('bhij,bhjv->bhiv', qk_n, v_new)
        S = S * gt_n[..., None, None] + jnp.einsum('bhck,bhcv->bhkv', kd_n, v_new)
        return S, o_n

    xs = tuple(jnp.moveaxis(t, 2, 0) for t in (u, w, qk, q_dec, k_dec, g_tot))
    S0 = jnp.zeros(q.shape[:2] + (q.shape[-1], v.shape[-1]), jnp.float32)
    _, o = lax.scan(step, S0, xs)
    o = jnp.moveaxis(o, 0, 2)
    o = jnp.moveaxis(o, 1, 3).reshape(o.shape[0], -1, o.shape[1], o.shape[-1])
    return o[:, pad:]


def ssd_chunked(xdt, a, Bm, Cm):
    L = xdt.shape[1]
    pad = (-L) % CHUNK
    X = front_pad_chunks(xdt, pad)
    Bsz, Nc = X.shape[:2]
    X = X.reshape(Bsz, Nc, CHUNK, M2_GROUPS, M2_HPG, M2_HEADDIM)
    Bc = front_pad_chunks(Bm, pad)
    Cc = front_pad_chunks(Cm, pad)
    a_cum = jnp.cumsum(front_pad_chunks(a, pad).reshape(Bsz, Nc, CHUNK, M2_GROUPS, M2_HPG), axis=2)
    diff = a_cum[:, :, :, None] - a_cum[:, :, None, :]
    Lmat = jnp.exp(jnp.where(causal_mask()[:, :, None, None], diff, -jnp.inf))
    CB = jnp.einsum('bnlgs,bnmgs->bnlmg', Cc, Bc)
    y_diag = jnp.einsum('bnlmgk,bnmgkp->bnlgkp', CB[..., None] * Lmat, X)
    decay_states = jnp.exp(a_cum[:, :, -1:] - a_cum)
    states = jnp.einsum('bnlgs,bnlgkp->bngkps', Bc, X * decay_states[..., None])
    chunk_decay = jnp.exp(a_cum[:, :, -1])

    def step(S, inp):
        st, dc = inp
        return S * dc[..., None, None] + st, S

    S0 = jnp.zeros((Bsz, M2_GROUPS, M2_HPG, M2_HEADDIM, M2_DSTATE), jnp.float32)
    _, S_start = lax.scan(step, S0, (jnp.moveaxis(states, 1, 0), jnp.moveaxis(chunk_decay, 1, 0)))
    S_start = jnp.moveaxis(S_start, 0, 1)
    y_off = jnp.einsum('bnlgs,bngkps->bnlgkp', Cc, S_start) * jnp.exp(a_cum)[..., None]
    y = (y_diag + y_off).reshape(Bsz, Nc * CHUNK, M2_HEADS, M2_HEADDIM)
    return y[:, pad:]


def hgrn2_chunked(q, log_f, k, i):
    L = q.shape[1]
    pad = (-L) % CHUNK
    prep = lambda t: jnp.moveaxis(front_pad_chunks(t, pad), 3, 1)
    q, log_f, k, i = prep(q), prep(log_f), prep(k), prep(i)
    Bsz, H, Nc = q.shape[:3]

    def intra(S, inp):
        q_t, f_t, k_t, i_t = inp
        S = S * f_t[..., None] + k_t[..., None] * i_t[..., None, :]
        return S, jnp.einsum('bhnk,bhnkv->bhnv', q_t, S)

    xs = tuple(jnp.moveaxis(t, 3, 0) for t in (q, jnp.exp(log_f), k, i))
    S_loc, o_intra = lax.scan(intra, jnp.zeros((Bsz, H, Nc, HG_DK, HG_DV), jnp.float32), xs)
    o_intra = jnp.moveaxis(o_intra, 0, 3)
    g_cum = jnp.cumsum(log_f, axis=3)
    chunk_decay = jnp.exp(g_cum[:, :, :, -1])

    def inter(S, inp):
        s_loc, dc = inp
        return S * dc[..., None] + s_loc, S

    _, S_start = lax.scan(inter, jnp.zeros((Bsz, H, HG_DK, HG_DV), jnp.float32),
                          (jnp.moveaxis(S_loc, 2, 0), jnp.moveaxis(chunk_decay, 2, 0)))
    S_start = jnp.moveaxis(S_start, 0, 2)
    o = o_intra + jnp.einsum('bhnck,bhnkv->bhncv', q * jnp.exp(g_cum), S_start)
    o = jnp.moveaxis(o, 1, 3).reshape(Bsz, Nc * CHUNK, H, HG_DV)
    return o[:, pad:]


def s5_ssm(u, A_re, A_im, B_re, B_im, C_re, C_im, D, log_dt):
    f = lambda t: t.astype(jnp.float32)
    A_re, A_im, B_re, B_im, C_re, C_im, D = map(f, (A_re, A_im, B_re, B_im, C_re, C_im, D))
    Bsz, L, _ = u.shape
    ug = u.reshape(Bsz, L, S5_NG, S5_GROUP)
    dt = jnp.exp(f(log_dt))[:, None]
    mag = jnp.exp(A_re * dt)
    lam_re, lam_im = mag * jnp.cos(A_im * dt), mag * jnp.sin(A_im * dt)
    den = jnp.square(A_re) + jnp.square(A_im)
    nr, ni = lam_re - 1.0, lam_im
    z_re, z_im = (nr * A_re + ni * A_im) / den, (ni * A_re - nr * A_im) / den
    Bb_re = z_re[..., None] * B_re - z_im[..., None] * B_im
    Bb_im = z_re[..., None] * B_im + z_im[..., None] * B_re
    bu_re = jnp.einsum('gpc,blgc->lbgp', Bb_re, ug)
    bu_im = jnp.einsum('gpc,blgc->lbgp', Bb_im, ug)
    a_re = jnp.broadcast_to(lam_re, (L, 1, S5_NG, S5_P))
    a_im = jnp.broadcast_to(lam_im, (L, 1, S5_NG, S5_P))

    def combine(e1, e2):
        a1r, a1i, b1r, b1i = e1
        a2r, a2i, b2r, b2i = e2
        return (a1r * a2r - a1i * a2i, a1r * a2i + a1i * a2r,
                a2r * b1r - a2i * b1i + b2r, a2r * b1i + a2i * b1r + b2i)

    _, _, x_re, x_im = lax.associative_scan(combine, (a_re, a_im, bu_re, bu_im), axis=0)
    y = jnp.einsum('gcp,lbgp->blgc', C_re, x_re) - jnp.einsum('gcp,lbgp->blgc', C_im, x_im)
    return y.reshape(Bsz, L, BRANCH_W) + D * u


def hybrid_layer(h, w_in, gdn_conv_w, gdn_A_log, gdn_dt_bias, gdn_norm_w,
                 m2_conv_w, m2_conv_b, m2_dt_bias, m2_A_log, m2_D, m2_norm_w,
                 hg_lb, hg_norm_w,
                 s5_A_re, s5_A_im, s5_B_re, s5_B_im, s5_C_re, s5_C_im, s5_D, s5_log_dt,
                 s5_glu_w1, s5_glu_w2, w_branch, w_out, ln_g, ln_b):
    f32 = lambda t: t.astype(jnp.float32)
    Bsz, L, _ = h.shape
    proj = h @ w_in
    (gdn_qkv, gdn_z, gdn_b, gdn_a, m2_xbc, m2_z, m2_dt,
     hg_q, hg_f, hg_i, hg_z, s5_u, s5_z, gates) = jnp.split(proj, IN_OFFSETS, axis=-1)

    qkv = jax.nn.silu(causal_conv(f32(gdn_qkv), f32(gdn_conv_w)))
    q, k, v = jnp.split(qkv, 3, axis=-1)
    q = l2_norm(heads(q, GDN_HEADS)) * GDN_DK ** -0.5
    k = l2_norm(heads(k, GDN_HEADS))
    v = heads(v, GDN_HEADS)
    beta = jax.nn.sigmoid(f32(gdn_b))
    g = -jnp.exp(f32(gdn_A_log)) * jax.nn.softplus(f32(gdn_a) + f32(gdn_dt_bias))
    o_a = gated_delta_rule(q, k, v, g, beta)
    y_a = (rms_norm(o_a) * f32(gdn_norm_w)).reshape(Bsz, L, BRANCH_W) * jax.nn.silu(f32(gdn_z))

    xbc = jax.nn.silu(causal_conv(f32(m2_xbc), f32(m2_conv_w)) + f32(m2_conv_b))
    xs, Bm, Cm = jnp.split(xbc, [BRANCH_W, BRANCH_W + M2_GROUPS * M2_DSTATE], axis=-1)
    xs = heads(xs, M2_HEADS)
    Bm, Cm = heads(Bm, M2_GROUPS), heads(Cm, M2_GROUPS)
    dt = jax.nn.softplus(f32(m2_dt) + f32(m2_dt_bias))
    y_ssd = ssd_chunked(xs * dt[..., None], dt * (-jnp.exp(f32(m2_A_log))), Bm, Cm)
    y_ssd = (y_ssd + f32(m2_D)[:, None] * xs).reshape(Bsz, L, BRANCH_W)
    y_b = rms_norm(heads(y_ssd * jax.nn.silu(f32(m2_z)), M2_GROUPS)).reshape(Bsz, L, BRANCH_W) * f32(m2_norm_w)

    zf = f32(hg_f)
    log_f = jnp.logaddexp(jnp.log(hg_lb), jnp.log1p(-hg_lb) + jax.nn.log_sigmoid(zf))
    k_c = (1.0 - hg_lb) * jax.nn.sigmoid(-zf)
    o_c = hgrn2_chunked(heads(jax.nn.silu(f32(hg_q)), HG_HEADS), heads(log_f, HG_HEADS),
                        heads(k_c, HG_HEADS), heads(f32(hg_i), HG_HEADS))
    y_c = (rms_norm(o_c) * f32(hg_norm_w)).reshape(Bsz, L, BRANCH_W) * jax.nn.silu(f32(hg_z))

    y_s5 = jax.nn.gelu(s5_ssm(f32(s5_u), s5_A_re, s5_A_im, s5_B_re, s5_B_im,
                              s5_C_re, s5_C_im, s5_D, s5_log_dt))
    y_d = (y_s5 @ s5_glu_w1) * jax.nn.sigmoid(y_s5 @ s5_glu_w2) * jax.nn.silu(f32(s5_z))

    branches = jnp.stack([y_a, y_b, y_c, y_d], axis=2)
    branch_out = jnp.einsum('blkw,kwd->blkd', branches, w_branch)
    gate = jax.nn.sigmoid(f32(gates).reshape(Bsz, L, N_BRANCH, D_MODEL))
    mixed = jnp.sum(gate * branch_out, axis=2)
    out = mixed @ w_out
    return layer_norm(ALPHA * f32(h) + out, ln_g, ln_b).astype(h.dtype)


def _dt_bias(key, shape):
    dt = jnp.exp(jax.random.uniform(key, shape, jnp.float32, math.log(1e-3), math.log(1e-1)))
    return dt + jnp.log(-jnp.expm1(-dt))


def setup_inputs(seed: int = 0) -> dict:
    key = jax.random.key(seed)
    ks = iter(jax.random.split(key, 48))
    nrm = lambda shape, s: jax.random.normal(next(ks), shape, jnp.float32) * s
    s5_n = jnp.arange(S5_P, dtype=jnp.float32)
    return {
        'x': nrm((BATCH, SEQ, D_MODEL), 1.0),
        'meta_tokens': nrm((N_META, D_MODEL), 1.0),
        'ln_in_g': 1.0 + nrm((D_MODEL,), 0.02),
        'ln_in_b': nrm((D_MODEL,), 0.02),
        'w_in': nrm((DEPTH, D_MODEL, N_IN), D_MODEL ** -0.5),
        'gdn_conv_w': nrm((DEPTH, CONV_K, 3 * BRANCH_W), CONV_K ** -0.5),
        'gdn_A_log': jnp.log(jax.random.uniform(next(ks), (DEPTH, GDN_HEADS), jnp.float32, 1.0, 16.0)),
        'gdn_dt_bias': _dt_bias(next(ks), (DEPTH, GDN_HEADS)),
        'gdn_norm_w': 1.0 + nrm((DEPTH, GDN_DV), 0.02),
        'm2_conv_w': nrm((DEPTH, CONV_K, BRANCH_W + 2 * M2_GROUPS * M2_DSTATE), CONV_K ** -0.5),
        'm2_conv_b': nrm((DEPTH, BRANCH_W + 2 * M2_GROUPS * M2_DSTATE), 0.02),
        'm2_dt_bias': _dt_bias(next(ks), (DEPTH, M2_HEADS)),
        'm2_A_log': jnp.log(jax.random.uniform(next(ks), (DEPTH, M2_HEADS), jnp.float32, 1.0, 16.0)),
        'm2_D': 1.0 + nrm((DEPTH, M2_HEADS), 0.1),
        'm2_norm_w': 1.0 + nrm((DEPTH, BRANCH_W), 0.02),
        'hg_lb_logits': nrm((DEPTH, BRANCH_W), 0.1),
        'hg_norm_w': 1.0 + nrm((DEPTH, HG_DV), 0.02),
        's5_A_re': -0.5 + nrm((DEPTH, S5_NG, S5_P), 0.01),
        's5_A_im': math.pi * s5_n + nrm((DEPTH, S5_NG, S5_P), 0.01),
        's5_B_re': nrm((DEPTH, S5_NG, S5_P, S5_GROUP), (2 * S5_GROUP) ** -0.5),
        's5_B_im': nrm((DEPTH, S5_NG, S5_P, S5_GROUP), (2 * S5_GROUP) ** -0.5),
        's5_C_re': nrm((DEPTH, S5_NG, S5_GROUP, S5_P), S5_P ** -0.5),
        's5_C_im': nrm((DEPTH, S5_NG, S5_GROUP, S5_P), S5_P ** -0.5),
        's5_D': nrm((DEPTH, BRANCH_W), 1.0),
        's5_log_dt': jax.random.uniform(next(ks), (DEPTH, S5_NG), jnp.float32, math.log(1e-3), math.log(1e-1)),
        's5_glu_w1': nrm((DEPTH, BRANCH_W, BRANCH_W), BRANCH_W ** -0.5),
        's5_glu_w2': nrm((DEPTH, BRANCH_W, BRANCH_W), BRANCH_W ** -0.5),
        'w_branch': nrm((DEPTH, N_BRANCH, BRANCH_W, D_MODEL), BRANCH_W ** -0.5 * BETA_INIT),
        'w_out': nrm((DEPTH, D_MODEL, D_MODEL), D_MODEL ** -0.5 * BETA_INIT),
        'ln_g': 1.0 + nrm((DEPTH, D_MODEL), 0.02),
        'ln_b': nrm((DEPTH, D_MODEL), 0.02),
    }


def reference(x, meta_tokens, ln_in_g, ln_in_b, w_in, gdn_conv_w, gdn_A_log, gdn_dt_bias, gdn_norm_w,
              m2_conv_w, m2_conv_b, m2_dt_bias, m2_A_log, m2_D, m2_norm_w,
              hg_lb_logits, hg_norm_w,
              s5_A_re, s5_A_im, s5_B_re, s5_B_im, s5_C_re, s5_C_im, s5_D, s5_log_dt,
              s5_glu_w1, s5_glu_w2, w_branch, w_out, ln_g, ln_b):
    Bsz = x.shape[0]
    meta = jnp.broadcast_to(meta_tokens.astype(x.dtype)[None], (Bsz, N_META, D_MODEL))
    h = layer_norm(jnp.concatenate([meta, x], axis=1), ln_in_g, ln_in_b)
    cum = jnp.cumsum(jax.nn.softmax(hg_lb_logits.astype(jnp.float32), axis=0), axis=0)
    lower_bounds = cum - cum[0:1]
    for l in range(DEPTH):
        h = hybrid_layer(h, w_in[l], gdn_conv_w[l], gdn_A_log[l], gdn_dt_bias[l], gdn_norm_w[l],
                         m2_conv_w[l], m2_conv_b[l], m2_dt_bias[l], m2_A_log[l], m2_D[l], m2_norm_w[l],
                         lower_bounds[l], hg_norm_w[l],
                         s5_A_re[l], s5_A_im[l], s5_B_re[l], s5_B_im[l], s5_C_re[l], s5_C_im[l],
                         s5_D[l], s5_log_dt[l], s5_glu_w1[l], s5_glu_w2[l],
                         w_branch[l], w_out[l], ln_g[l], ln_b[l])
    return h[:, N_META:]
```

```python
import functools
import math

import jax
import jax.numpy as jnp
from jax import lax
from jax.experimental import pallas as pl
from jax.experimental.pallas import tpu as pltpu

F32 = jnp.float32
BF = jnp.bfloat16

D_MODEL = 1024
DEPTH = 4
N_META = 16
CONV_K = 4
N_BRANCH = 4
BRANCH_W = 512
GDN_HEADS = 4
GDN_DK = 128
M2_HEADS = 8
M2_HEADDIM = 64
M2_GROUPS = 2
M2_DSTATE = 128
HG_HEADS = 4
HG_DK = 128
S5_GROUP = 16
S5_NG = 32
S5_P = 64
S5_STATE = S5_NG * S5_P
ALPHA = (2 * DEPTH) ** 0.25
LN_EPS = 1e-5
RMS_EPS = 1e-6

TILE = 256
CH = 64
NCH = TILE // CH
SUB = 16
HG_EXP_CLAMP = 80.0
VMEM_LIMIT = 56 * 1024 * 1024

OFF_QKV, OFF_GZ, OFF_XBC, OFF_MZ, OFF_MDT = 0, 1536, 2048, 3072, 3584
OFF_HG, OFF_S5, OFF_BA = 4096, 6144, 7168
N1 = 7296


def _sigmoid(x):
    return 0.5 * (jnp.tanh(0.5 * x) + 1.0)


def _silu(x):
    return x * _sigmoid(x)


def _softplus(x):
    return jnp.maximum(x, 0.0) + jnp.log1p(jnp.exp(-jnp.abs(x)))


def _mm(a, b):
    return jnp.dot(a.astype(BF), b.astype(BF), preferred_element_type=F32)


def _mm_nt(a, b):
    return lax.dot_general(a.astype(BF), b.astype(BF), (((1,), (1,)), ((), ())),
                           preferred_element_type=F32)


def _mm_tn(a, b):
    return lax.dot_general(a.astype(BF), b.astype(BF), (((0,), (0,)), ((), ())),
                           preferred_element_type=F32)


def _mm3_left(m_bf, x):
    hi = x.astype(BF)
    r1 = x - hi.astype(F32)
    mid = r1.astype(BF)
    lo = (r1 - mid.astype(F32)).astype(BF)
    d = lambda v: jnp.dot(m_bf, v, preferred_element_type=F32)
    return (d(lo) + d(mid)) + d(hi)


def _rms_norm(x):
    return x * lax.rsqrt(jnp.mean(x * x, axis=-1, keepdims=True) + RMS_EPS)


def _masks64():
    i = jnp.arange(CH)[:, None]
    j = jnp.arange(CH)[None, :]
    ms = [(i >= j), (i > j)]
    for s in (1, 2, 4, 8, 16, 32):
        ms.append((i // (2 * s) == j // (2 * s)) & (i % (2 * s) >= s) & (j % (2 * s) < s))
    ms.append(i == j)
    return jnp.stack(ms).astype(F32)


def _cparams():
    return pltpu.CompilerParams(dimension_semantics=("arbitrary", "arbitrary"),
                                vmem_limit_bytes=VMEM_LIMIT)


def _full(shape):
    n = len(shape)
    return pl.BlockSpec(shape, lambda b, t: (0,) * n)


def _ln_in_kernel(x_ref, g_ref, b_ref, o_ref, *, pad):
    t = pl.program_id(1)
    x = x_ref[...]
    mu = jnp.mean(x, axis=-1, keepdims=True)
    xc = x - mu
    var = jnp.mean(xc * xc, axis=-1, keepdims=True)
    y = xc * lax.rsqrt(var + LN_EPS) * g_ref[...] + b_ref[...]
    row = t * TILE + lax.broadcasted_iota(jnp.int32, (TILE, 1), 0)
    o_ref[...] = jnp.where(row >= pad, y, 0.0)


def _ln_in_call(xin, g, b, pad):
    bsz, lp, d = xin.shape
    return pl.pallas_call(
        functools.partial(_ln_in_kernel, pad=pad),
        grid=(bsz, lp // TILE),
        in_specs=[pl.BlockSpec((None, TILE, d), lambda b_, t: (b_, t, 0)), _full((1, d)), _full((1, d))],
        out_specs=pl.BlockSpec((None, TILE, d), lambda b_, t: (b_, t, 0)),
        out_shape=jax.ShapeDtypeStruct((bsz, lp, d), F32),
        compiler_params=_cparams(),
        name="ln_in",
    )(xin, g.reshape(1, d), b.reshape(1, d))


def _proj_kernel(h_ref, w_ref, o_ref):
    hb = h_ref[...].astype(BF)
    for j in range(0, N1, 512):
        w = min(512, N1 - j)
        o_ref[:, j:j + w] = jnp.dot(hb, w_ref[:, j:j + w], preferred_element_type=F32)


def _proj_call(h, w1):
    bsz, lp, d = h.shape
    return pl.pallas_call(
        _proj_kernel,
        grid=(bsz, lp // TILE),
        in_specs=[pl.BlockSpec((None, TILE, d), lambda b_, t: (b_, t, 0)), _full((d, N1))],
        out_specs=pl.BlockSpec((None, TILE, N1), lambda b_, t: (b_, t, 0)),
        out_shape=jax.ShapeDtypeStruct((bsz, lp, N1), F32),
        compiler_params=_cparams(),
        name="proj",
    )(h, w1)


def _conv_chunk(xp_ref, cw, r0):
    x = xp_ref[r0 + 8:r0 + 8 + CH, :] * cw[CONV_K - 1:CONV_K, :]
    for j in range(CONV_K - 1):
        s = r0 + 8 - (CONV_K - 1) + j
        x = x + xp_ref[s:s + CH, :] * cw[j:j + 1, :]
    return x


def _tri_inverse(a, msk_ref):
    x = msk_ref[8] - a * msk_ref[2]
    for lvl in range(1, 6):
        off = a * msk_ref[2 + lvl]
        x = x - _mm(_mm(x, off), x)
    return x


def _gdn_kernel(qkv_ref, z_ref, ba_ref, cw_ref, lane_ref, nw_ref, msk_ref, tri_ref, y_ref, xp_ref, s_ref):
    t = pl.program_id(1)

    @pl.when(t == 0)
    def _():
        xp_ref[0:8, :] = jnp.zeros((8, 3 * BRANCH_W), F32)
        s_ref[...] = jnp.zeros_like(s_ref)

    xp_ref[8:8 + TILE, :] = qkv_ref[...]
    cw = cw_ref[...]
    exp_a = lane_ref[0:1, :]
    dt_bias = lane_ref[1:2, :]
    nw = nw_ref[...]
    causal = msk_ref[0]
    strict = msk_ref[1]
    tri = tri_ref[...]
    for c in range(NCH):
        r0 = c * CH
        qkv = _silu(_conv_chunk(xp_ref, cw, r0))
        ba = ba_ref[r0:r0 + CH, :]
        beta_all = _sigmoid(ba)
        gl = -(exp_a * _softplus(ba + dt_bias))
        gc = _mm3_left(tri, gl)
        gc_t = gc.T
        zs = z_ref[r0:r0 + CH, :]
        for h in range(GDN_HEADS):
            sl = slice(h * GDN_DK, (h + 1) * GDN_DK)
            q = qkv[:, sl]
            k = qkv[:, BRANCH_W + h * GDN_DK:BRANCH_W + (h + 1) * GDN_DK]
            v = qkv[:, 2 * BRANCH_W + h * GDN_DK:2 * BRANCH_W + (h + 1) * GDN_DK]
            q = q * (lax.rsqrt(jnp.sum(q * q, axis=-1, keepdims=True) + RMS_EPS) * GDN_DK ** -0.5)
            k = k * lax.rsqrt(jnp.sum(k * k, axis=-1, keepdims=True) + RMS_EPS)
            beta = beta_all[:, h:h + 1]
            gcol = gc[:, 4 + h:5 + h]
            grow = gc_t[4 + h:5 + h, :]
            decay = jnp.exp(jnp.where(causal > 0, gcol - grow, -jnp.inf))
            kb = k * beta
            a = strict * (_mm_nt(kb, k) * decay)
            tinv = _tri_inverse(a, msk_ref)
            eg = jnp.exp(gcol)
            uw = _mm(tinv, jnp.concatenate([v * beta, kb * eg], axis=1))
            u = uw[:, :GDN_DK]
            w = uw[:, GDN_DK:]
            qk = causal * (_mm_nt(q, k) * decay)
            s = s_ref[h]
            v_new = u - _mm(w, s)
            o = _mm(q * eg, s) + _mm(qk, v_new)
            glast = gc[CH - 1:CH, 4 + h:5 + h]
            kd = k * jnp.exp(glast - gcol)
            s_ref[h] = s * jnp.exp(glast) + _mm_tn(kd, v_new)
            y = _rms_norm(o) * nw * _silu(zs[:, sl])
            y_ref[r0:r0 + CH, sl] = y.astype(y_ref.dtype)
    xp_ref[0:8, :] = xp_ref[TILE:TILE + 8, :]


def _gdn_call(p, conv_w, lane, norm_w, msk, tri):
    bsz, lp, _ = p.shape
    return pl.pallas_call(
        _gdn_kernel,
        grid=(bsz, lp // TILE),
        in_specs=[
            pl.BlockSpec((None, TILE, 1536), lambda b_, t: (b_, t, OFF_QKV // 1536)),
            pl.BlockSpec((None, TILE, 512), lambda b_, t: (b_, t, OFF_GZ // 512)),
            pl.BlockSpec((None, TILE, 128), lambda b_, t: (b_, t, OFF_BA // 128)),
            _full((CONV_K, 1536)), _full((8, 128)), _full((1, GDN_DK)), _full((9, CH, CH)), _full((CH, CH)),
        ],
        out_specs=pl.BlockSpec((None, TILE, BRANCH_W), lambda b_, t: (b_, t, 0)),
        out_shape=jax.ShapeDtypeStruct((bsz, lp, BRANCH_W), BF),
        scratch_shapes=[pltpu.VMEM((TILE + 8, 1536), F32), pltpu.VMEM((GDN_HEADS, GDN_DK, GDN_DK), F32)],
        compiler_params=_cparams(),
        name="gdn",
    )(p, p, p, conv_w, lane, norm_w, msk, tri)


def _ssd_kernel(xbc_ref, zdt_ref, cw_ref, cb_ref, vec_ref, eyet_ref, maskt_ref, bd_ref, tri_ref, ones_ref,
                y_ref, xp_ref, s_ref, *, pad):
    t = pl.program_id(1)

    @pl.when(t == 0)
    def _():
        xp_ref[0:8, :] = jnp.zeros((8, 1024), F32)
        s_ref[...] = jnp.zeros_like(s_ref)

    xp_ref[8:8 + TILE, :] = xbc_ref[...]
    cw = cw_ref[...]
    cb = cb_ref[...]
    dt_bias = vec_ref[0:1, :]
    neg_a = vec_ref[1:2, :]
    d_skip = vec_ref[2:3, :]
    nw = vec_ref[3:4, :]
    tri = tri_ref[...]
    ones = ones_ref[...]
    gw = M2_HEADDIM * (M2_HEADS // M2_GROUPS)
    for c in range(NCH):
        r0 = c * CH
        xbc = _silu(_conv_chunk(xp_ref, cw, r0) + cb)
        xs = xbc[:, :BRANCH_W]
        row = t * TILE + r0 + lax.broadcasted_iota(jnp.int32, (CH, 1), 0)
        valid = (row >= pad).astype(F32)
        dt = _softplus(zdt_ref[r0:r0 + CH, BRANCH_W:] + dt_bias) * valid
        a = dt * neg_a
        acum = _mm3_left(tri, a)
        xdt = xs * dt
        z = zdt_ref[r0:r0 + CH, :BRANCH_W]
        for g in range(M2_GROUPS):
            gs = slice(g * gw, (g + 1) * gw)
            bg = xbc[:, BRANCH_W + g * M2_DSTATE:BRANCH_W + (g + 1) * M2_DSTATE]
            cg = xbc[:, BRANCH_W + M2_GROUPS * M2_DSTATE + g * M2_DSTATE:
                     BRANCH_W + M2_GROUPS * M2_DSTATE + (g + 1) * M2_DSTATE]
            ag = acum[:, gs]
            cbt = _mm_nt(cg, jnp.concatenate([bg] * 4, axis=0))
            arow = _mm3_left(ones, ag * eyet_ref[...])
            lm = jnp.exp(jnp.where(maskt_ref[...] > 0, ag - arow, -jnp.inf))
            xg = xdt[:, gs]
            xbd = jnp.concatenate([xg] * 4, axis=0) * bd_ref[...]
            y_diag = _mm(cbt * lm, xbd)
            alast = ag[CH - 1:CH, :]
            s = s_ref[g]
            y_off = _mm(cg, s) * jnp.exp(ag)
            s_ref[g] = s * jnp.exp(alast) + _mm_tn(bg, xg * jnp.exp(alast - ag))
            y = (y_diag + y_off + d_skip[:, gs] * xs[:, gs]) * _silu(z[:, gs])
            y = _rms_norm(y) * nw[:, gs]
            y_ref[r0:r0 + CH, gs] = y.astype(y_ref.dtype)
    xp_ref[0:8, :] = xp_ref[TILE:TILE + 8, :]


def _ssd_call(p, conv_w, conv_b, vec, eyet, maskt, bd, tri, ones, pad):
    bsz, lp, _ = p.shape
    return pl.pallas_call(
        functools.partial(_ssd_kernel, pad=pad),
        grid=(bsz, lp // TILE),
        in_specs=[
            pl.BlockSpec((None, TILE, 1024), lambda b_, t: (b_, t, OFF_XBC // 1024)),
            pl.BlockSpec((None, TILE, 1024), lambda b_, t: (b_, t, OFF_MZ // 1024)),
            _full((CONV_K, 1024)), _full((1, 1024)), _full((8, 512)),
            _full((CH, 256)), _full((CH, 256)), _full((256, 256)), _full((CH, CH)), _full((CH, CH)),
        ],
        out_specs=pl.BlockSpec((None, TILE, BRANCH_W), lambda b_, t: (b_, t, 0)),
        out_shape=jax.ShapeDtypeStruct((bsz, lp, BRANCH_W), BF),
        scratch_shapes=[pltpu.VMEM((TILE + 8, 1024), F32), pltpu.VMEM((M2_GROUPS, M2_DSTATE, 256), F32)],
        compiler_params=_cparams(),
        name="ssd",
    )(p, p, conv_w, conv_b, vec, eyet, maskt, bd, tri, ones)


def _hgrn_kernel(p_ref, vec_ref, nw_ref, msk_ref, tri_ref, y_ref, s_ref):
    t = pl.program_id(1)

    @pl.when(t == 0)
    def _():
        s_ref[...] = jnp.zeros_like(s_ref)

    log_lb = vec_ref[0:1, :]
    log1m_lb = vec_ref[1:2, :]
    one_m_lb = vec_ref[2:3, :]
    nw = nw_ref[...]
    causal = msk_ref[0]
    tri = tri_ref[...]
    for c in range(NCH):
        r0 = c * CH
        q = _silu(p_ref[r0:r0 + CH, 0:512])
        zf = p_ref[r0:r0 + CH, 512:1024]
        iv = p_ref[r0:r0 + CH, 1024:1536]
        z = p_ref[r0:r0 + CH, 1536:2048]
        e = jnp.exp(-jnp.abs(zf))
        l1pe = jnp.log1p(e)
        lsig = jnp.minimum(zf, 0.0) - l1pe
        bb = log1m_lb + lsig
        logf = jnp.maximum(log_lb, bb) + jnp.log1p(jnp.exp(-jnp.abs(log_lb - bb)))
        sig_neg = jnp.where(zf >= 0, e, 1.0) / (1.0 + e)
        k = one_m_lb * sig_neg
        gcum = _mm3_left(tri, logf)
        glast = gcum[CH - 1:CH, :]
        kd = k * jnp.exp(glast - gcum)
        o_rows = []
        for i in range(CH // SUB):
            rs = slice(i * SUB, (i + 1) * SUB)
            n = (i + 1) * SUB
            gref = gcum[i * SUB - 1:i * SUB, :] if i > 0 else jnp.zeros((1, BRANCH_W), F32)
            qt = q[rs] * jnp.exp(gcum[rs] - gref)
            kt = k[:n] * jnp.exp(jnp.minimum(gref - gcum[:n], HG_EXP_CLAMP))
            qs = qt * jnp.exp(gref)
            outs = []
            for h in range(HG_HEADS):
                sl = slice(h * HG_DK, (h + 1) * HG_DK)
                am = _mm_nt(qt[:, sl], kt[:, sl]) * causal[rs, :n]
                outs.append(_mm(am, iv[:n, sl]) + _mm_nt(qs[:, sl], s_ref[h]))
            o_rows.append(outs)
        for h in range(HG_HEADS):
            sl = slice(h * HG_DK, (h + 1) * HG_DK)
            o = jnp.concatenate([o_rows[i][h] for i in range(CH // SUB)], axis=0)
            s_ref[h] = s_ref[h] * jnp.exp(glast[:, sl]) + _mm_tn(iv[:, sl], kd[:, sl])
            y = _rms_norm(o) * nw * _silu(z[:, sl])
            y_ref[r0:r0 + CH, sl] = y.astype(y_ref.dtype)


def _hgrn_call(p, vec, norm_w, msk, tri):
    bsz, lp, _ = p.shape
    return pl.pallas_call(
        _hgrn_kernel,
        grid=(bsz, lp // TILE),
        in_specs=[
            pl.BlockSpec((None, TILE, 2048), lambda b_, t: (b_, t, OFF_HG // 2048)),
            _full((8, 512)), _full((1, HG_DK)), _full((9, CH, CH)), _full((CH, CH)),
        ],
        out_specs=pl.BlockSpec((None, TILE, BRANCH_W), lambda b_, t: (b_, t, 0)),
        out_shape=jax.ShapeDtypeStruct((bsz, lp, BRANCH_W), BF),
        scratch_shapes=[pltpu.VMEM((HG_HEADS, HG_DK, HG_DK), F32)],
        compiler_params=_cparams(),
        name="hgrn2",
    )(p, vec, norm_w, msk, tri)


def _s5_kernel(p_ref, bw_ref, cre_ref, cim_ref, pw_ref, d_ref, w1_ref, w2_ref, y_ref, xr_ref, xi_ref, c_ref):
    t = pl.program_id(1)

    @pl.when(t == 0)
    def _():
        c_ref[...] = jnp.zeros_like(c_ref)

    u = p_ref[:, 0:512]
    ub = u.astype(BF)
    for j in range(4):
        bu = jnp.dot(ub[:, j * 128:(j + 1) * 128], bw_ref[j], preferred_element_type=F32)
        xr_ref[:, j * 512:(j + 1) * 512] = bu[:, :512]
        xi_ref[:, j * 512:(j + 1) * 512] = bu[:, 512:]

    rowid = lax.broadcasted_iota(jnp.int32, (8, 512), 0)
    for cg in range(4):
        cs = slice(cg * 512, (cg + 1) * 512)
        l8r = pw_ref[0:8, cs]
        l8i = pw_ref[8:16, cs]
        steps = [(d, l8r[d - 1:d, :], l8i[d - 1:d, :]) for d in (1, 2, 4)]

        def body(b, carry, cs=cs, l8r=l8r, l8i=l8i, steps=steps):
            cr, ci = carry
            rows = pl.ds(pl.multiple_of(b * 8, 8), 8)
            r = xr_ref[rows, cs]
            im = xi_ref[rows, cs]
            for d, lr, li in steps:
                rsh = jnp.where(rowid >= d, pltpu.roll(r, d, 0), 0.0)
                ish = jnp.where(rowid >= d, pltpu.roll(im, d, 0), 0.0)
                r, im = r + (lr * rsh - li * ish), im + (lr * ish + li * rsh)
            r, im = r + (l8r * cr - l8i * ci), im + (l8r * ci + l8i * cr)
            xr_ref[rows, cs] = r
            xi_ref[rows, cs] = im
            return r[7:8, :], im[7:8, :]

        cr, ci = lax.fori_loop(0, TILE // 8, body, (c_ref[0:1, cs], c_ref[1:2, cs]))
        c_ref[0:1, cs] = cr
        c_ref[1:2, cs] = ci

    ys = []
    for j in range(4):
        cs = slice(j * 512, (j + 1) * 512)
        ys.append(_mm(xr_ref[:, cs], cre_ref[j]) - _mm(xi_ref[:, cs], cim_ref[j]))
    y = jnp.concatenate(ys, axis=1) + d_ref[...] * u
    y = 0.5 * y * (1.0 + jnp.tanh(math.sqrt(2.0 / math.pi) * (y + 0.044715 * (y * y * y))))
    yb = y.astype(BF)
    glu = jnp.dot(yb, w1_ref[...], preferred_element_type=F32) * _sigmoid(
        jnp.dot(yb, w2_ref[...], preferred_element_type=F32))
    y_ref[...] = (glu * _silu(p_ref[:, 512:1024])).astype(y_ref.dtype)


def _s5_call(p, bw, cre, cim, pw, dvec, w1, w2):
    bsz, lp, _ = p.shape
    return pl.pallas_call(
        _s5_kernel,
        grid=(bsz, lp // TILE),
        in_specs=[
            pl.BlockSpec((None, TILE, 1024), lambda b_, t: (b_, t, OFF_S5 // 1024)),
            _full((4, 128, 1024)), _full((4, 512, 128)), _full((4, 512, 128)), _full((16, S5_STATE)),
            _full((1, 512)), _full((512, 512)), _full((512, 512)),
        ],
        out_specs=pl.BlockSpec((None, TILE, BRANCH_W), lambda b_, t: (b_, t, 0)),
        out_shape=jax.ShapeDtypeStruct((bsz, lp, BRANCH_W), BF),
        scratch_shapes=[pltpu.VMEM((TILE, S5_STATE), F32), pltpu.VMEM((TILE, S5_STATE), F32),
                        pltpu.VMEM((2, S5_STATE), F32)],
        compiler_params=_cparams(),
        name="s5",
    )(p, bw, cre, cim, pw, dvec, w1, w2)


def _merge_kernel(h_ref, ya_ref, yb_ref, yc_ref, yd_ref, wg_ref, wb_ref, wo_ref, g_ref, b_ref, o_ref, *, pad):
    t = pl.program_id(1)
    h = h_ref[...]
    hb = h.astype(BF)
    mixed = jnp.zeros((TILE, D_MODEL), F32)
    for br, y_ref in enumerate((ya_ref, yb_ref, yc_ref, yd_ref)):
        gate = _sigmoid(jnp.dot(hb, wg_ref[:, br * D_MODEL:(br + 1) * D_MODEL], preferred_element_type=F32))
        mixed = mixed + gate * jnp.dot(y_ref[...], wb_ref[br], preferred_element_type=F32)
    out = jnp.dot(mixed.astype(BF), wo_ref[...], preferred_element_type=F32)
    r = ALPHA * h + out
    mu = jnp.mean(r, axis=-1, keepdims=True)
    rc = r - mu
    var = jnp.mean(rc * rc, axis=-1, keepdims=True)
    y = rc * lax.rsqrt(var + LN_EPS) * g_ref[...] + b_ref[...]
    row = t * TILE + lax.broadcasted_iota(jnp.int32, (TILE, 1), 0)
    o_ref[...] = jnp.where(row >= pad, y, 0.0)


def _merge_call(h, ya, yb, yc, yd, wg, wb, wo, g, b, pad):
    bsz, lp, d = h.shape
    tok = lambda w: pl.BlockSpec((None, TILE, w), lambda b_, t: (b_, t, 0))
    return pl.pallas_call(
        functools.partial(_merge_kernel, pad=pad),
        grid=(bsz, lp // TILE),
        in_specs=[tok(d), tok(BRANCH_W), tok(BRANCH_W), tok(BRANCH_W), tok(BRANCH_W),
                  _full((d, N_BRANCH * d)), _full((N_BRANCH, BRANCH_W, d)), _full((d, d)),
                  _full((1, d)), _full((1, d))],
        out_specs=tok(d),
        out_shape=jax.ShapeDtypeStruct((bsz, lp, d), F32),
        compiler_params=_cparams(),
        name="merge",
    )(h, ya, yb, yc, yd, wg, wb, wo, g.reshape(1, d), b.reshape(1, d))


def _split_w_in(w):
    sizes = (1536, 512, 4, 4, 1024, 512, 8, 512, 512, 512, 512, 512, 512, 4096)
    offs = [0]
    for s in sizes:
        offs.append(offs[-1] + s)
    return [w[:, offs[i]:offs[i + 1]] for i in range(len(sizes))]


def _prep_w1(w):
    (qkv, gz, gb, ga, xbc, mz, mdt, hq, hf, hi, hz, su, sz, gates) = _split_w_in(w)
    d = w.shape[0]
    ba = jnp.concatenate([gb, ga, jnp.zeros((d, 120), w.dtype)], axis=1)
    mdt_e = jnp.repeat(mdt, M2_HEADDIM, axis=1)
    w1 = jnp.concatenate([qkv, gz, xbc, mz, mdt_e, hq, hf, hi, hz, su, sz, ba], axis=1)
    return w1.astype(BF), gates.astype(BF)


def _lane_row(vals, off, width=128):
    return jnp.zeros((width,), F32).at[off:off + vals.shape[0]].set(vals.astype(F32))


def _prep_s5(a_re, a_im, b_re, b_im, c_re, c_im, log_dt):
    f = lambda v: v.astype(F32)
    a_re, a_im, b_re, b_im, c_re, c_im = map(f, (a_re, a_im, b_re, b_im, c_re, c_im))
    dt = jnp.exp(f(log_dt))[:, None]
    mag = jnp.exp(a_re * dt)
    lam_re, lam_im = mag * jnp.cos(a_im * dt), mag * jnp.sin(a_im * dt)
    den = jnp.square(a_re) + jnp.square(a_im)
    nr, ni = lam_re - 1.0, lam_im
    z_re, z_im = (nr * a_re + ni * a_im) / den, (ni * a_re - nr * a_im) / den
    bb_re = z_re[..., None] * b_re - z_im[..., None] * b_im
    bb_im = z_re[..., None] * b_im + z_im[..., None] * b_re
    eye8 = jnp.eye(8, dtype=F32)

    def in_blocks(bb):
        blk = jnp.transpose(bb, (0, 2, 1)).reshape(4, 8, S5_GROUP, S5_P)
        return jnp.einsum('jicp,ik->jickp', blk, eye8).reshape(4, 128, 512)

    def out_blocks(cc):
        blk = jnp.transpose(cc, (0, 2, 1)).reshape(4, 8, S5_P, S5_GROUP)
        return jnp.einsum('jipc,ik->jipkc', blk, eye8).reshape(4, 512, 128)

    bw = jnp.concatenate([in_blocks(bb_re), in_blocks(bb_im)], axis=2).astype(BF)
    cre = out_blocks(c_re).astype(BF)
    cim = out_blocks(c_im).astype(BF)
    pw = jnp.arange(1, 9, dtype=F32)[:, None, None]
    pmag = jnp.exp(a_re[None] * dt[None] * pw)
    pang = a_im[None] * dt[None] * pw
    pw_tab = jnp.concatenate([(pmag * jnp.cos(pang)).reshape(8, S5_STATE),
                              (pmag * jnp.sin(pang)).reshape(8, S5_STATE)], axis=0)
    return bw, cre, cim, pw_tab


@jax.jit
def kernel(x, meta_tokens, ln_in_g, ln_in_b, w_in, gdn_conv_w, gdn_A_log, gdn_dt_bias, gdn_norm_w, m2_conv_w, m2_conv_b, m2_dt_bias, m2_A_log, m2_D, m2_norm_w, hg_lb_logits, hg_norm_w, s5_A_re, s5_A_im, s5_B_re, s5_B_im, s5_C_re, s5_C_im, s5_D, s5_log_dt, s5_glu_w1, s5_glu_w2, w_branch, w_out, ln_g, ln_b):
    bsz, seq, d = x.shape
    ltot = N_META + seq
    lp = -(-ltot // TILE) * TILE
    pad = lp - ltot
    meta = jnp.broadcast_to(meta_tokens.astype(x.dtype)[None], (bsz, N_META, d))
    xin = jnp.concatenate([jnp.zeros((bsz, pad, d), x.dtype), meta, x], axis=1)
    h = _ln_in_call(xin, ln_in_g.astype(F32), ln_in_b.astype(F32), pad)

    msk = _masks64()
    tri = msk[0].astype(BF)
    ones = jnp.ones((CH, CH), BF)
    jj = jnp.arange(256)
    eyet = (jnp.arange(CH)[:, None] == (jj % CH)[None, :]).astype(F32)
    maskt = (jnp.arange(CH)[:, None] >= (jj % CH)[None, :]).astype(F32)
    bd = ((jj // CH)[:, None] == (jj // CH)[None, :]).astype(F32)

    cum = jnp.cumsum(jax.nn.softmax(hg_lb_logits.astype(F32), axis=0), axis=0)
    lower_bounds = cum - cum[0:1]

    for l in range(w_in.shape[0]):
        w1, wg = _prep_w1(w_in[l])
        p = _proj_call(h, w1)

        lane = jnp.stack([_lane_row(jnp.exp(gdn_A_log[l].astype(F32)), 4), _lane_row(gdn_dt_bias[l], 4)]
                         + [jnp.zeros((128,), F32)] * 6)
        ya = _gdn_call(p, gdn_conv_w[l].astype(F32), lane, gdn_norm_w[l].astype(F32).reshape(1, GDN_DK), msk, tri)

        rep = lambda v: jnp.repeat(v.astype(F32), M2_HEADDIM)
        vec = jnp.stack([rep(m2_dt_bias[l]), rep(-jnp.exp(m2_A_log[l].astype(F32))), rep(m2_D[l]),
                         m2_norm_w[l].astype(F32)] + [jnp.zeros((BRANCH_W,), F32)] * 4)
        yb = _ssd_call(p, m2_conv_w[l].astype(F32), m2_conv_b[l].astype(F32).reshape(1, -1), vec,
                       eyet, maskt, bd, tri, ones, pad)

        lb = lower_bounds[l]
        hvec = jnp.stack([jnp.log(lb), jnp.log1p(-lb), 1.0 - lb] + [jnp.zeros((BRANCH_W,), F32)] * 5)
        yc = _hgrn_call(p, hvec, hg_norm_w[l].astype(F32).reshape(1, HG_DK), msk, tri)

        bw, cre, cim, pw_tab = _prep_s5(s5_A_re[l], s5_A_im[l], s5_B_re[l], s5_B_im[l], s5_C_re[l], s5_C_im[l],
                                        s5_log_dt[l])
        yd = _s5_call(p, bw, cre, cim, pw_tab, s5_D[l].astype(F32).reshape(1, -1),
                      s5_glu_w1[l].astype(BF), s5_glu_w2[l].astype(BF))

        h = _merge_call(h, ya, yb, yc, yd, wg, w_branch[l].astype(BF), w_out[l].astype(BF),
                        ln_g[l].astype(F32), ln_b[l].astype(F32), pad)
    return h[:, pad + N_META:]
```

```python
import functools
import math

import jax
import jax.numpy as jnp
from jax import lax
from jax.experimental import pallas as pl
from jax.experimental.pallas import tpu as pltpu

F32 = jnp.float32
BF = jnp.bfloat16

D_MODEL = 1024
DEPTH = 4
N_META = 16
CONV_K = 4
N_BRANCH = 4
BRANCH_W = 512
GDN_HEADS = 4
GDN_DK = 128
M2_HEADS = 8
M2_HEADDIM = 64
M2_GROUPS = 2
M2_DSTATE = 128
HG_HEADS = 4
HG_DK = 128
S5_GROUP = 16
S5_NG = 32
S5_P = 64
S5_STATE = S5_NG * S5_P
ALPHA = (2 * DEPTH) ** 0.25
LN_EPS = 1e-5
RMS_EPS = 1e-6

TILE = 256
CH = 64
NCH = TILE // CH
SUB = 16
HG_EXP_CLAMP = 80.0
VMEM_LIMIT = 56 * 1024 * 1024

OFF_QKV, OFF_GZ, OFF_XBC, OFF_MZ, OFF_MDT = 0, 1536, 2048, 3072, 3584
OFF_HG, OFF_S5, OFF_BA = 4096, 6144, 7168
N1 = 7296


def _sigmoid(x):
    return 0.5 * (jnp.tanh(0.5 * x) + 1.0)


def _silu(x):
    return x * _sigmoid(x)


def _softplus(x):
    return jnp.maximum(x, 0.0) + jnp.log1p(jnp.exp(-jnp.abs(x)))


def _mm(a, b):
    return jnp.dot(a.astype(BF), b.astype(BF), preferred_element_type=F32)


def _mm_nt(a, b):
    return lax.dot_general(a.astype(BF), b.astype(BF), (((1,), (1,)), ((), ())),
                           preferred_element_type=F32)


def _mm_tn(a, b):
    return lax.dot_general(a.astype(BF), b.astype(BF), (((0,), (0,)), ((), ())),
                           preferred_element_type=F32)


def _mm3_left(m_bf, x):
    hi = x.astype(BF)
    r1 = x - hi.astype(F32)
    mid = r1.astype(BF)
    lo = (r1 - mid.astype(F32)).astype(BF)
    d = lambda v: jnp.dot(m_bf, v, preferred_element_type=F32)
    return (d(lo) + d(mid)) + d(hi)


def _rms_norm(x):
    return x * lax.rsqrt(jnp.mean(x * x, axis=-1, keepdims=True) + RMS_EPS)


def _masks64():
    i = jnp.arange(CH)[:, None]
    j = jnp.arange(CH)[None, :]
    ms = [(i >= j), (i > j)]
    for s in (1, 2, 4, 8, 16, 32):
        ms.append((i // (2 * s) == j // (2 * s)) & (i % (2 * s) >= s) & (j % (2 * s) < s))
    ms.append(i == j)
    return jnp.stack(ms).astype(F32)


def _cparams():
    return pltpu.CompilerParams(dimension_semantics=("arbitrary", "arbitrary"),
                                vmem_limit_bytes=VMEM_LIMIT)


def _full(shape):
    n = len(shape)
    return pl.BlockSpec(shape, lambda b, t: (0,) * n)


def _ln_in_kernel(x_ref, g_ref, b_ref, o_ref, *, pad):
    t = pl.program_id(1)
    x = x_ref[...]
    mu = jnp.mean(x, axis=-1, keepdims=True)
    xc = x - mu
    var = jnp.mean(xc * xc, axis=-1, keepdims=True)
    y = xc * lax.rsqrt(var + LN_EPS) * g_ref[...] + b_ref[...]
    row = t * TILE + lax.broadcasted_iota(jnp.int32, (TILE, 1), 0)
    o_ref[...] = jnp.where(row >= pad, y, 0.0)


def _ln_in_call(xin, g, b, pad):
    bsz, lp, d = xin.shape
    return pl.pallas_call(
        functools.partial(_ln_in_kernel, pad=pad),
        grid=(bsz, lp // TILE),
        in_specs=[pl.BlockSpec((None, TILE, d), lambda b_, t: (b_, t, 0)), _full((1, d)), _full((1, d))],
        out_specs=pl.BlockSpec((None, TILE, d), lambda b_, t: (b_, t, 0)),
        out_shape=jax.ShapeDtypeStruct((bsz, lp, d), F32),
        compiler_params=_cparams(),
        name="ln_in",
    )(xin, g.reshape(1, d), b.reshape(1, d))


def _proj_kernel(h_ref, w_ref, o_ref):
    hb = h_ref[...].astype(BF)
    for j in range(0, N1, 512):
        w = min(512, N1 - j)
        o_ref[:, j:j + w] = jnp.dot(hb, w_ref[:, j:j + w], preferred_element_type=F32)


def _proj_call(h, w1):
    bsz, lp, d = h.shape
    return pl.pallas_call(
        _proj_kernel,
        grid=(bsz, lp // TILE),
        in_specs=[pl.BlockSpec((None, TILE, d), lambda b_, t: (b_, t, 0)), _full((d, N1))],
        out_specs=pl.BlockSpec((None, TILE, N1), lambda b_, t: (b_, t, 0)),
        out_shape=jax.ShapeDtypeStruct((bsz, lp, N1), F32),
        compiler_params=_cparams(),
        name="proj",
    )(h, w1)


def _conv_chunk(xp_ref, cw, r0):
    x = xp_ref[r0 + 8:r0 + 8 + CH, :] * cw[CONV_K - 1:CONV_K, :]
    for j in range(CONV_K - 1):
        s = r0 + 8 - (CONV_K - 1) + j
        x = x + xp_ref[s:s + CH, :] * cw[j:j + 1, :]
    return x


def _tri_inverse_many(a_list, msk_ref):
    xs = [msk_ref[8] - a * msk_ref[2] for a in a_list]
    for lvl in range(1, 6):
        m = msk_ref[2 + lvl]
        t1 = [_mm(x, a * m) for x, a in zip(xs, a_list)]
        xs = [x - _mm(t, x) for x, t in zip(xs, t1)]
    return xs


def _gdn_kernel(qkv_ref, z_ref, ba_ref, cw_ref, lane_ref, nw_ref, msk_ref, tri_ref, y_ref, xp_ref, s_ref):
    t = pl.program_id(1)

    @pl.when(t == 0)
    def _():
        xp_ref[0:8, :] = jnp.zeros((8, 3 * BRANCH_W), F32)
        s_ref[...] = jnp.zeros_like(s_ref)

    xp_ref[8:8 + TILE, :] = qkv_ref[...]
    cw = cw_ref[...]
    exp_a = lane_ref[0:1, :]
    dt_bias = lane_ref[1:2, :]
    nw = nw_ref[...]
    causal = msk_ref[0]
    strict = msk_ref[1]
    tri = tri_ref[...]
    items = []
    a_list = []
    for c in range(NCH):
        r0 = c * CH
        qkv = _silu(_conv_chunk(xp_ref, cw, r0))
        ba = ba_ref[r0:r0 + CH, :]
        beta_all = _sigmoid(ba)
        gl = -(exp_a * _softplus(ba + dt_bias))
        gc = _mm3_left(tri, gl)
        gc_t = gc.T
        for h in range(GDN_HEADS):
            sl = slice(h * GDN_DK, (h + 1) * GDN_DK)
            q = qkv[:, sl]
            k = qkv[:, BRANCH_W + h * GDN_DK:BRANCH_W + (h + 1) * GDN_DK]
            v = qkv[:, 2 * BRANCH_W + h * GDN_DK:2 * BRANCH_W + (h + 1) * GDN_DK]
            q = q * (lax.rsqrt(jnp.sum(q * q, axis=-1, keepdims=True) + RMS_EPS) * GDN_DK ** -0.5)
            k = k * lax.rsqrt(jnp.sum(k * k, axis=-1, keepdims=True) + RMS_EPS)
            beta = beta_all[:, h:h + 1]
            gcol = gc[:, 4 + h:5 + h]
            grow = gc_t[4 + h:5 + h, :]
            decay = jnp.exp(jnp.where(causal > 0, gcol - grow, -jnp.inf))
            kb = k * beta
            eg = jnp.exp(gcol)
            glast = gc[CH - 1:CH, 4 + h:5 + h]
            a_list.append(strict * (_mm_nt(kb, k) * decay))
            items.append(dict(
                rhs=jnp.concatenate([v * beta, kb * eg], axis=1),
                qk=causal * (_mm_nt(q, k) * decay),
                qd=q * eg,
                kd=k * jnp.exp(glast - gcol),
                gtot=jnp.exp(glast)))
    tinvs = _tri_inverse_many(a_list, msk_ref)
    uws = [_mm(tinv, it["rhs"]) for tinv, it in zip(tinvs, items)]
    for c in range(NCH):
        r0 = c * CH
        for h in range(GDN_HEADS):
            sl = slice(h * GDN_DK, (h + 1) * GDN_DK)
            it = items[c * GDN_HEADS + h]
            uw = uws[c * GDN_HEADS + h]
            s = s_ref[h]
            v_new = uw[:, :GDN_DK] - _mm(uw[:, GDN_DK:], s)
            o = _mm(it["qd"], s) + _mm(it["qk"], v_new)
            s_ref[h] = s * it["gtot"] + _mm_tn(it["kd"], v_new)
            y = _rms_norm(o) * nw * _silu(z_ref[r0:r0 + CH, sl])
            y_ref[r0:r0 + CH, sl] = y.astype(y_ref.dtype)
    xp_ref[0:8, :] = xp_ref[TILE:TILE + 8, :]


def _gdn_call(p, conv_w, lane, norm_w, msk, tri):
    bsz, lp, _ = p.shape
    return pl.pallas_call(
        _gdn_kernel,
        grid=(bsz, lp // TILE),
        in_specs=[
            pl.BlockSpec((None, TILE, 1536), lambda b_, t: (b_, t, OFF_QKV // 1536)),
            pl.BlockSpec((None, TILE, 512), lambda b_, t: (b_, t, OFF_GZ // 512)),
            pl.BlockSpec((None, TILE, 128), lambda b_, t: (b_, t, OFF_BA // 128)),
            _full((CONV_K, 1536)), _full((8, 128)), _full((1, GDN_DK)), _full((9, CH, CH)), _full((CH, CH)),
        ],
        out_specs=pl.BlockSpec((None, TILE, BRANCH_W), lambda b_, t: (b_, t, 0)),
        out_shape=jax.ShapeDtypeStruct((bsz, lp, BRANCH_W), BF),
        scratch_shapes=[pltpu.VMEM((TILE + 8, 1536), F32), pltpu.VMEM((GDN_HEADS, GDN_DK, GDN_DK), F32)],
        compiler_params=_cparams(),
        name="gdn",
    )(p, p, p, conv_w, lane, norm_w, msk, tri)


def _ssd_kernel(xbc_ref, zdt_ref, cw_ref, cb_ref, vec_ref, eyet_ref, maskt_ref, bd_ref, tri_ref, ones_ref,
                y_ref, xp_ref, s_ref, *, pad):
    t = pl.program_id(1)

    @pl.when(t == 0)
    def _():
        xp_ref[0:8, :] = jnp.zeros((8, 1024), F32)
        s_ref[...] = jnp.zeros_like(s_ref)

    xp_ref[8:8 + TILE, :] = xbc_ref[...]
    cw = cw_ref[...]
    cb = cb_ref[...]
    dt_bias = vec_ref[0:1, :]
    neg_a = vec_ref[1:2, :]
    d_skip = vec_ref[2:3, :]
    nw = vec_ref[3:4, :]
    tri = tri_ref[...]
    ones = ones_ref[...]
    gw = M2_HEADDIM * (M2_HEADS // M2_GROUPS)
    for c in range(NCH):
        r0 = c * CH
        xbc = _silu(_conv_chunk(xp_ref, cw, r0) + cb)
        xs = xbc[:, :BRANCH_W]
        row = t * TILE + r0 + lax.broadcasted_iota(jnp.int32, (CH, 1), 0)
        valid = (row >= pad).astype(F32)
        dt = _softplus(zdt_ref[r0:r0 + CH, BRANCH_W:] + dt_bias) * valid
        a = dt * neg_a
        acum = _mm3_left(tri, a)
        xdt = xs * dt
        z = zdt_ref[r0:r0 + CH, :BRANCH_W]
        for g in range(M2_GROUPS):
            gs = slice(g * gw, (g + 1) * gw)
            bg = xbc[:, BRANCH_W + g * M2_DSTATE:BRANCH_W + (g + 1) * M2_DSTATE]
            cg = xbc[:, BRANCH_W + M2_GROUPS * M2_DSTATE + g * M2_DSTATE:
                     BRANCH_W + M2_GROUPS * M2_DSTATE + (g + 1) * M2_DSTATE]
            ag = acum[:, gs]
            cbt = _mm_nt(cg, jnp.concatenate([bg] * 4, axis=0))
            arow = _mm3_left(ones, ag * eyet_ref[...])
            lm = jnp.exp(jnp.where(maskt_ref[...] > 0, ag - arow, -jnp.inf))
            xg = xdt[:, gs]
            xbd = jnp.concatenate([xg] * 4, axis=0) * bd_ref[...]
            y_diag = _mm(cbt * lm, xbd)
            alast = ag[CH - 1:CH, :]
            s = s_ref[g]
            y_off = _mm(cg, s) * jnp.exp(ag)
            s_ref[g] = s * jnp.exp(alast) + _mm_tn(bg, xg * jnp.exp(alast - ag))
            y = (y_diag + y_off + d_skip[:, gs] * xs[:, gs]) * _silu(z[:, gs])
            y = _rms_norm(y) * nw[:, gs]
            y_ref[r0:r0 + CH, gs] = y.astype(y_ref.dtype)
    xp_ref[0:8, :] = xp_ref[TILE:TILE + 8, :]


def _ssd_call(p, conv_w, conv_b, vec, eyet, maskt, bd, tri, ones, pad):
    bsz, lp, _ = p.shape
    return pl.pallas_call(
        functools.partial(_ssd_kernel, pad=pad),
        grid=(bsz, lp // TILE),
        in_specs=[
            pl.BlockSpec((None, TILE, 1024), lambda b_, t: (b_, t, OFF_XBC // 1024)),
            pl.BlockSpec((None, TILE, 1024), lambda b_, t: (b_, t, OFF_MZ // 1024)),
            _full((CONV_K, 1024)), _full((1, 1024)), _full((8, 512)),
            _full((CH, 256)), _full((CH, 256)), _full((256, 256)), _full((CH, CH)), _full((CH, CH)),
        ],
        out_specs=pl.BlockSpec((None, TILE, BRANCH_W), lambda b_, t: (b_, t, 0)),
        out_shape=jax.ShapeDtypeStruct((bsz, lp, BRANCH_W), BF),
        scratch_shapes=[pltpu.VMEM((TILE + 8, 1024), F32), pltpu.VMEM((M2_GROUPS, M2_DSTATE, 256), F32)],
        compiler_params=_cparams(),
        name="ssd",
    )(p, p, conv_w, conv_b, vec, eyet, maskt, bd, tri, ones)


def _hgrn_kernel(p_ref, vec_ref, nw_ref, msk_ref, tri_ref, y_ref, s_ref):
    t = pl.program_id(1)

    @pl.when(t == 0)
    def _():
        s_ref[...] = jnp.zeros_like(s_ref)

    log_lb = vec_ref[0:1, :]
    log1m_lb = vec_ref[1:2, :]
    one_m_lb = vec_ref[2:3, :]
    nw = nw_ref[...]
    causal = msk_ref[0]
    tri = tri_ref[...]
    hsl = [slice(h * HG_DK, (h + 1) * HG_DK) for h in range(HG_HEADS)]
    nsub = CH // SUB
    prep = []
    for c in range(NCH):
        r0 = c * CH
        q = _silu(p_ref[r0:r0 + CH, 0:512])
        zf = p_ref[r0:r0 + CH, 512:1024]
        e = jnp.exp(-jnp.abs(zf))
        lsig = jnp.minimum(zf, 0.0) - jnp.log1p(e)
        bb = log1m_lb + lsig
        logf = jnp.maximum(log_lb, bb) + jnp.log1p(jnp.exp(-jnp.abs(log_lb - bb)))
        sig_neg = jnp.where(zf >= 0, e, 1.0) / (1.0 + e)
        k = one_m_lb * sig_neg
        gcum = _mm3_left(tri, logf)
        glast = gcum[CH - 1:CH, :]
        subs = []
        for i in range(nsub):
            rs = slice(i * SUB, (i + 1) * SUB)
            n = (i + 1) * SUB
            gref = gcum[i * SUB - 1:i * SUB, :] if i > 0 else jnp.zeros((1, BRANCH_W), F32)
            qt = q[rs] * jnp.exp(gcum[rs] - gref)
            kt = k[:n] * jnp.exp(jnp.minimum(gref - gcum[:n], HG_EXP_CLAMP))
            subs.append((qt, kt, qt * jnp.exp(gref)))
        prep.append(dict(kd=k * jnp.exp(glast - gcum), eg=jnp.exp(glast), subs=subs))
    incs = [[_mm_tn(p_ref[c * CH:(c + 1) * CH, 1024 + h * HG_DK:1024 + (h + 1) * HG_DK], prep[c]["kd"][:, hsl[h]])
             for h in range(HG_HEADS)] for c in range(NCH)]
    states = [[s_ref[h] for h in range(HG_HEADS)]]
    for c in range(NCH):
        states.append([states[c][h] * prep[c]["eg"][:, hsl[h]] + incs[c][h] for h in range(HG_HEADS)])
    for h in range(HG_HEADS):
        s_ref[h] = states[NCH][h]
    ams = [[[_mm_nt(prep[c]["subs"][i][0][:, hsl[h]], prep[c]["subs"][i][1][:, hsl[h]])
             * causal[i * SUB:(i + 1) * SUB, :(i + 1) * SUB]
             for h in range(HG_HEADS)] for i in range(nsub)] for c in range(NCH)]
    for c in range(NCH):
        r0 = c * CH
        for h in range(HG_HEADS):
            iv = p_ref[r0:r0 + CH, 1024 + h * HG_DK:1024 + (h + 1) * HG_DK]
            o = jnp.concatenate(
                [_mm(ams[c][i][h], iv[:(i + 1) * SUB]) + _mm_nt(prep[c]["subs"][i][2][:, hsl[h]], states[c][h])
                 for i in range(nsub)], axis=0)
            z = p_ref[r0:r0 + CH, 1536 + h * HG_DK:1536 + (h + 1) * HG_DK]
            y_ref[r0:r0 + CH, hsl[h]] = (_rms_norm(o) * nw * _silu(z)).astype(y_ref.dtype)


def _hgrn_call(p, vec, norm_w, msk, tri):
    bsz, lp, _ = p.shape
    return pl.pallas_call(
        _hgrn_kernel,
        grid=(bsz, lp // TILE),
        in_specs=[
            pl.BlockSpec((None, TILE, 2048), lambda b_, t: (b_, t, OFF_HG // 2048)),
            _full((8, 512)), _full((1, HG_DK)), _full((9, CH, CH)), _full((CH, CH)),
        ],
        out_specs=pl.BlockSpec((None, TILE, BRANCH_W), lambda b_, t: (b_, t, 0)),
        out_shape=jax.ShapeDtypeStruct((bsz, lp, BRANCH_W), BF),
        scratch_shapes=[pltpu.VMEM((HG_HEADS, HG_DK, HG_DK), F32)],
        compiler_params=_cparams(),
        name="hgrn2",
    )(p, vec, norm_w, msk, tri)


def _s5_kernel(p_ref, bw_ref, cre_ref, cim_ref, pw_ref, d_ref, w1_ref, w2_ref, y_ref, xr_ref, xi_ref, c_ref):
    t = pl.program_id(1)

    @pl.when(t == 0)
    def _():
        c_ref[...] = jnp.zeros_like(c_ref)

    u = p_ref[:, 0:512]
    ub = u.astype(BF)
    for j in range(4):
        bu = jnp.dot(ub[:, j * 128:(j + 1) * 128], bw_ref[j], preferred_element_type=F32)
        xr_ref[:, j * 512:(j + 1) * 512] = bu[:, :512]
        xi_ref[:, j * 512:(j + 1) * 512] = bu[:, 512:]

    rowid = lax.broadcasted_iota(jnp.int32, (8, 512), 0)
    for cg in range(4):
        cs = slice(cg * 512, (cg + 1) * 512)
        l8r = pw_ref[0:8, cs]
        l8i = pw_ref[8:16, cs]
        steps = [(d, l8r[d - 1:d, :], l8i[d - 1:d, :]) for d in (1, 2, 4)]

        def body(b, carry, cs=cs, l8r=l8r, l8i=l8i, steps=steps):
            cr, ci = carry
            rows = pl.ds(pl.multiple_of(b * 8, 8), 8)
            r = xr_ref[rows, cs]
            im = xi_ref[rows, cs]
            for d, lr, li in steps:
                rsh = jnp.where(rowid >= d, pltpu.roll(r, d, 0), 0.0)
                ish = jnp.where(rowid >= d, pltpu.roll(im, d, 0), 0.0)
                r, im = r + (lr * rsh - li * ish), im + (lr * ish + li * rsh)
            r, im = r + (l8r * cr - l8i * ci), im + (l8r * ci + l8i * cr)
            xr_ref[rows, cs] = r
            xi_ref[rows, cs] = im
            return r[7:8, :], im[7:8, :]

        cr, ci = lax.fori_loop(0, TILE // 8, body, (c_ref[0:1, cs], c_ref[1:2, cs]))
        c_ref[0:1, cs] = cr
        c_ref[1:2, cs] = ci

    ys = []
    for j in range(4):
        cs = slice(j * 512, (j + 1) * 512)
        ys.append(_mm(xr_ref[:, cs], cre_ref[j]) - _mm(xi_ref[:, cs], cim_ref[j]))
    y = jnp.concatenate(ys, axis=1) + d_ref[...] * u
    y = 0.5 * y * (1.0 + jnp.tanh(math.sqrt(2.0 / math.pi) * (y + 0.044715 * (y * y * y))))
    yb = y.astype(BF)
    glu = jnp.dot(yb, w1_ref[...], preferred_element_type=F32) * _sigmoid(
        jnp.dot(yb, w2_ref[...], preferred_element_type=F32))
    y_ref[...] = (glu * _silu(p_ref[:, 512:1024])).astype(y_ref.dtype)


def _s5_call(p, bw, cre, cim, pw, dvec, w1, w2):
    bsz, lp, _ = p.shape
    return pl.pallas_call(
        _s5_kernel,
        grid=(bsz, lp // TILE),
        in_specs=[
            pl.BlockSpec((None, TILE, 1024), lambda b_, t: (b_, t, OFF_S5 // 1024)),
            _full((4, 128, 1024)), _full((4, 512, 128)), _full((4, 512, 128)), _full((16, S5_STATE)),
            _full((1, 512)), _full((512, 512)), _full((512, 512)),
        ],
        out_specs=pl.BlockSpec((None, TILE, BRANCH_W), lambda b_, t: (b_, t, 0)),
        out_shape=jax.ShapeDtypeStruct((bsz, lp, BRANCH_W), BF),
        scratch_shapes=[pltpu.VMEM((TILE, S5_STATE), F32), pltpu.VMEM((TILE, S5_STATE), F32),
                        pltpu.VMEM((2, S5_STATE), F32)],
        compiler_params=_cparams(),
        name="s5",
    )(p, bw, cre, cim, pw, dvec, w1, w2)


def _merge_kernel(h_ref, ya_ref, yb_ref, yc_ref, yd_ref, wg_ref, wb_ref, wo_ref, g_ref, b_ref, o_ref, *, pad):
    t = pl.program_id(1)
    h = h_ref[...]
    hb = h.astype(BF)
    mixed = jnp.zeros((TILE, D_MODEL), F32)
    for br, y_ref in enumerate((ya_ref, yb_ref, yc_ref, yd_ref)):
        gate = _sigmoid(jnp.dot(hb, wg_ref[:, br * D_MODEL:(br + 1) * D_MODEL], preferred_element_type=F32))
        mixed = mixed + gate * jnp.dot(y_ref[...], wb_ref[br], preferred_element_type=F32)
    out = jnp.dot(mixed.astype(BF), wo_ref[...], preferred_element_type=F32)
    r = ALPHA * h + out
    mu = jnp.mean(r, axis=-1, keepdims=True)
    rc = r - mu
    var = jnp.mean(rc * rc, axis=-1, keepdims=True)
    y = rc * lax.rsqrt(var + LN_EPS) * g_ref[...] + b_ref[...]
    row = t * TILE + lax.broadcasted_iota(jnp.int32, (TILE, 1), 0)
    o_ref[...] = jnp.where(row >= pad, y, 0.0)


def _merge_call(h, ya, yb, yc, yd, wg, wb, wo, g, b, pad):
    bsz, lp, d = h.shape
    tok = lambda w: pl.BlockSpec((None, TILE, w), lambda b_, t: (b_, t, 0))
    return pl.pallas_call(
        functools.partial(_merge_kernel, pad=pad),
        grid=(bsz, lp // TILE),
        in_specs=[tok(d), tok(BRANCH_W), tok(BRANCH_W), tok(BRANCH_W), tok(BRANCH_W),
                  _full((d, N_BRANCH * d)), _full((N_BRANCH, BRANCH_W, d)), _full((d, d)),
                  _full((1, d)), _full((1, d))],
        out_specs=tok(d),
        out_shape=jax.ShapeDtypeStruct((bsz, lp, d), F32),
        compiler_params=_cparams(),
        name="merge",
    )(h, ya, yb, yc, yd, wg, wb, wo, g.reshape(1, d), b.reshape(1, d))


def _split_w_in(w):
    sizes = (1536, 512, 4, 4, 1024, 512, 8, 512, 512, 512, 512, 512, 512, 4096)
    offs = [0]
    for s in sizes:
        offs.append(offs[-1] + s)
    return [w[:, offs[i]:offs[i + 1]] for i in range(len(sizes))]


def _prep_w1(w):
    (qkv, gz, gb, ga, xbc, mz, mdt, hq, hf, hi, hz, su, sz, gates) = _split_w_in(w)
    d = w.shape[0]
    ba = jnp.concatenate([gb, ga, jnp.zeros((d, 120), w.dtype)], axis=1)
    mdt_e = jnp.repeat(mdt, M2_HEADDIM, axis=1)
    w1 = jnp.concatenate([qkv, gz, xbc, mz, mdt_e, hq, hf, hi, hz, su, sz, ba], axis=1)
    return w1.astype(BF), gates.astype(BF)


def _lane_row(vals, off, width=128):
    return jnp.zeros((width,), F32).at[off:off + vals.shape[0]].set(vals.astype(F32))


def _prep_s5(a_re, a_im, b_re, b_im, c_re, c_im, log_dt):
    f = lambda v: v.astype(F32)
    a_re, a_im, b_re, b_im, c_re, c_im = map(f, (a_re, a_im, b_re, b_im, c_re, c_im))
    dt = jnp.exp(f(log_dt))[:, None]
    mag = jnp.exp(a_re * dt)
    lam_re, lam_im = mag * jnp.cos(a_im * dt), mag * jnp.sin(a_im * dt)
    den = jnp.square(a_re) + jnp.square(a_im)
    nr, ni = lam_re - 1.0, lam_im
    z_re, z_im = (nr * a_re + ni * a_im) / den, (ni * a_re - nr * a_im) / den
    bb_re = z_re[..., None] * b_re - z_im[..., None] * b_im
    bb_im = z_re[..., None] * b_im + z_im[..., None] * b_re
    eye8 = jnp.eye(8, dtype=F32)

    def in_blocks(bb):
        blk = jnp.transpose(bb, (0, 2, 1)).reshape(4, 8, S5_GROUP, S5_P)
        return jnp.einsum('jicp,ik->jickp', blk, eye8).reshape(4, 128, 512)

    def out_blocks(cc):
        blk = jnp.transpose(cc, (0, 2, 1)).reshape(4, 8, S5_P, S5_GROUP)
        return jnp.einsum('jipc,ik->jipkc', blk, eye8).reshape(4, 512, 128)

    bw = jnp.concatenate([in_blocks(bb_re), in_blocks(bb_im)], axis=2).astype(BF)
    cre = out_blocks(c_re).astype(BF)
    cim = out_blocks(c_im).astype(BF)
    pw = jnp.arange(1, 9, dtype=F32)[:, None, None]
    pmag = jnp.exp(a_re[None] * dt[None] * pw)
    pang = a_im[None] * dt[None] * pw
    pw_tab = jnp.concatenate([(pmag * jnp.cos(pang)).reshape(8, S5_STATE),
                              (pmag * jnp.sin(pang)).reshape(8, S5_STATE)], axis=0)
    return bw, cre, cim, pw_tab


@jax.jit
def kernel(x, meta_tokens, ln_in_g, ln_in_b, w_in, gdn_conv_w, gdn_A_log, gdn_dt_bias, gdn_norm_w, m2_conv_w, m2_conv_b, m2_dt_bias, m2_A_log, m2_D, m2_norm_w, hg_lb_logits, hg_norm_w, s5_A_re, s5_A_im, s5_B_re, s5_B_im, s5_C_re, s5_C_im, s5_D, s5_log_dt, s5_glu_w1, s5_glu_w2, w_branch, w_out, ln_g, ln_b):
    bsz, seq, d = x.shape
    ltot = N_META + seq
    lp = -(-ltot // TILE) * TILE
    pad = lp - ltot
    meta = jnp.broadcast_to(meta_tokens.astype(x.dtype)[None], (bsz, N_META, d))
    xin = jnp.concatenate([jnp.zeros((bsz, pad, d), x.dtype), meta, x], axis=1)
    h = _ln_in_call(xin, ln_in_g.astype(F32), ln_in_b.astype(F32), pad)

    msk = _masks64()
    tri = msk[0].astype(BF)
    ones = jnp.ones((CH, CH), BF)
    jj = jnp.arange(256)
    eyet = (jnp.arange(CH)[:, None] == (jj % CH)[None, :]).astype(F32)
    maskt = (jnp.arange(CH)[:, None] >= (jj % CH)[None, :]).astype(F32)
    bd = ((jj // CH)[:, None] == (jj // CH)[None, :]).astype(F32)

    cum = jnp.cumsum(jax.nn.softmax(hg_lb_logits.astype(F32), axis=0), axis=0)
    lower_bounds = cum - cum[0:1]

    for l in range(w_in.shape[0]):
        w1, wg = _prep_w1(w_in[l])
        p = _proj_call(h, w1)

        lane = jnp.stack([_lane_row(jnp.exp(gdn_A_log[l].astype(F32)), 4), _lane_row(gdn_dt_bias[l], 4)]
                         + [jnp.zeros((128,), F32)] * 6)
        ya = _gdn_call(p, gdn_conv_w[l].astype(F32), lane, gdn_norm_w[l].astype(F32).reshape(1, GDN_DK), msk, tri)

        rep = lambda v: jnp.repeat(v.astype(F32), M2_HEADDIM)
        vec = jnp.stack([rep(m2_dt_bias[l]), rep(-jnp.exp(m2_A_log[l].astype(F32))), rep(m2_D[l]),
                         m2_norm_w[l].astype(F32)] + [jnp.zeros((BRANCH_W,), F32)] * 4)
        yb = _ssd_call(p, m2_conv_w[l].astype(F32), m2_conv_b[l].astype(F32).reshape(1, -1), vec,
                       eyet, maskt, bd, tri, ones, pad)

        lb = lower_bounds[l]
        hvec = jnp.stack([jnp.log(lb), jnp.log1p(-lb), 1.0 - lb] + [jnp.zeros((BRANCH_W,), F32)] * 5)
        yc = _hgrn_call(p, hvec, hg_norm_w[l].astype(F32).reshape(1, HG_DK), msk, tri)

        bw, cre, cim, pw_tab = _prep_s5(s5_A_re[l], s5_A_im[l], s5_B_re[l], s5_B_im[l], s5_C_re[l], s5_C_im[l],
                                        s5_log_dt[l])
        yd = _s5_call(p, bw, cre, cim, pw_tab, s5_D[l].astype(F32).reshape(1, -1),
                      s5_glu_w1[l].astype(BF), s5_glu_w2[l].astype(BF))

        h = _merge_call(h, ya, yb, yc, yd, wg, w_branch[l].astype(BF), w_out[l].astype(BF),
                        ln_g[l].astype(F32), ln_b[l].astype(F32), pad)
    return h[:, pad + N_META:]
```

```python
import functools
import math

import jax
import jax.numpy as jnp
from jax import lax
from jax.experimental import pallas as pl
from jax.experimental.pallas import tpu as pltpu

F32 = jnp.float32
BF = jnp.bfloat16

D_MODEL = 1024
DEPTH = 4
N_META = 16
CONV_K = 4
N_BRANCH = 4
BRANCH_W = 512
GDN_HEADS = 4
GDN_DK = 128
M2_HEADS = 8
M2_HEADDIM = 64
M2_GROUPS = 2
M2_DSTATE = 128
HG_HEADS = 4
HG_DK = 128
S5_GROUP = 16
S5_NG = 32
S5_P = 64
S5_STATE = S5_NG * S5_P
ALPHA = (2 * DEPTH) ** 0.25
LN_EPS = 1e-5
RMS_EPS = 1e-6

TILE = 256
CH = 64
NCH = TILE // CH
SUB = 16
HG_EXP_CLAMP = 80.0
VMEM_LIMIT = 56 * 1024 * 1024

OFF_QKV, OFF_GZ, OFF_XBC, OFF_MZ, OFF_MDT = 0, 1536, 2048, 3072, 3584
OFF_HG, OFF_BA, OFF_S5 = 4096, 6144, 6272
NP = 6272
N1 = 7296


def _sigmoid(x):
    return 0.5 * (jnp.tanh(0.5 * x) + 1.0)


def _silu(x):
    return x * _sigmoid(x)


def _softplus(x):
    return jnp.maximum(x, 0.0) + jnp.log1p(jnp.exp(-jnp.abs(x)))


def _mm(a, b):
    return jnp.dot(a.astype(BF), b.astype(BF), preferred_element_type=F32)


def _mm_nt(a, b):
    return lax.dot_general(a.astype(BF), b.astype(BF), (((1,), (1,)), ((), ())),
                           preferred_element_type=F32)


def _mm_tn(a, b):
    return lax.dot_general(a.astype(BF), b.astype(BF), (((0,), (0,)), ((), ())),
                           preferred_element_type=F32)


def _mm3_left(m_bf, x):
    hi = x.astype(BF)
    r1 = x - hi.astype(F32)
    mid = r1.astype(BF)
    lo = (r1 - mid.astype(F32)).astype(BF)
    d = lambda v: jnp.dot(m_bf, v, preferred_element_type=F32)
    return (d(lo) + d(mid)) + d(hi)


def _rms_norm(x):
    return x * lax.rsqrt(jnp.mean(x * x, axis=-1, keepdims=True) + RMS_EPS)


def _masks64():
    i = jnp.arange(CH)[:, None]
    j = jnp.arange(CH)[None, :]
    ms = [(i >= j), (i > j)]
    for s in (1, 2, 4, 8, 16, 32):
        ms.append((i // (2 * s) == j // (2 * s)) & (i % (2 * s) >= s) & (j % (2 * s) < s))
    ms.append(i == j)
    return jnp.stack(ms).astype(F32)


def _cparams():
    return pltpu.CompilerParams(dimension_semantics=("arbitrary", "arbitrary"),
                                vmem_limit_bytes=VMEM_LIMIT)


def _full(shape):
    n = len(shape)
    return pl.BlockSpec(shape, lambda b, t: (0,) * n, pipeline_mode=pl.Buffered(1))


def _ln_in_kernel(x_ref, g_ref, b_ref, o_ref, *, pad):
    t = pl.program_id(1)
    x = x_ref[...]
    mu = jnp.mean(x, axis=-1, keepdims=True)
    xc = x - mu
    var = jnp.mean(xc * xc, axis=-1, keepdims=True)
    y = xc * lax.rsqrt(var + LN_EPS) * g_ref[...] + b_ref[...]
    row = t * TILE + lax.broadcasted_iota(jnp.int32, (TILE, 1), 0)
    o_ref[...] = jnp.where(row >= pad, y, 0.0)


def _ln_in_call(xin, g, b, pad):
    bsz, lp, d = xin.shape
    return pl.pallas_call(
        functools.partial(_ln_in_kernel, pad=pad),
        grid=(bsz, lp // TILE),
        in_specs=[pl.BlockSpec((None, TILE, d), lambda b_, t: (b_, t, 0)), _full((1, d)), _full((1, d))],
        out_specs=pl.BlockSpec((None, TILE, d), lambda b_, t: (b_, t, 0)),
        out_shape=jax.ShapeDtypeStruct((bsz, lp, d), F32),
        compiler_params=_cparams(),
        name="ln_in",
    )(xin, g.reshape(1, d), b.reshape(1, d))


S5_CG = 512
S5_NBLK = TILE // 8
PROJ_CHUNK = 512


def _s5_scan_block(b, cg, carry, xr_ref, xi_ref, pw_ref):
    rows = slice(8 * b, 8 * b + 8)
    cs = slice(cg * S5_CG, (cg + 1) * S5_CG)
    cr, ci = carry
    r = xr_ref[rows, cs]
    im = xi_ref[rows, cs]
    for n, d in enumerate((1, 2, 4)):
        lr = pw_ref[16 + 16 * n:24 + 16 * n, cs]
        li = pw_ref[24 + 16 * n:32 + 16 * n, cs]
        rsh = pltpu.roll(r, d, 0)
        ish = pltpu.roll(im, d, 0)
        r, im = r + (lr * rsh - li * ish), im + (lr * ish + li * rsh)
    l8r = pw_ref[0:8, cs]
    l8i = pw_ref[8:16, cs]
    r, im = r + (l8r * cr - l8i * ci), im + (l8r * ci + l8i * cr)
    xr_ref[rows, cs] = r
    xi_ref[rows, cs] = im
    return r[7:8, :], im[7:8, :]


def _proj_s5_kernel(h_ref, w_ref, bw_ref, cre_ref, cim_ref, pw_ref, d_ref, g1_ref, g2_ref,
                    p_ref, yd_ref, xr_ref, xi_ref, c_ref):
    t = pl.program_id(1)

    @pl.when(t == 0)
    def _():
        c_ref[...] = jnp.zeros_like(c_ref)

    hb = h_ref[...].astype(BF)
    u = jnp.dot(hb, w_ref[:, OFF_S5:OFF_S5 + BRANCH_W], preferred_element_type=F32)
    ub = u.astype(BF)
    for j in range(4):
        bu = jnp.dot(ub[:, j * 128:(j + 1) * 128], bw_ref[j], preferred_element_type=F32)
        xr_ref[:, j * 512:(j + 1) * 512] = bu[:, :512]
        xi_ref[:, j * 512:(j + 1) * 512] = bu[:, 512:]

    ncg = S5_STATE // S5_CG
    carries = [(c_ref[0:1, cg * S5_CG:(cg + 1) * S5_CG], c_ref[1:2, cg * S5_CG:(cg + 1) * S5_CG])
               for cg in range(ncg)]
    work = [(b, cg) for b in range(S5_NBLK) for cg in range(ncg)]
    starts = list(range(0, NP, PROJ_CHUNK))
    per = -(-len(work) // len(starts))
    for n, j in enumerate(starts):
        w = min(PROJ_CHUNK, NP - j)
        p_ref[:, j:j + w] = jnp.dot(hb, w_ref[:, j:j + w], preferred_element_type=F32)
        for b, cg in work[n * per:(n + 1) * per]:
            carries[cg] = _s5_scan_block(b, cg, carries[cg], xr_ref, xi_ref, pw_ref)
    for cg in range(ncg):
        c_ref[0:1, cg * S5_CG:(cg + 1) * S5_CG] = carries[cg][0]
        c_ref[1:2, cg * S5_CG:(cg + 1) * S5_CG] = carries[cg][1]

    ys = []
    for j in range(4):
        cs = slice(j * 512, (j + 1) * 512)
        ys.append(_mm(xr_ref[:, cs], cre_ref[j]) - _mm(xi_ref[:, cs], cim_ref[j]))
    y = jnp.concatenate(ys, axis=1) + d_ref[...] * u
    y = 0.5 * y * (1.0 + jnp.tanh(math.sqrt(2.0 / math.pi) * (y + 0.044715 * (y * y * y))))
    yb = y.astype(BF)
    glu = jnp.dot(yb, g1_ref[...], preferred_element_type=F32) * _sigmoid(
        jnp.dot(yb, g2_ref[...], preferred_element_type=F32))
    z = jnp.dot(hb, w_ref[:, OFF_S5 + BRANCH_W:OFF_S5 + 2 * BRANCH_W], preferred_element_type=F32)
    yd_ref[...] = (glu * _silu(z)).astype(yd_ref.dtype)


def _proj_s5_call(h, w1, bw, cre, cim, pw, dvec, g1, g2):
    bsz, lp, d = h.shape
    return pl.pallas_call(
        _proj_s5_kernel,
        grid=(bsz, lp // TILE),
        in_specs=[pl.BlockSpec((None, TILE, d), lambda b_, t: (b_, t, 0)), _full((d, N1)),
                  _full((4, 128, 1024)), _full((4, 512, 128)), _full((4, 512, 128)), _full((64, S5_STATE)),
                  _full((1, 512)), _full((512, 512)), _full((512, 512))],
        out_specs=[pl.BlockSpec((None, TILE, NP), lambda b_, t: (b_, t, 0)),
                   pl.BlockSpec((None, TILE, BRANCH_W), lambda b_, t: (b_, t, 0))],
        out_shape=[jax.ShapeDtypeStruct((bsz, lp, NP), F32), jax.ShapeDtypeStruct((bsz, lp, BRANCH_W), BF)],
        scratch_shapes=[pltpu.VMEM((TILE, S5_STATE), F32), pltpu.VMEM((TILE, S5_STATE), F32),
                        pltpu.VMEM((2, S5_STATE), F32)],
        compiler_params=_cparams(),
        name="proj_s5",
    )(h, w1, bw, cre, cim, pw, dvec, g1, g2)


def _conv_chunk(xp_ref, cw, r0):
    x = xp_ref[r0 + 8:r0 + 8 + CH, :] * cw[CONV_K - 1:CONV_K, :]
    for j in range(CONV_K - 1):
        s = r0 + 8 - (CONV_K - 1) + j
        x = x + xp_ref[s:s + CH, :] * cw[j:j + 1, :]
    return x


def _tri_inverse_many(a_list, msk_ref):
    xs = [msk_ref[8] - a * msk_ref[2] for a in a_list]
    for lvl in range(1, 6):
        m = msk_ref[2 + lvl]
        t1 = [_mm(x, a * m) for x, a in zip(xs, a_list)]
        xs = [x - _mm(t, x) for x, t in zip(xs, t1)]
    return xs


def _gdn_kernel(qkv_ref, z_ref, ba_ref, cw_ref, lane_ref, nw_ref, msk_ref, tri_ref, y_ref, xp_ref, s_ref):
    t = pl.program_id(1)

    @pl.when(t == 0)
    def _():
        xp_ref[0:8, :] = jnp.zeros((8, 3 * BRANCH_W), F32)
        s_ref[...] = jnp.zeros_like(s_ref)

    xp_ref[8:8 + TILE, :] = qkv_ref[...]
    cw = cw_ref[...]
    exp_a = lane_ref[0:1, :]
    dt_bias = lane_ref[1:2, :]
    nw = nw_ref[...]
    causal = msk_ref[0]
    strict = msk_ref[1]
    tri = tri_ref[...]
    items = []
    a_list = []
    for c in range(NCH):
        r0 = c * CH
        qkv = _silu(_conv_chunk(xp_ref, cw, r0))
        ba = ba_ref[r0:r0 + CH, :]
        beta_all = _sigmoid(ba)
        gl = -(exp_a * _softplus(ba + dt_bias))
        gc = _mm3_left(tri, gl)
        gc_t = gc.T
        for h in range(GDN_HEADS):
            sl = slice(h * GDN_DK, (h + 1) * GDN_DK)
            q = qkv[:, sl]
            k = qkv[:, BRANCH_W + h * GDN_DK:BRANCH_W + (h + 1) * GDN_DK]
            v = qkv[:, 2 * BRANCH_W + h * GDN_DK:2 * BRANCH_W + (h + 1) * GDN_DK]
            q = q * (lax.rsqrt(jnp.sum(q * q, axis=-1, keepdims=True) + RMS_EPS) * GDN_DK ** -0.5)
            k = k * lax.rsqrt(jnp.sum(k * k, axis=-1, keepdims=True) + RMS_EPS)
            beta = beta_all[:, h:h + 1]
            gcol = gc[:, 4 + h:5 + h]
            grow = gc_t[4 + h:5 + h, :]
            decay = jnp.exp(jnp.where(causal > 0, gcol - grow, -jnp.inf))
            kb = k * beta
            eg = jnp.exp(gcol)
            glast = gc[CH - 1:CH, 4 + h:5 + h]
            a_list.append(strict * (_mm_nt(kb, k) * decay))
            items.append(dict(
                rhs=jnp.concatenate([v * beta, kb * eg], axis=1),
                qk=causal * (_mm_nt(q, k) * decay),
                qd=q * eg,
                kd=k * jnp.exp(glast - gcol),
                gtot=jnp.exp(glast)))
    tinvs = _tri_inverse_many(a_list, msk_ref)
    uws = [_mm(tinv, it["rhs"]) for tinv, it in zip(tinvs, items)]
    for c in range(NCH):
        r0 = c * CH
        for h in range(GDN_HEADS):
            sl = slice(h * GDN_DK, (h + 1) * GDN_DK)
            it = items[c * GDN_HEADS + h]
            uw = uws[c * GDN_HEADS + h]
            s = s_ref[h]
            v_new = uw[:, :GDN_DK] - _mm(uw[:, GDN_DK:], s)
            o = _mm(it["qd"], s) + _mm(it["qk"], v_new)
            s_ref[h] = s * it["gtot"] + _mm_tn(it["kd"], v_new)
            y = _rms_norm(o) * nw * _silu(z_ref[r0:r0 + CH, sl])
            y_ref[r0:r0 + CH, sl] = y.astype(y_ref.dtype)
    xp_ref[0:8, :] = xp_ref[TILE:TILE + 8, :]


def _gdn_call(p, conv_w, lane, norm_w, msk, tri):
    bsz, lp, _ = p.shape
    return pl.pallas_call(
        _gdn_kernel,
        grid=(bsz, lp // TILE),
        in_specs=[
            pl.BlockSpec((None, TILE, 1536), lambda b_, t: (b_, t, OFF_QKV // 1536)),
            pl.BlockSpec((None, TILE, 512), lambda b_, t: (b_, t, OFF_GZ // 512)),
            pl.BlockSpec((None, TILE, 128), lambda b_, t: (b_, t, OFF_BA // 128)),
            _full((CONV_K, 1536)), _full((8, 128)), _full((1, GDN_DK)), _full((9, CH, CH)), _full((CH, CH)),
        ],
        out_specs=pl.BlockSpec((None, TILE, BRANCH_W), lambda b_, t: (b_, t, 0)),
        out_shape=jax.ShapeDtypeStruct((bsz, lp, BRANCH_W), BF),
        scratch_shapes=[pltpu.VMEM((TILE + 8, 1536), F32), pltpu.VMEM((GDN_HEADS, GDN_DK, GDN_DK), F32)],
        compiler_params=_cparams(),
        name="gdn",
    )(p, p, p, conv_w, lane, norm_w, msk, tri)


def _ssd_kernel(xbc_ref, zdt_ref, cw_ref, cb_ref, vec_ref, eyet_ref, maskt_ref, bd_ref, tri_ref, ones_ref,
                y_ref, xp_ref, s_ref, *, pad):
    t = pl.program_id(1)

    @pl.when(t == 0)
    def _():
        xp_ref[0:8, :] = jnp.zeros((8, 1024), F32)
        s_ref[...] = jnp.zeros_like(s_ref)

    xp_ref[8:8 + TILE, :] = xbc_ref[...]
    cw = cw_ref[...]
    cb = cb_ref[...]
    dt_bias = vec_ref[0:1, :]
    neg_a = vec_ref[1:2, :]
    d_skip = vec_ref[2:3, :]
    nw = vec_ref[3:4, :]
    tri = tri_ref[...]
    ones = ones_ref[...]
    gw = M2_HEADDIM * (M2_HEADS // M2_GROUPS)
    for c in range(NCH):
        r0 = c * CH
        xbc = _silu(_conv_chunk(xp_ref, cw, r0) + cb)
        xs = xbc[:, :BRANCH_W]
        row = t * TILE + r0 + lax.broadcasted_iota(jnp.int32, (CH, 1), 0)
        valid = (row >= pad).astype(F32)
        dt = _softplus(zdt_ref[r0:r0 + CH, BRANCH_W:] + dt_bias) * valid
        a = dt * neg_a
        acum = _mm3_left(tri, a)
        xdt = xs * dt
        z = zdt_ref[r0:r0 + CH, :BRANCH_W]
        for g in range(M2_GROUPS):
            gs = slice(g * gw, (g + 1) * gw)
            bg = xbc[:, BRANCH_W + g * M2_DSTATE:BRANCH_W + (g + 1) * M2_DSTATE]
            cg = xbc[:, BRANCH_W + M2_GROUPS * M2_DSTATE + g * M2_DSTATE:
                     BRANCH_W + M2_GROUPS * M2_DSTATE + (g + 1) * M2_DSTATE]
            ag = acum[:, gs]
            cbt = _mm_nt(cg, jnp.concatenate([bg] * 4, axis=0))
            arow = _mm3_left(ones, ag * eyet_ref[...])
            lm = jnp.exp(jnp.where(maskt_ref[...] > 0, ag - arow, -jnp.inf))
            xg = xdt[:, gs]
            xbd = jnp.concatenate([xg] * 4, axis=0) * bd_ref[...]
            y_diag = _mm(cbt * lm, xbd)
            alast = ag[CH - 1:CH, :]
            s = s_ref[g]
            y_off = _mm(cg, s) * jnp.exp(ag)
            s_ref[g] = s * jnp.exp(alast) + _mm_tn(bg, xg * jnp.exp(alast - ag))
            y = (y_diag + y_off + d_skip[:, gs] * xs[:, gs]) * _silu(z[:, gs])
            y = _rms_norm(y) * nw[:, gs]
            y_ref[r0:r0 + CH, gs] = y.astype(y_ref.dtype)
    xp_ref[0:8, :] = xp_ref[TILE:TILE + 8, :]


def _ssd_call(p, conv_w, conv_b, vec, eyet, maskt, bd, tri, ones, pad):
    bsz, lp, _ = p.shape
    return pl.pallas_call(
        functools.partial(_ssd_kernel, pad=pad),
        grid=(bsz, lp // TILE),
        in_specs=[
            pl.BlockSpec((None, TILE, 1024), lambda b_, t: (b_, t, OFF_XBC // 1024)),
            pl.BlockSpec((None, TILE, 1024), lambda b_, t: (b_, t, OFF_MZ // 1024)),
            _full((CONV_K, 1024)), _full((1, 1024)), _full((8, 512)),
            _full((CH, 256)), _full((CH, 256)), _full((256, 256)), _full((CH, CH)), _full((CH, CH)),
        ],
        out_specs=pl.BlockSpec((None, TILE, BRANCH_W), lambda b_, t: (b_, t, 0)),
        out_shape=jax.ShapeDtypeStruct((bsz, lp, BRANCH_W), BF),
        scratch_shapes=[pltpu.VMEM((TILE + 8, 1024), F32), pltpu.VMEM((M2_GROUPS, M2_DSTATE, 256), F32)],
        compiler_params=_cparams(),
        name="ssd",
    )(p, p, conv_w, conv_b, vec, eyet, maskt, bd, tri, ones)


def _hgrn_kernel(p_ref, vec_ref, nw_ref, msk_ref, tri_ref, y_ref, s_ref):
    t = pl.program_id(1)

    @pl.when(t == 0)
    def _():
        s_ref[...] = jnp.zeros_like(s_ref)

    log_lb = vec_ref[0:1, :]
    log1m_lb = vec_ref[1:2, :]
    one_m_lb = vec_ref[2:3, :]
    nw = nw_ref[...]
    causal = msk_ref[0]
    tri = tri_ref[...]
    hsl = [slice(h * HG_DK, (h + 1) * HG_DK) for h in range(HG_HEADS)]
    nsub = CH // SUB
    prep = []
    for c in range(NCH):
        r0 = c * CH
        q = _silu(p_ref[r0:r0 + CH, 0:512])
        zf = p_ref[r0:r0 + CH, 512:1024]
        e = jnp.exp(-jnp.abs(zf))
        lsig = jnp.minimum(zf, 0.0) - jnp.log1p(e)
        bb = log1m_lb + lsig
        logf = jnp.maximum(log_lb, bb) + jnp.log1p(jnp.exp(-jnp.abs(log_lb - bb)))
        sig_neg = jnp.where(zf >= 0, e, 1.0) / (1.0 + e)
        k = one_m_lb * sig_neg
        gcum = _mm3_left(tri, logf)
        glast = gcum[CH - 1:CH, :]
        subs = []
        for i in range(nsub):
            rs = slice(i * SUB, (i + 1) * SUB)
            n = (i + 1) * SUB
            gref = gcum[i * SUB - 1:i * SUB, :] if i > 0 else jnp.zeros((1, BRANCH_W), F32)
            qt = q[rs] * jnp.exp(gcum[rs] - gref)
            kt = k[:n] * jnp.exp(jnp.minimum(gref - gcum[:n], HG_EXP_CLAMP))
            subs.append((qt, kt, qt * jnp.exp(gref)))
        prep.append(dict(kd=k * jnp.exp(glast - gcum), eg=jnp.exp(glast), subs=subs))
    incs = [[_mm_tn(p_ref[c * CH:(c + 1) * CH, 1024 + h * HG_DK:1024 + (h + 1) * HG_DK], prep[c]["kd"][:, hsl[h]])
             for h in range(HG_HEADS)] for c in range(NCH)]
    states = [[s_ref[h] for h in range(HG_HEADS)]]
    for c in range(NCH):
        states.append([states[c][h] * prep[c]["eg"][:, hsl[h]] + incs[c][h] for h in range(HG_HEADS)])
    for h in range(HG_HEADS):
        s_ref[h] = states[NCH][h]
    ams = [[[_mm_nt(prep[c]["subs"][i][0][:, hsl[h]], prep[c]["subs"][i][1][:, hsl[h]])
             * causal[i * SUB:(i + 1) * SUB, :(i + 1) * SUB]
             for h in range(HG_HEADS)] for i in range(nsub)] for c in range(NCH)]
    for c in range(NCH):
        r0 = c * CH
        for h in range(HG_HEADS):
            iv = p_ref[r0:r0 + CH, 1024 + h * HG_DK:1024 + (h + 1) * HG_DK]
            o = jnp.concatenate(
                [_mm(ams[c][i][h], iv[:(i + 1) * SUB]) + _mm_nt(prep[c]["subs"][i][2][:, hsl[h]], states[c][h])
                 for i in range(nsub)], axis=0)
            z = p_ref[r0:r0 + CH, 1536 + h * HG_DK:1536 + (h + 1) * HG_DK]
            y_ref[r0:r0 + CH, hsl[h]] = (_rms_norm(o) * nw * _silu(z)).astype(y_ref.dtype)


def _hgrn_call(p, vec, norm_w, msk, tri):
    bsz, lp, _ = p.shape
    return pl.pallas_call(
        _hgrn_kernel,
        grid=(bsz, lp // TILE),
        in_specs=[
            pl.BlockSpec((None, TILE, 2048), lambda b_, t: (b_, t, OFF_HG // 2048)),
            _full((8, 512)), _full((1, HG_DK)), _full((9, CH, CH)), _full((CH, CH)),
        ],
        out_specs=pl.BlockSpec((None, TILE, BRANCH_W), lambda b_, t: (b_, t, 0)),
        out_shape=jax.ShapeDtypeStruct((bsz, lp, BRANCH_W), BF),
        scratch_shapes=[pltpu.VMEM((HG_HEADS, HG_DK, HG_DK), F32)],
        compiler_params=_cparams(),
        name="hgrn2",
    )(p, vec, norm_w, msk, tri)


def _merge_kernel(h_ref, ya_ref, yb_ref, yc_ref, yd_ref, wg_ref, wb_ref, wo_ref, g_ref, b_ref, o_ref, *, pad):
    t = pl.program_id(1)
    h = h_ref[...]
    hb = h.astype(BF)
    mixed = jnp.zeros((TILE, D_MODEL), F32)
    for br, y_ref in enumerate((ya_ref, yb_ref, yc_ref, yd_ref)):
        gate = _sigmoid(jnp.dot(hb, wg_ref[:, br * D_MODEL:(br + 1) * D_MODEL], preferred_element_type=F32))
        mixed = mixed + gate * jnp.dot(y_ref[...], wb_ref[br], preferred_element_type=F32)
    out = jnp.dot(mixed.astype(BF), wo_ref[...], preferred_element_type=F32)
    r = ALPHA * h + out
    mu = jnp.mean(r, axis=-1, keepdims=True)
    rc = r - mu
    var = jnp.mean(rc * rc, axis=-1, keepdims=True)
    y = rc * lax.rsqrt(var + LN_EPS) * g_ref[...] + b_ref[...]
    row = t * TILE + lax.broadcasted_iota(jnp.int32, (TILE, 1), 0)
    o_ref[...] = jnp.where(row >= pad, y, 0.0)


def _merge_call(h, ya, yb, yc, yd, wg, wb, wo, g, b, pad):
    bsz, lp, d = h.shape
    tok = lambda w: pl.BlockSpec((None, TILE, w), lambda b_, t: (b_, t, 0))
    return pl.pallas_call(
        functools.partial(_merge_kernel, pad=pad),
        grid=(bsz, lp // TILE),
        in_specs=[tok(d), tok(BRANCH_W), tok(BRANCH_W), tok(BRANCH_W), tok(BRANCH_W),
                  _full((d, N_BRANCH * d)), _full((N_BRANCH, BRANCH_W, d)), _full((d, d)),
                  _full((1, d)), _full((1, d))],
        out_specs=tok(d),
        out_shape=jax.ShapeDtypeStruct((bsz, lp, d), F32),
        compiler_params=_cparams(),
        name="merge",
    )(h, ya, yb, yc, yd, wg, wb, wo, g.reshape(1, d), b.reshape(1, d))


def _split_w_in(w):
    sizes = (1536, 512, 4, 4, 1024, 512, 8, 512, 512, 512, 512, 512, 512, 4096)
    offs = [0]
    for s in sizes:
        offs.append(offs[-1] + s)
    return [w[:, offs[i]:offs[i + 1]] for i in range(len(sizes))]


def _prep_w1(w):
    (qkv, gz, gb, ga, xbc, mz, mdt, hq, hf, hi, hz, su, sz, gates) = _split_w_in(w)
    d = w.shape[0]
    ba = jnp.concatenate([gb, ga, jnp.zeros((d, 120), w.dtype)], axis=1)
    mdt_e = jnp.repeat(mdt, M2_HEADDIM, axis=1)
    w1 = jnp.concatenate([qkv, gz, xbc, mz, mdt_e, hq, hf, hi, hz, ba, su, sz], axis=1)
    return w1.astype(BF), gates.astype(BF)


def _lane_row(vals, off, width=128):
    return jnp.zeros((width,), F32).at[off:off + vals.shape[0]].set(vals.astype(F32))


def _prep_s5(a_re, a_im, b_re, b_im, c_re, c_im, log_dt):
    f = lambda v: v.astype(F32)
    a_re, a_im, b_re, b_im, c_re, c_im = map(f, (a_re, a_im, b_re, b_im, c_re, c_im))
    dt = jnp.exp(f(log_dt))[:, None]
    mag = jnp.exp(a_re * dt)
    lam_re, lam_im = mag * jnp.cos(a_im * dt), mag * jnp.sin(a_im * dt)
    den = jnp.square(a_re) + jnp.square(a_im)
    nr, ni = lam_re - 1.0, lam_im
    z_re, z_im = (nr * a_re + ni * a_im) / den, (ni * a_re - nr * a_im) / den
    bb_re = z_re[..., None] * b_re - z_im[..., None] * b_im
    bb_im = z_re[..., None] * b_im + z_im[..., None] * b_re
    eye8 = jnp.eye(8, dtype=F32)

    def in_blocks(bb):
        blk = jnp.transpose(bb, (0, 2, 1)).reshape(4, 8, S5_GROUP, S5_P)
        return jnp.einsum('jicp,ik->jickp', blk, eye8).reshape(4, 128, 512)

    def out_blocks(cc):
        blk = jnp.transpose(cc, (0, 2, 1)).reshape(4, 8, S5_P, S5_GROUP)
        return jnp.einsum('jipc,ik->jipkc', blk, eye8).reshape(4, 512, 128)

    bw = jnp.concatenate([in_blocks(bb_re), in_blocks(bb_im)], axis=2).astype(BF)
    cre = out_blocks(c_re).astype(BF)
    cim = out_blocks(c_im).astype(BF)
    pw = jnp.arange(1, 9, dtype=F32)[:, None, None]
    pmag = jnp.exp(a_re[None] * dt[None] * pw)
    pang = a_im[None] * dt[None] * pw
    p_re = (pmag * jnp.cos(pang)).reshape(8, S5_STATE)
    p_im = (pmag * jnp.sin(pang)).reshape(8, S5_STATE)
    row = jnp.arange(8)[:, None]
    tabs = [p_re, p_im]
    for dd in (1, 2, 4):
        tabs.append(jnp.where(row >= dd, p_re[dd - 1][None, :], 0.0))
        tabs.append(jnp.where(row >= dd, p_im[dd - 1][None, :], 0.0))
    return bw, cre, cim, jnp.concatenate(tabs, axis=0)


@jax.jit
def kernel(x, meta_tokens, ln_in_g, ln_in_b, w_in, gdn_conv_w, gdn_A_log, gdn_dt_bias, gdn_norm_w, m2_conv_w, m2_conv_b, m2_dt_bias, m2_A_log, m2_D, m2_norm_w, hg_lb_logits, hg_norm_w, s5_A_re, s5_A_im, s5_B_re, s5_B_im, s5_C_re, s5_C_im, s5_D, s5_log_dt, s5_glu_w1, s5_glu_w2, w_branch, w_out, ln_g, ln_b):
    bsz, seq, d = x.shape
    ltot = N_META + seq
    lp = -(-ltot // TILE) * TILE
    pad = lp - ltot
    meta = jnp.broadcast_to(meta_tokens.astype(x.dtype)[None], (bsz, N_META, d))
    xin = jnp.concatenate([jnp.zeros((bsz, pad, d), x.dtype), meta, x], axis=1)
    h = _ln_in_call(xin, ln_in_g.astype(F32), ln_in_b.astype(F32), pad)

    msk = _masks64()
    tri = msk[0].astype(BF)
    ones = jnp.ones((CH, CH), BF)
    jj = jnp.arange(256)
    eyet = (jnp.arange(CH)[:, None] == (jj % CH)[None, :]).astype(F32)
    maskt = (jnp.arange(CH)[:, None] >= (jj % CH)[None, :]).astype(F32)
    bd = ((jj // CH)[:, None] == (jj // CH)[None, :]).astype(F32)

    cum = jnp.cumsum(jax.nn.softmax(hg_lb_logits.astype(F32), axis=0), axis=0)
    lower_bounds = cum - cum[0:1]

    for l in range(w_in.shape[0]):
        w1, wg = _prep_w1(w_in[l])
        bw, cre, cim, pw_tab = _prep_s5(s5_A_re[l], s5_A_im[l], s5_B_re[l], s5_B_im[l], s5_C_re[l], s5_C_im[l],
                                        s5_log_dt[l])
        p, yd = _proj_s5_call(h, w1, bw, cre, cim, pw_tab, s5_D[l].astype(F32).reshape(1, -1),
                              s5_glu_w1[l].astype(BF), s5_glu_w2[l].astype(BF))

        lane = jnp.stack([_lane_row(jnp.exp(gdn_A_log[l].astype(F32)), 4), _lane_row(gdn_dt_bias[l], 4)]
                         + [jnp.zeros((128,), F32)] * 6)
        ya = _gdn_call(p, gdn_conv_w[l].astype(F32), lane, gdn_norm_w[l].astype(F32).reshape(1, GDN_DK), msk, tri)

        rep = lambda v: jnp.repeat(v.astype(F32), M2_HEADDIM)
        vec = jnp.stack([rep(m2_dt_bias[l]), rep(-jnp.exp(m2_A_log[l].astype(F32))), rep(m2_D[l]),
                         m2_norm_w[l].astype(F32)] + [jnp.zeros((BRANCH_W,), F32)] * 4)
        yb = _ssd_call(p, m2_conv_w[l].astype(F32), m2_conv_b[l].astype(F32).reshape(1, -1), vec,
                       eyet, maskt, bd, tri, ones, pad)

        lb = lower_bounds[l]
        hvec = jnp.stack([jnp.log(lb), jnp.log1p(-lb), 1.0 - lb] + [jnp.zeros((BRANCH_W,), F32)] * 5)
        yc = _hgrn_call(p, hvec, hg_norm_w[l].astype(F32).reshape(1, HG_DK), msk, tri)

        h = _merge_call(h, ya, yb, yc, yd, wg, w_branch[l].astype(BF), w_out[l].astype(BF),
                        ln_g[l].astype(F32), ln_b[l].astype(F32), pad)
    return h[:, pad + N_META:]
```

```python
import functools
import math

import jax
import jax.numpy as jnp
from jax import lax
from jax.experimental import pallas as pl
from jax.experimental.pallas import tpu as pltpu

F32 = jnp.float32
BF = jnp.bfloat16

D_MODEL = 1024
DEPTH = 4
N_META = 16
CONV_K = 4
N_BRANCH = 4
BRANCH_W = 512
GDN_HEADS = 4
GDN_DK = 128
M2_HEADS = 8
M2_HEADDIM = 64
M2_GROUPS = 2
M2_DSTATE = 128
HG_HEADS = 4
HG_DK = 128
S5_GROUP = 16
S5_NG = 32
S5_P = 64
S5_STATE = S5_NG * S5_P
ALPHA = (2 * DEPTH) ** 0.25
LN_EPS = 1e-5
RMS_EPS = 1e-6

TILE = 256
CH = 64
NCH = TILE // CH
SUB = 16
HG_EXP_CLAMP = 80.0
VMEM_LIMIT = 56 * 1024 * 1024

OFF_QKV, OFF_GZ, OFF_XBC, OFF_MZ, OFF_MDT = 0, 1536, 2048, 3072, 3584
OFF_HG, OFF_BA, OFF_S5 = 4096, 6144, 6272
NP = 6272
N1 = 7296


def _sigmoid(x):
    return 0.5 * (jnp.tanh(0.5 * x) + 1.0)


def _silu(x):
    return x * _sigmoid(x)


def _softplus(x):
    return jnp.maximum(x, 0.0) + jnp.log1p(jnp.exp(-jnp.abs(x)))


def _mm(a, b):
    return jnp.dot(a.astype(BF), b.astype(BF), preferred_element_type=F32)


def _mm_nt(a, b):
    return lax.dot_general(a.astype(BF), b.astype(BF), (((1,), (1,)), ((), ())),
                           preferred_element_type=F32)


def _mm_tn(a, b):
    return lax.dot_general(a.astype(BF), b.astype(BF), (((0,), (0,)), ((), ())),
                           preferred_element_type=F32)


def _mm3_left(m_bf, x):
    hi = x.astype(BF)
    r1 = x - hi.astype(F32)
    mid = r1.astype(BF)
    lo = (r1 - mid.astype(F32)).astype(BF)
    d = lambda v: jnp.dot(m_bf, v, preferred_element_type=F32)
    return (d(lo) + d(mid)) + d(hi)


def _rms_norm(x):
    return x * lax.rsqrt(jnp.mean(x * x, axis=-1, keepdims=True) + RMS_EPS)


def _masks64():
    i = jnp.arange(CH)[:, None]
    j = jnp.arange(CH)[None, :]
    ms = [(i >= j), (i > j)]
    for s in (1, 2, 4, 8, 16, 32):
        ms.append((i // (2 * s) == j // (2 * s)) & (i % (2 * s) >= s) & (j % (2 * s) < s))
    ms.append(i == j)
    return jnp.stack(ms).astype(F32)


def _cparams():
    return pltpu.CompilerParams(dimension_semantics=("arbitrary", "arbitrary"),
                                vmem_limit_bytes=VMEM_LIMIT)


def _full(shape):
    n = len(shape)
    return pl.BlockSpec(shape, lambda b, t: (0,) * n, pipeline_mode=pl.Buffered(1))


def _ln_in_kernel(x_ref, g_ref, b_ref, o_ref, *, pad):
    t = pl.program_id(1)
    x = x_ref[...]
    mu = jnp.mean(x, axis=-1, keepdims=True)
    xc = x - mu
    var = jnp.mean(xc * xc, axis=-1, keepdims=True)
    y = xc * lax.rsqrt(var + LN_EPS) * g_ref[...] + b_ref[...]
    row = t * TILE + lax.broadcasted_iota(jnp.int32, (TILE, 1), 0)
    o_ref[...] = jnp.where(row >= pad, y, 0.0)


def _ln_in_call(xin, g, b, pad):
    bsz, lp, d = xin.shape
    return pl.pallas_call(
        functools.partial(_ln_in_kernel, pad=pad),
        grid=(bsz, lp // TILE),
        in_specs=[pl.BlockSpec((None, TILE, d), lambda b_, t: (b_, t, 0)), _full((1, d)), _full((1, d))],
        out_specs=pl.BlockSpec((None, TILE, d), lambda b_, t: (b_, t, 0)),
        out_shape=jax.ShapeDtypeStruct((bsz, lp, d), F32),
        compiler_params=_cparams(),
        name="ln_in",
    )(xin, g.reshape(1, d), b.reshape(1, d))


S5_CG = 512
S5_NBLK = TILE // 8
PROJ_CHUNK = 512
GATE_CHUNK = 256


def _s5_scan_block(b, cg, carry, xr_ref, xi_ref, pw_ref):
    rows = slice(8 * b, 8 * b + 8)
    cs = slice(cg * S5_CG, (cg + 1) * S5_CG)
    cr, ci = carry
    r = xr_ref[rows, cs]
    im = xi_ref[rows, cs]
    for n, d in enumerate((1, 2, 4)):
        lr = pw_ref[16 + 16 * n:24 + 16 * n, cs]
        li = pw_ref[24 + 16 * n:32 + 16 * n, cs]
        rsh = pltpu.roll(r, d, 0)
        ish = pltpu.roll(im, d, 0)
        r, im = r + (lr * rsh - li * ish), im + (lr * ish + li * rsh)
    l8r = pw_ref[0:8, cs]
    l8i = pw_ref[8:16, cs]
    r, im = r + (l8r * cr - l8i * ci), im + (l8r * ci + l8i * cr)
    xr_ref[rows, cs] = r
    xi_ref[rows, cs] = im
    return r[7:8, :], im[7:8, :]


def _proj_s5_kernel(h_ref, w_ref, bw_ref, cre_ref, cim_ref, pw_ref, d_ref, g1_ref, g2_ref,
                    p_ref, yd_ref, xr_ref, xi_ref, c_ref):
    t = pl.program_id(1)

    @pl.when(t == 0)
    def _():
        c_ref[...] = jnp.zeros_like(c_ref)

    hb = h_ref[...].astype(BF)
    u = jnp.dot(hb, w_ref[:, OFF_S5:OFF_S5 + BRANCH_W], preferred_element_type=F32)
    ub = u.astype(BF)
    for j in range(4):
        bu = jnp.dot(ub[:, j * 128:(j + 1) * 128], bw_ref[j], preferred_element_type=F32)
        xr_ref[:, j * 512:(j + 1) * 512] = bu[:, :512]
        xi_ref[:, j * 512:(j + 1) * 512] = bu[:, 512:]

    ncg = S5_STATE // S5_CG
    carries = [(c_ref[0:1, cg * S5_CG:(cg + 1) * S5_CG], c_ref[1:2, cg * S5_CG:(cg + 1) * S5_CG])
               for cg in range(ncg)]
    work = [(b, cg) for b in range(S5_NBLK) for cg in range(ncg)]
    starts = list(range(0, NP, PROJ_CHUNK))
    per = -(-len(work) // len(starts))
    for n, j in enumerate(starts):
        w = min(PROJ_CHUNK, NP - j)
        p_ref[:, j:j + w] = jnp.dot(hb, w_ref[:, j:j + w], preferred_element_type=F32)
        for b, cg in work[n * per:(n + 1) * per]:
            carries[cg] = _s5_scan_block(b, cg, carries[cg], xr_ref, xi_ref, pw_ref)
    for cg in range(ncg):
        c_ref[0:1, cg * S5_CG:(cg + 1) * S5_CG] = carries[cg][0]
        c_ref[1:2, cg * S5_CG:(cg + 1) * S5_CG] = carries[cg][1]

    ys = []
    for j in range(4):
        cs = slice(j * 512, (j + 1) * 512)
        ys.append(_mm(xr_ref[:, cs], cre_ref[j]) - _mm(xi_ref[:, cs], cim_ref[j]))
    y = jnp.concatenate(ys, axis=1) + d_ref[...] * u
    y = 0.5 * y * (1.0 + jnp.tanh(math.sqrt(2.0 / math.pi) * (y + 0.044715 * (y * y * y))))
    yb = y.astype(BF)
    glu = jnp.dot(yb, g1_ref[...], preferred_element_type=F32) * _sigmoid(
        jnp.dot(yb, g2_ref[...], preferred_element_type=F32))
    z = jnp.dot(hb, w_ref[:, OFF_S5 + BRANCH_W:OFF_S5 + 2 * BRANCH_W], preferred_element_type=F32)
    yd_ref[...] = (glu * _silu(z)).astype(yd_ref.dtype)


def _proj_s5_call(h, w1, bw, cre, cim, pw, dvec, g1, g2):
    bsz, lp, d = h.shape
    return pl.pallas_call(
        _proj_s5_kernel,
        grid=(bsz, lp // TILE),
        in_specs=[pl.BlockSpec((None, TILE, d), lambda b_, t: (b_, t, 0)), _full((d, N1)),
                  _full((4, 128, 1024)), _full((4, 512, 128)), _full((4, 512, 128)), _full((64, S5_STATE)),
                  _full((1, 512)), _full((512, 512)), _full((512, 512))],
        out_specs=[pl.BlockSpec((None, TILE, NP), lambda b_, t: (b_, t, 0)),
                   pl.BlockSpec((None, TILE, BRANCH_W), lambda b_, t: (b_, t, 0))],
        out_shape=[jax.ShapeDtypeStruct((bsz, lp, NP), F32), jax.ShapeDtypeStruct((bsz, lp, BRANCH_W), BF)],
        scratch_shapes=[pltpu.VMEM((TILE, S5_STATE), F32), pltpu.VMEM((TILE, S5_STATE), F32),
                        pltpu.VMEM((2, S5_STATE), F32)],
        compiler_params=_cparams(),
        name="proj_s5",
    )(h, w1, bw, cre, cim, pw, dvec, g1, g2)


def _conv_chunk(xp_ref, cw, r0):
    x = xp_ref[r0 + 8:r0 + 8 + CH, :] * cw[CONV_K - 1:CONV_K, :]
    for j in range(CONV_K - 1):
        s = r0 + 8 - (CONV_K - 1) + j
        x = x + xp_ref[s:s + CH, :] * cw[j:j + 1, :]
    return x


def _tri_inverse_many(a_list, msk_ref):
    xs = [msk_ref[8] - a * msk_ref[2] for a in a_list]
    for lvl in range(1, 6):
        m = msk_ref[2 + lvl]
        t1 = [_mm(x, a * m) for x, a in zip(xs, a_list)]
        xs = [x - _mm(t, x) for x, t in zip(xs, t1)]
    return xs


def _gdn_kernel(qkv_ref, z_ref, ba_ref, cw_ref, lane_ref, nw_ref, msk_ref, tri_ref, y_ref, xp_ref, s_ref,
                fill=lambda: None):
    t = pl.program_id(1)

    @pl.when(t == 0)
    def _():
        xp_ref[0:8, :] = jnp.zeros((8, 3 * BRANCH_W), F32)
        s_ref[...] = jnp.zeros_like(s_ref)

    xp_ref[8:8 + TILE, :] = qkv_ref[...]
    cw = cw_ref[...]
    exp_a = lane_ref[0:1, :]
    dt_bias = lane_ref[1:2, :]
    nw = nw_ref[...]
    causal = msk_ref[0]
    strict = msk_ref[1]
    tri = tri_ref[...]
    items = []
    a_list = []
    for c in range(NCH):
        r0 = c * CH
        qkv = _silu(_conv_chunk(xp_ref, cw, r0))
        ba = ba_ref[r0:r0 + CH, :]
        beta_all = _sigmoid(ba)
        gl = -(exp_a * _softplus(ba + dt_bias))
        gc = _mm3_left(tri, gl)
        gc_t = gc.T
        for h in range(GDN_HEADS):
            sl = slice(h * GDN_DK, (h + 1) * GDN_DK)
            q = qkv[:, sl]
            k = qkv[:, BRANCH_W + h * GDN_DK:BRANCH_W + (h + 1) * GDN_DK]
            v = qkv[:, 2 * BRANCH_W + h * GDN_DK:2 * BRANCH_W + (h + 1) * GDN_DK]
            q = q * (lax.rsqrt(jnp.sum(q * q, axis=-1, keepdims=True) + RMS_EPS) * GDN_DK ** -0.5)
            k = k * lax.rsqrt(jnp.sum(k * k, axis=-1, keepdims=True) + RMS_EPS)
            beta = beta_all[:, h:h + 1]
            gcol = gc[:, 4 + h:5 + h]
            grow = gc_t[4 + h:5 + h, :]
            decay = jnp.exp(jnp.where(causal > 0, gcol - grow, -jnp.inf))
            kb = k * beta
            eg = jnp.exp(gcol)
            glast = gc[CH - 1:CH, 4 + h:5 + h]
            a_list.append(strict * (_mm_nt(kb, k) * decay))
            items.append(dict(
                rhs=jnp.concatenate([v * beta, kb * eg], axis=1),
                qk=causal * (_mm_nt(q, k) * decay),
                qd=q * eg,
                kd=k * jnp.exp(glast - gcol),
                gtot=jnp.exp(glast)))
            fill()
    tinvs = _tri_inverse_many(a_list, msk_ref)
    uws = [_mm(tinv, it["rhs"]) for tinv, it in zip(tinvs, items)]
    kd_uw = [_mm_tn(it["kd"], uw) for it, uw in zip(items, uws)]
    qk_uw = [_mm(it["qk"], uw) for it, uw in zip(items, uws)]
    states = [s_ref[h] for h in range(GDN_HEADS)]
    for c in range(NCH):
        r0 = c * CH
        for h in range(GDN_HEADS):
            sl = slice(h * GDN_DK, (h + 1) * GDN_DK)
            n = c * GDN_HEADS + h
            s = states[h]
            o = _mm(items[n]["qd"] - qk_uw[n][:, GDN_DK:], s) + qk_uw[n][:, :GDN_DK]
            states[h] = (s * items[n]["gtot"] - _mm(kd_uw[n][:, GDN_DK:], s)) + kd_uw[n][:, :GDN_DK]
            y = _rms_norm(o) * nw * _silu(z_ref[r0:r0 + CH, sl])
            y_ref[r0:r0 + CH, sl] = y.astype(y_ref.dtype)
    for h in range(GDN_HEADS):
        s_ref[h] = states[h]
    xp_ref[0:8, :] = xp_ref[TILE:TILE + 8, :]


def _gdn_call(p, conv_w, lane, norm_w, msk, tri):
    bsz, lp, _ = p.shape
    return pl.pallas_call(
        _gdn_kernel,
        grid=(bsz, lp // TILE),
        in_specs=[
            pl.BlockSpec((None, TILE, 1536), lambda b_, t: (b_, t, OFF_QKV // 1536)),
            pl.BlockSpec((None, TILE, 512), lambda b_, t: (b_, t, OFF_GZ // 512)),
            pl.BlockSpec((None, TILE, 128), lambda b_, t: (b_, t, OFF_BA // 128)),
            _full((CONV_K, 1536)), _full((8, 128)), _full((1, GDN_DK)), _full((9, CH, CH)), _full((CH, CH)),
        ],
        out_specs=pl.BlockSpec((None, TILE, BRANCH_W), lambda b_, t: (b_, t, 0)),
        out_shape=jax.ShapeDtypeStruct((bsz, lp, BRANCH_W), BF),
        scratch_shapes=[pltpu.VMEM((TILE + 8, 1536), F32), pltpu.VMEM((GDN_HEADS, GDN_DK, GDN_DK), F32)],
        compiler_params=_cparams(),
        name="gdn",
    )(p, p, p, conv_w, lane, norm_w, msk, tri)


def _ssd_kernel(xbc_ref, zdt_ref, cw_ref, cb_ref, vec_ref, eyet_ref, maskt_ref, bd_ref, tri_ref, ones_ref,
                y_ref, xp_ref, s_ref, *, pad, fill=lambda: None):
    t = pl.program_id(1)

    @pl.when(t == 0)
    def _():
        xp_ref[0:8, :] = jnp.zeros((8, 1024), F32)
        s_ref[...] = jnp.zeros_like(s_ref)

    xp_ref[8:8 + TILE, :] = xbc_ref[...]
    cw = cw_ref[...]
    cb = cb_ref[...]
    dt_bias = vec_ref[0:1, :]
    neg_a = vec_ref[1:2, :]
    d_skip = vec_ref[2:3, :]
    nw = vec_ref[3:4, :]
    tri = tri_ref[...]
    ones = ones_ref[...]
    gw = M2_HEADDIM * (M2_HEADS // M2_GROUPS)
    for c in range(NCH):
        r0 = c * CH
        xbc = _silu(_conv_chunk(xp_ref, cw, r0) + cb)
        xs = xbc[:, :BRANCH_W]
        row = t * TILE + r0 + lax.broadcasted_iota(jnp.int32, (CH, 1), 0)
        valid = (row >= pad).astype(F32)
        dt = _softplus(zdt_ref[r0:r0 + CH, BRANCH_W:] + dt_bias) * valid
        a = dt * neg_a
        acum = _mm3_left(tri, a)
        xdt = xs * dt
        z = zdt_ref[r0:r0 + CH, :BRANCH_W]
        for g in range(M2_GROUPS):
            gs = slice(g * gw, (g + 1) * gw)
            bg = xbc[:, BRANCH_W + g * M2_DSTATE:BRANCH_W + (g + 1) * M2_DSTATE]
            cg = xbc[:, BRANCH_W + M2_GROUPS * M2_DSTATE + g * M2_DSTATE:
                     BRANCH_W + M2_GROUPS * M2_DSTATE + (g + 1) * M2_DSTATE]
            ag = acum[:, gs]
            cbt = _mm_nt(cg, jnp.concatenate([bg] * 4, axis=0))
            arow = _mm3_left(ones, ag * eyet_ref[...])
            lm = jnp.exp(jnp.where(maskt_ref[...] > 0, ag - arow, -jnp.inf))
            xg = xdt[:, gs]
            xbd = jnp.concatenate([xg] * 4, axis=0) * bd_ref[...]
            y_diag = _mm(cbt * lm, xbd)
            alast = ag[CH - 1:CH, :]
            s = s_ref[g]
            y_off = _mm(cg, s) * jnp.exp(ag)
            s_ref[g] = s * jnp.exp(alast) + _mm_tn(bg, xg * jnp.exp(alast - ag))
            y = (y_diag + y_off + d_skip[:, gs] * xs[:, gs]) * _silu(z[:, gs])
            y = _rms_norm(y) * nw[:, gs]
            y_ref[r0:r0 + CH, gs] = y.astype(y_ref.dtype)
        fill()
    xp_ref[0:8, :] = xp_ref[TILE:TILE + 8, :]


def _ssd_call(p, conv_w, conv_b, vec, eyet, maskt, bd, tri, ones, pad):
    bsz, lp, _ = p.shape
    return pl.pallas_call(
        functools.partial(_ssd_kernel, pad=pad),
        grid=(bsz, lp // TILE),
        in_specs=[
            pl.BlockSpec((None, TILE, 1024), lambda b_, t: (b_, t, OFF_XBC // 1024)),
            pl.BlockSpec((None, TILE, 1024), lambda b_, t: (b_, t, OFF_MZ // 1024)),
            _full((CONV_K, 1024)), _full((1, 1024)), _full((8, 512)),
            _full((CH, 256)), _full((CH, 256)), _full((256, 256)), _full((CH, CH)), _full((CH, CH)),
        ],
        out_specs=pl.BlockSpec((None, TILE, BRANCH_W), lambda b_, t: (b_, t, 0)),
        out_shape=jax.ShapeDtypeStruct((bsz, lp, BRANCH_W), BF),
        scratch_shapes=[pltpu.VMEM((TILE + 8, 1024), F32), pltpu.VMEM((M2_GROUPS, M2_DSTATE, 256), F32)],
        compiler_params=_cparams(),
        name="ssd",
    )(p, p, conv_w, conv_b, vec, eyet, maskt, bd, tri, ones)


def _hgrn_kernel(p_ref, vec_ref, nw_ref, msk_ref, tri_ref, y_ref, s_ref, fill=lambda: None):
    t = pl.program_id(1)

    @pl.when(t == 0)
    def _():
        s_ref[...] = jnp.zeros_like(s_ref)

    log_lb = vec_ref[0:1, :]
    log1m_lb = vec_ref[1:2, :]
    one_m_lb = vec_ref[2:3, :]
    nw = nw_ref[...]
    causal = msk_ref[0]
    tri = tri_ref[...]
    hsl = [slice(h * HG_DK, (h + 1) * HG_DK) for h in range(HG_HEADS)]
    nsub = CH // SUB
    prep = []
    for c in range(NCH):
        r0 = c * CH
        q = _silu(p_ref[r0:r0 + CH, 0:512])
        zf = p_ref[r0:r0 + CH, 512:1024]
        e = jnp.exp(-jnp.abs(zf))
        lsig = jnp.minimum(zf, 0.0) - jnp.log1p(e)
        bb = log1m_lb + lsig
        logf = jnp.maximum(log_lb, bb) + jnp.log1p(jnp.exp(-jnp.abs(log_lb - bb)))
        sig_neg = jnp.where(zf >= 0, e, 1.0) / (1.0 + e)
        k = one_m_lb * sig_neg
        gcum = _mm3_left(tri, logf)
        glast = gcum[CH - 1:CH, :]
        subs = []
        for i in range(nsub):
            rs = slice(i * SUB, (i + 1) * SUB)
            n = (i + 1) * SUB
            gref = gcum[i * SUB - 1:i * SUB, :] if i > 0 else jnp.zeros((1, BRANCH_W), F32)
            qt = q[rs] * jnp.exp(gcum[rs] - gref)
            kt = k[:n] * jnp.exp(jnp.minimum(gref - gcum[:n], HG_EXP_CLAMP))
            subs.append((qt, kt, qt * jnp.exp(gref)))
        prep.append(dict(kd=k * jnp.exp(glast - gcum), eg=jnp.exp(glast), subs=subs))
        fill()
    incs = [[_mm_tn(p_ref[c * CH:(c + 1) * CH, 1024 + h * HG_DK:1024 + (h + 1) * HG_DK], prep[c]["kd"][:, hsl[h]])
             for h in range(HG_HEADS)] for c in range(NCH)]
    states = [[s_ref[h] for h in range(HG_HEADS)]]
    for c in range(NCH):
        states.append([states[c][h] * prep[c]["eg"][:, hsl[h]] + incs[c][h] for h in range(HG_HEADS)])
    for h in range(HG_HEADS):
        s_ref[h] = states[NCH][h]
    ams = [[[_mm_nt(prep[c]["subs"][i][0][:, hsl[h]], prep[c]["subs"][i][1][:, hsl[h]])
             * causal[i * SUB:(i + 1) * SUB, :(i + 1) * SUB]
             for h in range(HG_HEADS)] for i in range(nsub)] for c in range(NCH)]
    for c in range(NCH):
        r0 = c * CH
        for h in range(HG_HEADS):
            iv = p_ref[r0:r0 + CH, 1024 + h * HG_DK:1024 + (h + 1) * HG_DK]
            o = jnp.concatenate(
                [_mm(ams[c][i][h], iv[:(i + 1) * SUB]) + _mm_nt(prep[c]["subs"][i][2][:, hsl[h]], states[c][h])
                 for i in range(nsub)], axis=0)
            z = p_ref[r0:r0 + CH, 1536 + h * HG_DK:1536 + (h + 1) * HG_DK]
            y_ref[r0:r0 + CH, hsl[h]] = (_rms_norm(o) * nw * _silu(z)).astype(y_ref.dtype)


def _hgrn_call(p, vec, norm_w, msk, tri):
    bsz, lp, _ = p.shape
    return pl.pallas_call(
        _hgrn_kernel,
        grid=(bsz, lp // TILE),
        in_specs=[
            pl.BlockSpec((None, TILE, 2048), lambda b_, t: (b_, t, OFF_HG // 2048)),
            _full((8, 512)), _full((1, HG_DK)), _full((9, CH, CH)), _full((CH, CH)),
        ],
        out_specs=pl.BlockSpec((None, TILE, BRANCH_W), lambda b_, t: (b_, t, 0)),
        out_shape=jax.ShapeDtypeStruct((bsz, lp, BRANCH_W), BF),
        scratch_shapes=[pltpu.VMEM((HG_HEADS, HG_DK, HG_DK), F32)],
        compiler_params=_cparams(),
        name="hgrn2",
    )(p, vec, norm_w, msk, tri)


def _merge_kernel(h_ref, ya_ref, yb_ref, yc_ref, yd_ref, wg_ref, wb_ref, wo_ref, g_ref, b_ref, o_ref, *, pad):
    t = pl.program_id(1)
    h = h_ref[...]
    hb = h.astype(BF)
    mixed = jnp.zeros((TILE, D_MODEL), F32)
    for br, y_ref in enumerate((ya_ref, yb_ref, yc_ref, yd_ref)):
        gate = _sigmoid(jnp.dot(hb, wg_ref[:, br * D_MODEL:(br + 1) * D_MODEL], preferred_element_type=F32))
        mixed = mixed + gate * jnp.dot(y_ref[...], wb_ref[br], preferred_element_type=F32)
    out = jnp.dot(mixed.astype(BF), wo_ref[...], preferred_element_type=F32)
    r = ALPHA * h + out
    mu = jnp.mean(r, axis=-1, keepdims=True)
    rc = r - mu
    var = jnp.mean(rc * rc, axis=-1, keepdims=True)
    y = rc * lax.rsqrt(var + LN_EPS) * g_ref[...] + b_ref[...]
    row = t * TILE + lax.broadcasted_iota(jnp.int32, (TILE, 1), 0)
    o_ref[...] = jnp.where(row >= pad, y, 0.0)


def _mix_merge_kernel(h_ref, qkv_ref, gz_ref, ba_ref, xbc_ref, zdt_ref, hg_ref, yd_ref,
                      gcw_ref, glane_ref, gnw_ref, scw_ref, scb_ref, svec_ref, eyet_ref, maskt_ref, bd_ref, ones_ref,
                      hvec_ref, hnw_ref, msk_ref, tri_ref, wg_ref, wb_ref, wo_ref, g_ref, b_ref,
                      o_ref, gxp_ref, gs_ref, sxp_ref, ss_ref, hs_ref, ya_ref, yb_ref, yc_ref, gate_ref, *, pad):
    t = pl.program_id(1)
    h = h_ref[...]
    hb = h.astype(BF)

    def gate_piece(j):
        cs = slice(j * GATE_CHUNK, (j + 1) * GATE_CHUNK)
        gate_ref[:, cs] = _sigmoid(jnp.dot(hb, wg_ref[:, cs], preferred_element_type=F32))

    def branch_piece(br, y_ref):
        cs = slice(br * D_MODEL, (br + 1) * D_MODEL)
        gate_ref[:, cs] = gate_ref[:, cs] * jnp.dot(y_ref[...], wb_ref[br], preferred_element_type=F32)

    pending = [functools.partial(gate_piece, j) for j in range(N_BRANCH * D_MODEL // GATE_CHUNK)]
    pending.append(functools.partial(branch_piece, 3, yd_ref))

    def fill(n):
        for _ in range(min(n, len(pending))):
            pending.pop(0)()

    _gdn_kernel(qkv_ref, gz_ref, ba_ref, gcw_ref, glane_ref, gnw_ref, msk_ref, tri_ref, ya_ref, gxp_ref, gs_ref,
                fill=functools.partial(fill, 1))
    fill(len(pending))
    pending.append(functools.partial(branch_piece, 0, ya_ref))
    _ssd_kernel(xbc_ref, zdt_ref, scw_ref, scb_ref, svec_ref, eyet_ref, maskt_ref, bd_ref, tri_ref, ones_ref,
                yb_ref, sxp_ref, ss_ref, pad=pad, fill=functools.partial(fill, 1))
    fill(len(pending))
    pending.append(functools.partial(branch_piece, 1, yb_ref))
    _hgrn_kernel(hg_ref, hvec_ref, hnw_ref, msk_ref, tri_ref, yc_ref, hs_ref, fill=functools.partial(fill, 1))
    fill(len(pending))
    branch_piece(2, yc_ref)
    mixed = ((gate_ref[:, 0:D_MODEL] + gate_ref[:, D_MODEL:2 * D_MODEL])
             + (gate_ref[:, 2 * D_MODEL:3 * D_MODEL] + gate_ref[:, 3 * D_MODEL:4 * D_MODEL]))
    out = jnp.dot(mixed.astype(BF), wo_ref[...], preferred_element_type=F32)
    r = ALPHA * h + out
    mu = jnp.mean(r, axis=-1, keepdims=True)
    rc = r - mu
    var = jnp.mean(rc * rc, axis=-1, keepdims=True)
    y = rc * lax.rsqrt(var + LN_EPS) * g_ref[...] + b_ref[...]
    row = t * TILE + lax.broadcasted_iota(jnp.int32, (TILE, 1), 0)
    o_ref[...] = jnp.where(row >= pad, y, 0.0)


def _mix_merge_call(h, p, yd, gdn_prm, ssd_prm, hg_prm, msk, tri, wg, wb, wo, g, b, pad):
    bsz, lp, d = h.shape
    tok = lambda w, j=0: pl.BlockSpec((None, TILE, w), lambda b_, t: (b_, t, j))
    return pl.pallas_call(
        functools.partial(_mix_merge_kernel, pad=pad),
        grid=(bsz, lp // TILE),
        in_specs=[tok(d), tok(1536, OFF_QKV // 1536), tok(512, OFF_GZ // 512), tok(128, OFF_BA // 128),
                  tok(1024, OFF_XBC // 1024), tok(1024, OFF_MZ // 1024), tok(2048, OFF_HG // 2048), tok(BRANCH_W),
                  _full((CONV_K, 1536)), _full((8, 128)), _full((1, GDN_DK)),
                  _full((CONV_K, 1024)), _full((1, 1024)), _full((8, 512)),
                  _full((CH, 256)), _full((CH, 256)), _full((256, 256)), _full((CH, CH)),
                  _full((8, 512)), _full((1, HG_DK)), _full((9, CH, CH)), _full((CH, CH)),
                  _full((d, N_BRANCH * d)), _full((N_BRANCH, BRANCH_W, d)), _full((d, d)),
                  _full((1, d)), _full((1, d))],
        out_specs=tok(d),
        out_shape=jax.ShapeDtypeStruct((bsz, lp, d), F32),
        scratch_shapes=[pltpu.VMEM((TILE + 8, 1536), F32), pltpu.VMEM((GDN_HEADS, GDN_DK, GDN_DK), F32),
                        pltpu.VMEM((TILE + 8, 1024), F32), pltpu.VMEM((M2_GROUPS, M2_DSTATE, 256), F32),
                        pltpu.VMEM((HG_HEADS, HG_DK, HG_DK), F32),
                        pltpu.VMEM((TILE, BRANCH_W), BF), pltpu.VMEM((TILE, BRANCH_W), BF),
                        pltpu.VMEM((TILE, BRANCH_W), BF), pltpu.VMEM((TILE, N_BRANCH * D_MODEL), F32)],
        compiler_params=_cparams(),
        name="mix_merge",
    )(h, p, p, p, p, p, p, yd, *gdn_prm, *ssd_prm, *hg_prm, msk, tri, wg, wb, wo, g.reshape(1, d), b.reshape(1, d))


def _merge_call(h, ya, yb, yc, yd, wg, wb, wo, g, b, pad):
    bsz, lp, d = h.shape
    tok = lambda w: pl.BlockSpec((None, TILE, w), lambda b_, t: (b_, t, 0))
    return pl.pallas_call(
        functools.partial(_merge_kernel, pad=pad),
        grid=(bsz, lp // TILE),
        in_specs=[tok(d), tok(BRANCH_W), tok(BRANCH_W), tok(BRANCH_W), tok(BRANCH_W),
                  _full((d, N_BRANCH * d)), _full((N_BRANCH, BRANCH_W, d)), _full((d, d)),
                  _full((1, d)), _full((1, d))],
        out_specs=tok(d),
        out_shape=jax.ShapeDtypeStruct((bsz, lp, d), F32),
        compiler_params=_cparams(),
        name="merge",
    )(h, ya, yb, yc, yd, wg, wb, wo, g.reshape(1, d), b.reshape(1, d))


def _split_w_in(w):
    sizes = (1536, 512, 4, 4, 1024, 512, 8, 512, 512, 512, 512, 512, 512, 4096)
    offs = [0]
    for s in sizes:
        offs.append(offs[-1] + s)
    return [w[:, offs[i]:offs[i + 1]] for i in range(len(sizes))]


def _prep_w1(w):
    (qkv, gz, gb, ga, xbc, mz, mdt, hq, hf, hi, hz, su, sz, gates) = _split_w_in(w)
    d = w.shape[0]
    ba = jnp.concatenate([gb, ga, jnp.zeros((d, 120), w.dtype)], axis=1)
    mdt_e = jnp.repeat(mdt, M2_HEADDIM, axis=1)
    w1 = jnp.concatenate([qkv, gz, xbc, mz, mdt_e, hq, hf, hi, hz, ba, su, sz], axis=1)
    return w1.astype(BF), gates.astype(BF)


def _lane_row(vals, off, width=128):
    return jnp.zeros((width,), F32).at[off:off + vals.shape[0]].set(vals.astype(F32))


def _prep_s5(a_re, a_im, b_re, b_im, c_re, c_im, log_dt):
    f = lambda v: v.astype(F32)
    a_re, a_im, b_re, b_im, c_re, c_im = map(f, (a_re, a_im, b_re, b_im, c_re, c_im))
    dt = jnp.exp(f(log_dt))[:, None]
    mag = jnp.exp(a_re * dt)
    lam_re, lam_im = mag * jnp.cos(a_im * dt), mag * jnp.sin(a_im * dt)
    den = jnp.square(a_re) + jnp.square(a_im)
    nr, ni = lam_re - 1.0, lam_im
    z_re, z_im = (nr * a_re + ni * a_im) / den, (ni * a_re - nr * a_im) / den
    bb_re = z_re[..., None] * b_re - z_im[..., None] * b_im
    bb_im = z_re[..., None] * b_im + z_im[..., None] * b_re
    eye8 = jnp.eye(8, dtype=F32)

    def in_blocks(bb):
        blk = jnp.transpose(bb, (0, 2, 1)).reshape(4, 8, S5_GROUP, S5_P)
        return jnp.einsum('jicp,ik->jickp', blk, eye8).reshape(4, 128, 512)

    def out_blocks(cc):
        blk = jnp.transpose(cc, (0, 2, 1)).reshape(4, 8, S5_P, S5_GROUP)
        return jnp.einsum('jipc,ik->jipkc', blk, eye8).reshape(4, 512, 128)

    bw = jnp.concatenate([in_blocks(bb_re), in_blocks(bb_im)], axis=2).astype(BF)
    cre = out_blocks(c_re).astype(BF)
    cim = out_blocks(c_im).astype(BF)
    pw = jnp.arange(1, 9, dtype=F32)[:, None, None]
    pmag = jnp.exp(a_re[None] * dt[None] * pw)
    pang = a_im[None] * dt[None] * pw
    p_re = (pmag * jnp.cos(pang)).reshape(8, S5_STATE)
    p_im = (pmag * jnp.sin(pang)).reshape(8, S5_STATE)
    row = jnp.arange(8)[:, None]
    tabs = [p_re, p_im]
    for dd in (1, 2, 4):
        tabs.append(jnp.where(row >= dd, p_re[dd - 1][None, :], 0.0))
        tabs.append(jnp.where(row >= dd, p_im[dd - 1][None, :], 0.0))
    return bw, cre, cim, jnp.concatenate(tabs, axis=0)


@jax.jit
def kernel(x, meta_tokens, ln_in_g, ln_in_b, w_in, gdn_conv_w, gdn_A_log, gdn_dt_bias, gdn_norm_w, m2_conv_w, m2_conv_b, m2_dt_bias, m2_A_log, m2_D, m2_norm_w, hg_lb_logits, hg_norm_w, s5_A_re, s5_A_im, s5_B_re, s5_B_im, s5_C_re, s5_C_im, s5_D, s5_log_dt, s5_glu_w1, s5_glu_w2, w_branch, w_out, ln_g, ln_b):
    bsz, seq, d = x.shape
    ltot = N_META + seq
    lp = -(-ltot // TILE) * TILE
    pad = lp - ltot
    meta = jnp.broadcast_to(meta_tokens.astype(x.dtype)[None], (bsz, N_META, d))
    xin = jnp.concatenate([jnp.zeros((bsz, pad, d), x.dtype), meta, x], axis=1)
    h = _ln_in_call(xin, ln_in_g.astype(F32), ln_in_b.astype(F32), pad)

    msk = _masks64()
    tri = msk[0].astype(BF)
    ones = jnp.ones((CH, CH), BF)
    jj = jnp.arange(256)
    eyet = (jnp.arange(CH)[:, None] == (jj % CH)[None, :]).astype(F32)
    maskt = (jnp.arange(CH)[:, None] >= (jj % CH)[None, :]).astype(F32)
    bd = ((jj // CH)[:, None] == (jj // CH)[None, :]).astype(F32)

    cum = jnp.cumsum(jax.nn.softmax(hg_lb_logits.astype(F32), axis=0), axis=0)
    lower_bounds = cum - cum[0:1]

    for l in range(w_in.shape[0]):
        w1, wg = _prep_w1(w_in[l])
        bw, cre, cim, pw_tab = _prep_s5(s5_A_re[l], s5_A_im[l], s5_B_re[l], s5_B_im[l], s5_C_re[l], s5_C_im[l],
                                        s5_log_dt[l])
        p, yd = _proj_s5_call(h, w1, bw, cre, cim, pw_tab, s5_D[l].astype(F32).reshape(1, -1),
                              s5_glu_w1[l].astype(BF), s5_glu_w2[l].astype(BF))

        lane = jnp.stack([_lane_row(jnp.exp(gdn_A_log[l].astype(F32)), 4), _lane_row(gdn_dt_bias[l], 4)]
                         + [jnp.zeros((128,), F32)] * 6)
        gdn_prm = (gdn_conv_w[l].astype(F32), lane, gdn_norm_w[l].astype(F32).reshape(1, GDN_DK))

        rep = lambda v: jnp.repeat(v.astype(F32), M2_HEADDIM)
        vec = jnp.stack([rep(m2_dt_bias[l]), rep(-jnp.exp(m2_A_log[l].astype(F32))), rep(m2_D[l]),
                         m2_norm_w[l].astype(F32)] + [jnp.zeros((BRANCH_W,), F32)] * 4)
        ssd_prm = (m2_conv_w[l].astype(F32), m2_conv_b[l].astype(F32).reshape(1, -1), vec, eyet, maskt, bd, ones)

        lb = lower_bounds[l]
        hvec = jnp.stack([jnp.log(lb), jnp.log1p(-lb), 1.0 - lb] + [jnp.zeros((BRANCH_W,), F32)] * 5)
        hg_prm = (hvec, hg_norm_w[l].astype(F32).reshape(1, HG_DK))

        h = _mix_merge_call(h, p, yd, gdn_prm, ssd_prm, hg_prm, msk, tri, wg, w_branch[l].astype(BF),
                            w_out[l].astype(BF), ln_g[l].astype(F32), ln_b[l].astype(F32), pad)
    return h[:, pad + N_META:]
```

```python
import functools
import math

import jax
import jax.numpy as jnp
from jax import lax
from jax.experimental import pallas as pl
from jax.experimental.pallas import tpu as pltpu

F32 = jnp.float32
BF = jnp.bfloat16

D_MODEL = 1024
DEPTH = 4
N_META = 16
CONV_K = 4
N_BRANCH = 4
BRANCH_W = 512
GDN_HEADS = 4
GDN_DK = 128
M2_HEADS = 8
M2_HEADDIM = 64
M2_GROUPS = 2
M2_DSTATE = 128
HG_HEADS = 4
HG_DK = 128
S5_GROUP = 16
S5_NG = 32
S5_P = 64
S5_STATE = S5_NG * S5_P
ALPHA = (2 * DEPTH) ** 0.25
LN_EPS = 1e-5
RMS_EPS = 1e-6

TILE = 256
CH = 64
NCH = TILE // CH
SUB = 16
HG_EXP_CLAMP = 80.0
VMEM_LIMIT = 56 * 1024 * 1024

OFF_QKV, OFF_GZ, OFF_XBC, OFF_MZ, OFF_MDT = 0, 1536, 2048, 3072, 3584
OFF_HG, OFF_BA, OFF_S5 = 4096, 6144, 6272
NP = 6272
N1 = 7296


def _sigmoid(x):
    return 0.5 * (jnp.tanh(0.5 * x) + 1.0)


def _silu(x):
    return x * _sigmoid(x)


def _softplus(x):
    return jnp.maximum(x, 0.0) + jnp.log1p(jnp.exp(-jnp.abs(x)))


def _mm(a, b):
    return jnp.dot(a.astype(BF), b.astype(BF), preferred_element_type=F32)


def _mm_nt(a, b):
    return lax.dot_general(a.astype(BF), b.astype(BF), (((1,), (1,)), ((), ())),
                           preferred_element_type=F32)


def _mm_tn(a, b):
    return lax.dot_general(a.astype(BF), b.astype(BF), (((0,), (0,)), ((), ())),
                           preferred_element_type=F32)


def _mm3_left(m_bf, x):
    hi = x.astype(BF)
    r1 = x - hi.astype(F32)
    mid = r1.astype(BF)
    lo = (r1 - mid.astype(F32)).astype(BF)
    d = lambda v: jnp.dot(m_bf, v, preferred_element_type=F32)
    return (d(lo) + d(mid)) + d(hi)


def _rms_norm(x):
    return x * lax.rsqrt(jnp.mean(x * x, axis=-1, keepdims=True) + RMS_EPS)


def _masks64():
    i = jnp.arange(CH)[:, None]
    j = jnp.arange(CH)[None, :]
    ms = [(i >= j), (i > j)]
    for s in (1, 2, 4, 8, 16, 32):
        ms.append((i // (2 * s) == j // (2 * s)) & (i % (2 * s) >= s) & (j % (2 * s) < s))
    ms.append(i == j)
    return jnp.stack(ms).astype(F32)


def _cparams():
    return pltpu.CompilerParams(dimension_semantics=("arbitrary", "arbitrary"),
                                vmem_limit_bytes=VMEM_LIMIT)


def _full(shape):
    n = len(shape)
    return pl.BlockSpec(shape, lambda b, t: (0,) * n, pipeline_mode=pl.Buffered(1))


def _ln_in_kernel(x_ref, m_ref, g_ref, b_ref, o_ref, *, pad):
    t = pl.program_id(1)
    x = jnp.where(t == 0, m_ref[...], x_ref[...])
    mu = jnp.mean(x, axis=-1, keepdims=True)
    xc = x - mu
    var = jnp.mean(xc * xc, axis=-1, keepdims=True)
    y = xc * lax.rsqrt(var + LN_EPS) * g_ref[...] + b_ref[...]
    row = t * TILE + lax.broadcasted_iota(jnp.int32, (TILE, 1), 0)
    o_ref[...] = jnp.where(row >= pad, y, 0.0)


def _ln_in_call(x, meta_tile, g, b, pad):
    bsz, seq, d = x.shape
    return pl.pallas_call(
        functools.partial(_ln_in_kernel, pad=pad),
        grid=(bsz, seq // TILE + 1),
        in_specs=[pl.BlockSpec((None, TILE, d), lambda b_, t: (b_, jnp.maximum(t - 1, 0), 0)),
                  _full((TILE, d)), _full((1, d)), _full((1, d))],
        out_specs=pl.BlockSpec((None, TILE, d), lambda b_, t: (b_, t, 0)),
        out_shape=jax.ShapeDtypeStruct((bsz, seq + TILE, d), F32),
        compiler_params=_cparams(),
        name="ln_in",
    )(x, meta_tile, g.reshape(1, d), b.reshape(1, d))


S5_CG = 512
S5_NBLK = TILE // 8
PROJ_CHUNK = 512
GATE_CHUNK = 256


def _s5_scan_block(b, cg, carry, xr_ref, xi_ref, pw_ref):
    rows = slice(8 * b, 8 * b + 8)
    cs = slice(cg * S5_CG, (cg + 1) * S5_CG)
    cr, ci = carry
    r = xr_ref[rows, cs]
    im = xi_ref[rows, cs]
    for n, d in enumerate((1, 2, 4)):
        lr = pw_ref[16 + 16 * n:24 + 16 * n, cs]
        li = pw_ref[24 + 16 * n:32 + 16 * n, cs]
        rsh = pltpu.roll(r, d, 0)
        ish = pltpu.roll(im, d, 0)
        r, im = r + (lr * rsh - li * ish), im + (lr * ish + li * rsh)
    l8r = pw_ref[0:8, cs]
    l8i = pw_ref[8:16, cs]
    r, im = r + (l8r * cr - l8i * ci), im + (l8r * ci + l8i * cr)
    xr_ref[rows, cs] = r
    xi_ref[rows, cs] = im
    return r[7:8, :], im[7:8, :]


def _proj_s5_kernel(h_ref, w_ref, bw_ref, cre_ref, cim_ref, pw_ref, d_ref, g1_ref, g2_ref,
                    p_ref, yd_ref, xr_ref, xi_ref, c_ref):
    t = pl.program_id(1)

    @pl.when(t == 0)
    def _():
        c_ref[...] = jnp.zeros_like(c_ref)

    hb = h_ref[...].astype(BF)
    u = jnp.dot(hb, w_ref[:, OFF_S5:OFF_S5 + BRANCH_W], preferred_element_type=F32)
    ub = u.astype(BF)
    for j in range(4):
        bu = jnp.dot(ub[:, j * 128:(j + 1) * 128], bw_ref[j], preferred_element_type=F32)
        xr_ref[:, j * 512:(j + 1) * 512] = bu[:, :512]
        xi_ref[:, j * 512:(j + 1) * 512] = bu[:, 512:]

    ncg = S5_STATE // S5_CG
    carries = [(c_ref[0:1, cg * S5_CG:(cg + 1) * S5_CG], c_ref[1:2, cg * S5_CG:(cg + 1) * S5_CG])
               for cg in range(ncg)]
    work = [(b, cg) for b in range(S5_NBLK) for cg in range(ncg)]
    starts = list(range(0, NP, PROJ_CHUNK))
    per = -(-len(work) // len(starts))
    for n, j in enumerate(starts):
        w = min(PROJ_CHUNK, NP - j)
        p_ref[:, j:j + w] = jnp.dot(hb, w_ref[:, j:j + w], preferred_element_type=F32)
        for b, cg in work[n * per:(n + 1) * per]:
            carries[cg] = _s5_scan_block(b, cg, carries[cg], xr_ref, xi_ref, pw_ref)
    for cg in range(ncg):
        c_ref[0:1, cg * S5_CG:(cg + 1) * S5_CG] = carries[cg][0]
        c_ref[1:2, cg * S5_CG:(cg + 1) * S5_CG] = carries[cg][1]

    ys = []
    for j in range(4):
        cs = slice(j * 512, (j + 1) * 512)
        ys.append(_mm(xr_ref[:, cs], cre_ref[j]) - _mm(xi_ref[:, cs], cim_ref[j]))
    y = jnp.concatenate(ys, axis=1) + d_ref[...] * u
    y = 0.5 * y * (1.0 + jnp.tanh(math.sqrt(2.0 / math.pi) * (y + 0.044715 * (y * y * y))))
    yb = y.astype(BF)
    glu = jnp.dot(yb, g1_ref[...], preferred_element_type=F32) * _sigmoid(
        jnp.dot(yb, g2_ref[...], preferred_element_type=F32))
    z = jnp.dot(hb, w_ref[:, OFF_S5 + BRANCH_W:OFF_S5 + 2 * BRANCH_W], preferred_element_type=F32)
    yd_ref[...] = (glu * _silu(z)).astype(yd_ref.dtype)


def _proj_s5_call(h, w1, bw, cre, cim, pw, dvec, g1, g2):
    bsz, lp, d = h.shape
    return pl.pallas_call(
        _proj_s5_kernel,
        grid=(bsz, lp // TILE),
        in_specs=[pl.BlockSpec((None, TILE, d), lambda b_, t: (b_, t, 0)), _full((d, N1)),
                  _full((4, 128, 1024)), _full((4, 512, 128)), _full((4, 512, 128)), _full((64, S5_STATE)),
                  _full((1, 512)), _full((512, 512)), _full((512, 512))],
        out_specs=[pl.BlockSpec((None, TILE, NP), lambda b_, t: (b_, t, 0)),
                   pl.BlockSpec((None, TILE, BRANCH_W), lambda b_, t: (b_, t, 0))],
        out_shape=[jax.ShapeDtypeStruct((bsz, lp, NP), F32), jax.ShapeDtypeStruct((bsz, lp, BRANCH_W), BF)],
        scratch_shapes=[pltpu.VMEM((TILE, S5_STATE), F32), pltpu.VMEM((TILE, S5_STATE), F32),
                        pltpu.VMEM((2, S5_STATE), F32)],
        compiler_params=_cparams(),
        name="proj_s5",
    )(h, w1, bw, cre, cim, pw, dvec, g1, g2)


def _conv_chunk(xp_ref, cw, r0):
    x = xp_ref[r0 + 8:r0 + 8 + CH, :] * cw[CONV_K - 1:CONV_K, :]
    for j in range(CONV_K - 1):
        s = r0 + 8 - (CONV_K - 1) + j
        x = x + xp_ref[s:s + CH, :] * cw[j:j + 1, :]
    return x


def _tri_inverse_many(a_list, msk_ref):
    xs = [msk_ref[8] - a * msk_ref[2] for a in a_list]
    for lvl in range(1, 6):
        m = msk_ref[2 + lvl]
        t1 = [_mm(x, a * m) for x, a in zip(xs, a_list)]
        xs = [x - _mm(t, x) for x, t in zip(xs, t1)]
    return xs


def _gdn_kernel(qkv_ref, z_ref, ba_ref, cw_ref, lane_ref, nw_ref, msk_ref, tri_ref, y_ref, xp_ref, s_ref,
                fill=lambda: None):
    t = pl.program_id(1)

    @pl.when(t == 0)
    def _():
        xp_ref[0:8, :] = jnp.zeros((8, 3 * BRANCH_W), F32)
        s_ref[...] = jnp.zeros_like(s_ref)

    xp_ref[8:8 + TILE, :] = qkv_ref[...]
    cw = cw_ref[...]
    exp_a = lane_ref[0:1, :]
    dt_bias = lane_ref[1:2, :]
    nw = nw_ref[...]
    causal = msk_ref[0]
    strict = msk_ref[1]
    tri = tri_ref[...]
    items = []
    a_list = []
    for c in range(NCH):
        r0 = c * CH
        qkv = _silu(_conv_chunk(xp_ref, cw, r0))
        ba = ba_ref[r0:r0 + CH, :]
        beta_all = _sigmoid(ba)
        gl = -(exp_a * _softplus(ba + dt_bias))
        gc = _mm3_left(tri, gl)
        gc_t = gc.T
        for h in range(GDN_HEADS):
            sl = slice(h * GDN_DK, (h + 1) * GDN_DK)
            q = qkv[:, sl]
            k = qkv[:, BRANCH_W + h * GDN_DK:BRANCH_W + (h + 1) * GDN_DK]
            v = qkv[:, 2 * BRANCH_W + h * GDN_DK:2 * BRANCH_W + (h + 1) * GDN_DK]
            q = q * (lax.rsqrt(jnp.sum(q * q, axis=-1, keepdims=True) + RMS_EPS) * GDN_DK ** -0.5)
            k = k * lax.rsqrt(jnp.sum(k * k, axis=-1, keepdims=True) + RMS_EPS)
            beta = beta_all[:, h:h + 1]
            gcol = gc[:, 4 + h:5 + h]
            grow = gc_t[4 + h:5 + h, :]
            decay = jnp.exp(jnp.where(causal > 0, gcol - grow, -jnp.inf))
            kb = k * beta
            eg = jnp.exp(gcol)
            glast = gc[CH - 1:CH, 4 + h:5 + h]
            a_list.append(strict * (_mm_nt(kb, k) * decay))
            items.append(dict(
                rhs=jnp.concatenate([v * beta, kb * eg], axis=1),
                qk=causal * (_mm_nt(q, k) * decay),
                qd=q * eg,
                kd=k * jnp.exp(glast - gcol),
                gtot=jnp.exp(glast)))
            fill()
    tinvs = _tri_inverse_many(a_list, msk_ref)
    uws = [_mm(tinv, it["rhs"]) for tinv, it in zip(tinvs, items)]
    kd_uw = [_mm_tn(it["kd"], uw) for it, uw in zip(items, uws)]
    qk_uw = [_mm(it["qk"], uw) for it, uw in zip(items, uws)]
    states = [s_ref[h] for h in range(GDN_HEADS)]
    for c in range(NCH):
        r0 = c * CH
        for h in range(GDN_HEADS):
            sl = slice(h * GDN_DK, (h + 1) * GDN_DK)
            n = c * GDN_HEADS + h
            s = states[h]
            o = _mm(items[n]["qd"] - qk_uw[n][:, GDN_DK:], s) + qk_uw[n][:, :GDN_DK]
            states[h] = (s * items[n]["gtot"] - _mm(kd_uw[n][:, GDN_DK:], s)) + kd_uw[n][:, :GDN_DK]
            y = _rms_norm(o) * nw * _silu(z_ref[r0:r0 + CH, sl])
            y_ref[r0:r0 + CH, sl] = y.astype(y_ref.dtype)
    for h in range(GDN_HEADS):
        s_ref[h] = states[h]
    xp_ref[0:8, :] = xp_ref[TILE:TILE + 8, :]


def _ssd_kernel(xbc_ref, zdt_ref, cw_ref, cb_ref, vec_ref, eyet_ref, maskt_ref, bd_ref, tri_ref, ones_ref,
                y_ref, xp_ref, s_ref, *, pad, fill=lambda: None):
    t = pl.program_id(1)

    @pl.when(t == 0)
    def _():
        xp_ref[0:8, :] = jnp.zeros((8, 1024), F32)
        s_ref[...] = jnp.zeros_like(s_ref)

    xp_ref[8:8 + TILE, :] = xbc_ref[...]
    cw = cw_ref[...]
    cb = cb_ref[...]
    dt_bias = vec_ref[0:1, :]
    neg_a = vec_ref[1:2, :]
    d_skip = vec_ref[2:3, :]
    nw = vec_ref[3:4, :]
    tri = tri_ref[...]
    ones = ones_ref[...]
    gw = M2_HEADDIM * (M2_HEADS // M2_GROUPS)
    for c in range(NCH):
        r0 = c * CH
        xbc = _silu(_conv_chunk(xp_ref, cw, r0) + cb)
        xs = xbc[:, :BRANCH_W]
        row = t * TILE + r0 + lax.broadcasted_iota(jnp.int32, (CH, 1), 0)
        valid = (row >= pad).astype(F32)
        dt = _softplus(zdt_ref[r0:r0 + CH, BRANCH_W:] + dt_bias) * valid
        a = dt * neg_a
        acum = _mm3_left(tri, a)
        xdt = xs * dt
        z = zdt_ref[r0:r0 + CH, :BRANCH_W]
        for g in range(M2_GROUPS):
            gs = slice(g * gw, (g + 1) * gw)
            bg = xbc[:, BRANCH_W + g * M2_DSTATE:BRANCH_W + (g + 1) * M2_DSTATE]
            cg = xbc[:, BRANCH_W + M2_GROUPS * M2_DSTATE + g * M2_DSTATE:
                     BRANCH_W + M2_GROUPS * M2_DSTATE + (g + 1) * M2_DSTATE]
            ag = acum[:, gs]
            cbt = _mm_nt(cg, jnp.concatenate([bg] * 4, axis=0))
            arow = _mm3_left(ones, ag * eyet_ref[...])
            lm = jnp.exp(jnp.where(maskt_ref[...] > 0, ag - arow, -jnp.inf))
            xg = xdt[:, gs]
            xbd = jnp.concatenate([xg] * 4, axis=0) * bd_ref[...]
            y_diag = _mm(cbt * lm, xbd)
            alast = ag[CH - 1:CH, :]
            s = s_ref[g]
            y_off = _mm(cg, s) * jnp.exp(ag)
            s_ref[g] = s * jnp.exp(alast) + _mm_tn(bg, xg * jnp.exp(alast - ag))
            y = (y_diag + y_off + d_skip[:, gs] * xs[:, gs]) * _silu(z[:, gs])
            y = _rms_norm(y) * nw[:, gs]
            y_ref[r0:r0 + CH, gs] = y.astype(y_ref.dtype)
        fill()
    xp_ref[0:8, :] = xp_ref[TILE:TILE + 8, :]


def _hgrn_kernel(p_ref, vec_ref, nw_ref, msk_ref, tri_ref, y_ref, s_ref, fill=lambda: None):
    t = pl.program_id(1)

    @pl.when(t == 0)
    def _():
        s_ref[...] = jnp.zeros_like(s_ref)

    log_lb = vec_ref[0:1, :]
    log1m_lb = vec_ref[1:2, :]
    one_m_lb = vec_ref[2:3, :]
    nw = nw_ref[...]
    causal = msk_ref[0]
    tri = tri_ref[...]
    hsl = [slice(h * HG_DK, (h + 1) * HG_DK) for h in range(HG_HEADS)]
    nsub = CH // SUB
    prep = []
    for c in range(NCH):
        r0 = c * CH
        q = _silu(p_ref[r0:r0 + CH, 0:512])
        zf = p_ref[r0:r0 + CH, 512:1024]
        e = jnp.exp(-jnp.abs(zf))
        lsig = jnp.minimum(zf, 0.0) - jnp.log1p(e)
        bb = log1m_lb + lsig
        logf = jnp.maximum(log_lb, bb) + jnp.log1p(jnp.exp(-jnp.abs(log_lb - bb)))
        sig_neg = jnp.where(zf >= 0, e, 1.0) / (1.0 + e)
        k = one_m_lb * sig_neg
        gcum = _mm3_left(tri, logf)
        glast = gcum[CH - 1:CH, :]
        subs = []
        for i in range(nsub):
            rs = slice(i * SUB, (i + 1) * SUB)
            n = (i + 1) * SUB
            gref = gcum[i * SUB - 1:i * SUB, :] if i > 0 else jnp.zeros((1, BRANCH_W), F32)
            qt = q[rs] * jnp.exp(gcum[rs] - gref)
            kt = k[:n] * jnp.exp(jnp.minimum(gref - gcum[:n], HG_EXP_CLAMP))
            subs.append((qt, kt, qt * jnp.exp(gref)))
        prep.append(dict(kd=k * jnp.exp(glast - gcum), eg=jnp.exp(glast), subs=subs))
        fill()
    incs = [[_mm_tn(p_ref[c * CH:(c + 1) * CH, 1024 + h * HG_DK:1024 + (h + 1) * HG_DK], prep[c]["kd"][:, hsl[h]])
             for h in range(HG_HEADS)] for c in range(NCH)]
    states = [[s_ref[h] for h in range(HG_HEADS)]]
    for c in range(NCH):
        states.append([states[c][h] * prep[c]["eg"][:, hsl[h]] + incs[c][h] for h in range(HG_HEADS)])
    for h in range(HG_HEADS):
        s_ref[h] = states[NCH][h]
    ams = [[[_mm_nt(prep[c]["subs"][i][0][:, hsl[h]], prep[c]["subs"][i][1][:, hsl[h]])
             * causal[i * SUB:(i + 1) * SUB, :(i + 1) * SUB]
             for h in range(HG_HEADS)] for i in range(nsub)] for c in range(NCH)]
    for c in range(NCH):
        r0 = c * CH
        for h in range(HG_HEADS):
            iv = p_ref[r0:r0 + CH, 1024 + h * HG_DK:1024 + (h + 1) * HG_DK]
            o = jnp.concatenate(
                [_mm(ams[c][i][h], iv[:(i + 1) * SUB]) + _mm_nt(prep[c]["subs"][i][2][:, hsl[h]], states[c][h])
                 for i in range(nsub)], axis=0)
            z = p_ref[r0:r0 + CH, 1536 + h * HG_DK:1536 + (h + 1) * HG_DK]
            y_ref[r0:r0 + CH, hsl[h]] = (_rms_norm(o) * nw * _silu(z)).astype(y_ref.dtype)


def _mix_merge_kernel(h_ref, qkv_ref, gz_ref, ba_ref, xbc_ref, zdt_ref, hg_ref, yd_ref,
                      gcw_ref, glane_ref, gnw_ref, scw_ref, scb_ref, svec_ref, eyet_ref, maskt_ref, bd_ref, ones_ref,
                      hvec_ref, hnw_ref, msk_ref, tri_ref, wg_ref, wb_ref, wo_ref, g_ref, b_ref,
                      o_ref, gxp_ref, gs_ref, sxp_ref, ss_ref, hs_ref, ya_ref, yb_ref, yc_ref, gate_ref, *, pad):
    t = pl.program_id(1)
    h = h_ref[...]
    hb = h.astype(BF)

    def gate_piece(j):
        cs = slice(j * GATE_CHUNK, (j + 1) * GATE_CHUNK)
        gate_ref[:, cs] = jnp.tanh(jnp.dot(hb, wg_ref[:, cs], preferred_element_type=F32)) + 1.0

    def branch_piece(br, y_ref):
        cs = slice(br * D_MODEL, (br + 1) * D_MODEL)
        gate_ref[:, cs] = gate_ref[:, cs] * jnp.dot(y_ref[...], wb_ref[br], preferred_element_type=F32)

    pending = [functools.partial(gate_piece, j) for j in range(N_BRANCH * D_MODEL // GATE_CHUNK)]
    pending.append(functools.partial(branch_piece, 3, yd_ref))

    def fill(n):
        for _ in range(min(n, len(pending))):
            pending.pop(0)()

    _gdn_kernel(qkv_ref, gz_ref, ba_ref, gcw_ref, glane_ref, gnw_ref, msk_ref, tri_ref, ya_ref, gxp_ref, gs_ref,
                fill=functools.partial(fill, 1))
    fill(len(pending))
    pending.append(functools.partial(branch_piece, 0, ya_ref))
    _ssd_kernel(xbc_ref, zdt_ref, scw_ref, scb_ref, svec_ref, eyet_ref, maskt_ref, bd_ref, tri_ref, ones_ref,
                yb_ref, sxp_ref, ss_ref, pad=pad, fill=functools.partial(fill, 1))
    fill(len(pending))
    pending.append(functools.partial(branch_piece, 1, yb_ref))
    _hgrn_kernel(hg_ref, hvec_ref, hnw_ref, msk_ref, tri_ref, yc_ref, hs_ref, fill=functools.partial(fill, 1))
    fill(len(pending))
    branch_piece(2, yc_ref)
    mixed = ((gate_ref[:, 0:D_MODEL] + gate_ref[:, D_MODEL:2 * D_MODEL])
             + (gate_ref[:, 2 * D_MODEL:3 * D_MODEL] + gate_ref[:, 3 * D_MODEL:4 * D_MODEL]))
    out = jnp.dot(mixed.astype(BF), wo_ref[...], preferred_element_type=F32)
    r = ALPHA * h + out
    mu = jnp.mean(r, axis=-1, keepdims=True)
    rc = r - mu
    var = jnp.mean(rc * rc, axis=-1, keepdims=True)
    y = rc * lax.rsqrt(var + LN_EPS) * g_ref[...] + b_ref[...]
    row = t * TILE + lax.broadcasted_iota(jnp.int32, (TILE, 1), 0)
    o_ref[...] = jnp.where(row >= pad, y, 0.0)


def _mix_merge_call(h, p, yd, gdn_prm, ssd_prm, hg_prm, msk, tri, wg, wb, wo, g, b, pad, drop_first_tile):
    bsz, lp, d = h.shape
    tok = lambda w, j=0: pl.BlockSpec((None, TILE, w), lambda b_, t: (b_, t, j))
    return pl.pallas_call(
        functools.partial(_mix_merge_kernel, pad=pad),
        grid=(bsz, lp // TILE),
        in_specs=[tok(d), tok(1536, OFF_QKV // 1536), tok(512, OFF_GZ // 512), tok(128, OFF_BA // 128),
                  tok(1024, OFF_XBC // 1024), tok(1024, OFF_MZ // 1024), tok(2048, OFF_HG // 2048), tok(BRANCH_W),
                  _full((CONV_K, 1536)), _full((8, 128)), _full((1, GDN_DK)),
                  _full((CONV_K, 1024)), _full((1, 1024)), _full((8, 512)),
                  _full((CH, 256)), _full((CH, 256)), _full((256, 256)), _full((CH, CH)),
                  _full((8, 512)), _full((1, HG_DK)), _full((9, CH, CH)), _full((CH, CH)),
                  _full((d, N_BRANCH * d)), _full((N_BRANCH, BRANCH_W, d)), _full((d, d)),
                  _full((1, d)), _full((1, d))],
        out_specs=(pl.BlockSpec((None, TILE, d), lambda b_, t: (b_, jnp.maximum(t - 1, 0), 0))
                   if drop_first_tile else tok(d)),
        out_shape=jax.ShapeDtypeStruct((bsz, lp - TILE if drop_first_tile else lp, d), F32),
        scratch_shapes=[pltpu.VMEM((TILE + 8, 1536), F32), pltpu.VMEM((GDN_HEADS, GDN_DK, GDN_DK), F32),
                        pltpu.VMEM((TILE + 8, 1024), F32), pltpu.VMEM((M2_GROUPS, M2_DSTATE, 256), F32),
                        pltpu.VMEM((HG_HEADS, HG_DK, HG_DK), F32),
                        pltpu.VMEM((TILE, BRANCH_W), BF), pltpu.VMEM((TILE, BRANCH_W), BF),
                        pltpu.VMEM((TILE, BRANCH_W), BF), pltpu.VMEM((TILE, N_BRANCH * D_MODEL), F32)],
        compiler_params=_cparams(),
        name="mix_merge",
    )(h, p, p, p, p, p, p, yd, *gdn_prm, *ssd_prm, *hg_prm, msk, tri, wg, wb, wo, g.reshape(1, d), b.reshape(1, d))


def _split_w_in(w):
    sizes = (1536, 512, 4, 4, 1024, 512, 8, 512, 512, 512, 512, 512, 512, 4096)
    offs = [0]
    for s in sizes:
        offs.append(offs[-1] + s)
    return [w[:, offs[i]:offs[i + 1]] for i in range(len(sizes))]


def _prep_w1(w):
    (qkv, gz, gb, ga, xbc, mz, mdt, hq, hf, hi, hz, su, sz, gates) = _split_w_in(w)
    d = w.shape[0]
    ba = jnp.concatenate([gb, ga, jnp.zeros((d, 120), w.dtype)], axis=1)
    mdt_e = jnp.repeat(mdt, M2_HEADDIM, axis=1)
    w1 = jnp.concatenate([qkv, gz, xbc, mz, mdt_e, hq, hf, hi, hz, ba, su, sz], axis=1)
    return w1.astype(BF), (0.5 * gates).astype(BF)


def _lane_row(vals, off, width=128):
    return jnp.zeros((width,), F32).at[off:off + vals.shape[0]].set(vals.astype(F32))


def _prep_s5(a_re, a_im, b_re, b_im, c_re, c_im, log_dt):
    f = lambda v: v.astype(F32)
    a_re, a_im, b_re, b_im, c_re, c_im = map(f, (a_re, a_im, b_re, b_im, c_re, c_im))
    dt = jnp.exp(f(log_dt))[:, None]
    mag = jnp.exp(a_re * dt)
    lam_re, lam_im = mag * jnp.cos(a_im * dt), mag * jnp.sin(a_im * dt)
    den = jnp.square(a_re) + jnp.square(a_im)
    nr, ni = lam_re - 1.0, lam_im
    z_re, z_im = (nr * a_re + ni * a_im) / den, (ni * a_re - nr * a_im) / den
    bb_re = z_re[..., None] * b_re - z_im[..., None] * b_im
    bb_im = z_re[..., None] * b_im + z_im[..., None] * b_re
    eye8 = jnp.eye(8, dtype=F32)

    def in_blocks(bb):
        blk = jnp.transpose(bb, (0, 2, 1)).reshape(4, 8, S5_GROUP, S5_P)
        return jnp.einsum('jicp,ik->jickp', blk, eye8).reshape(4, 128, 512)

    def out_blocks(cc):
        blk = jnp.transpose(cc, (0, 2, 1)).reshape(4, 8, S5_P, S5_GROUP)
        return jnp.einsum('jipc,ik->jipkc', blk, eye8).reshape(4, 512, 128)

    bw = jnp.concatenate([in_blocks(bb_re), in_blocks(bb_im)], axis=2).astype(BF)
    cre = out_blocks(c_re).astype(BF)
    cim = out_blocks(c_im).astype(BF)
    pw = jnp.arange(1, 9, dtype=F32)[:, None, None]
    pmag = jnp.exp(a_re[None] * dt[None] * pw)
    pang = a_im[None] * dt[None] * pw
    p_re = (pmag * jnp.cos(pang)).reshape(8, S5_STATE)
    p_im = (pmag * jnp.sin(pang)).reshape(8, S5_STATE)
    row = jnp.arange(8)[:, None]
    tabs = [p_re, p_im]
    for dd in (1, 2, 4):
        tabs.append(jnp.where(row >= dd, p_re[dd - 1][None, :], 0.0))
        tabs.append(jnp.where(row >= dd, p_im[dd - 1][None, :], 0.0))
    return bw, cre, cim, jnp.concatenate(tabs, axis=0)


@jax.jit
def kernel(x, meta_tokens, ln_in_g, ln_in_b, w_in, gdn_conv_w, gdn_A_log, gdn_dt_bias, gdn_norm_w, m2_conv_w, m2_conv_b, m2_dt_bias, m2_A_log, m2_D, m2_norm_w, hg_lb_logits, hg_norm_w, s5_A_re, s5_A_im, s5_B_re, s5_B_im, s5_C_re, s5_C_im, s5_D, s5_log_dt, s5_glu_w1, s5_glu_w2, w_branch, w_out, ln_g, ln_b):
    bsz, seq, d = x.shape
    ltot = N_META + seq
    lp = -(-ltot // TILE) * TILE
    pad = lp - ltot
    assert pad + N_META == TILE, "real tokens must start on the second tile"
    meta_tile = jnp.concatenate([jnp.zeros((pad, d), F32), meta_tokens.astype(F32)], axis=0)
    h = _ln_in_call(x.astype(F32), meta_tile, ln_in_g.astype(F32), ln_in_b.astype(F32), pad)

    msk = _masks64()
    tri = msk[0].astype(BF)
    ones = jnp.ones((CH, CH), BF)
    jj = jnp.arange(256)
    eyet = (jnp.arange(CH)[:, None] == (jj % CH)[None, :]).astype(F32)
    maskt = (jnp.arange(CH)[:, None] >= (jj % CH)[None, :]).astype(F32)
    bd = ((jj // CH)[:, None] == (jj // CH)[None, :]).astype(F32)

    cum = jnp.cumsum(jax.nn.softmax(hg_lb_logits.astype(F32), axis=0), axis=0)
    lower_bounds = cum - cum[0:1]

    for l in range(w_in.shape[0]):
        w1, wg = _prep_w1(w_in[l])
        bw, cre, cim, pw_tab = _prep_s5(s5_A_re[l], s5_A_im[l], s5_B_re[l], s5_B_im[l], s5_C_re[l], s5_C_im[l],
                                        s5_log_dt[l])
        p, yd = _proj_s5_call(h, w1, bw, cre, cim, pw_tab, s5_D[l].astype(F32).reshape(1, -1),
                              s5_glu_w1[l].astype(BF), s5_glu_w2[l].astype(BF))

        lane = jnp.stack([_lane_row(jnp.exp(gdn_A_log[l].astype(F32)), 4), _lane_row(gdn_dt_bias[l], 4)]
                         + [jnp.zeros((128,), F32)] * 6)
        gdn_prm = (gdn_conv_w[l].astype(F32), lane, gdn_norm_w[l].astype(F32).reshape(1, GDN_DK))

        rep = lambda v: jnp.repeat(v.astype(F32), M2_HEADDIM)
        vec = jnp.stack([rep(m2_dt_bias[l]), rep(-jnp.exp(m2_A_log[l].astype(F32))), rep(m2_D[l]),
                         m2_norm_w[l].astype(F32)] + [jnp.zeros((BRANCH_W,), F32)] * 4)
        ssd_prm = (m2_conv_w[l].astype(F32), m2_conv_b[l].astype(F32).reshape(1, -1), vec, eyet, maskt, bd, ones)

        lb = lower_bounds[l]
        hvec = jnp.stack([jnp.log(lb), jnp.log1p(-lb), 1.0 - lb] + [jnp.zeros((BRANCH_W,), F32)] * 5)
        hg_prm = (hvec, hg_norm_w[l].astype(F32).reshape(1, HG_DK))

        h = _mix_merge_call(h, p, yd, gdn_prm, ssd_prm, hg_prm, msk, tri, wg, (0.5 * w_branch[l]).astype(BF),
                            w_out[l].astype(BF), ln_g[l].astype(F32), ln_b[l].astype(F32), pad,
                            drop_first_tile=(l == w_in.shape[0] - 1))
    return h.astype(x.dtype)
```

```python
import functools
import math

import jax
import jax.numpy as jnp
from jax import lax
from jax.experimental import pallas as pl
from jax.experimental.pallas import tpu as pltpu

F32 = jnp.float32
BF = jnp.bfloat16

D_MODEL = 1024
DEPTH = 4
N_META = 16
CONV_K = 4
N_BRANCH = 4
BRANCH_W = 512
GDN_HEADS = 4
GDN_DK = 128
M2_HEADS = 8
M2_HEADDIM = 64
M2_GROUPS = 2
M2_DSTATE = 128
HG_HEADS = 4
HG_DK = 128
S5_GROUP = 16
S5_NG = 32
S5_P = 64
S5_STATE = S5_NG * S5_P
ALPHA = (2 * DEPTH) ** 0.25
LN_EPS = 1e-5
RMS_EPS = 1e-6

TILE = 256
CH = 64
NCH = TILE // CH
SUB = 16
HG_EXP_CLAMP = 80.0
VMEM_LIMIT = 56 * 1024 * 1024

OFF_QKV, OFF_GZ, OFF_XBC, OFF_MZ, OFF_MDT = 0, 1536, 2048, 3072, 3584
OFF_HG, OFF_BA, OFF_S5 = 4096, 6144, 6272
NP = 6272
N1 = 7296


def _sigmoid(x):
    return 0.5 * (jnp.tanh(0.5 * x) + 1.0)


def _silu(x):
    return x * _sigmoid(x)


def _softplus(x):
    return jnp.maximum(x, 0.0) + jnp.log1p(jnp.exp(-jnp.abs(x)))


def _mm(a, b):
    return jnp.dot(a.astype(BF), b.astype(BF), preferred_element_type=F32)


def _mm_nt(a, b):
    return lax.dot_general(a.astype(BF), b.astype(BF), (((1,), (1,)), ((), ())),
                           preferred_element_type=F32)


def _mm_tn(a, b):
    return lax.dot_general(a.astype(BF), b.astype(BF), (((0,), (0,)), ((), ())),
                           preferred_element_type=F32)


def _mm3_left(m_bf, x):
    hi = x.astype(BF)
    r1 = x - hi.astype(F32)
    mid = r1.astype(BF)
    lo = (r1 - mid.astype(F32)).astype(BF)
    d = lambda v: jnp.dot(m_bf, v, preferred_element_type=F32)
    return (d(lo) + d(mid)) + d(hi)


def _rms_norm(x):
    return x * lax.rsqrt(jnp.mean(x * x, axis=-1, keepdims=True) + RMS_EPS)


def _masks64():
    i = jnp.arange(CH)[:, None]
    j = jnp.arange(CH)[None, :]
    ms = [(i >= j), (i > j)]
    for s in (1, 2, 4, 8, 16, 32):
        ms.append((i // (2 * s) == j // (2 * s)) & (i % (2 * s) >= s) & (j % (2 * s) < s))
    ms.append(i == j)
    return jnp.stack(ms).astype(F32)


def _cparams():
    return pltpu.CompilerParams(dimension_semantics=("arbitrary", "arbitrary"),
                                vmem_limit_bytes=VMEM_LIMIT)


def _full(shape):
    n = len(shape)
    return pl.BlockSpec(shape, lambda b, t: (0,) * n, pipeline_mode=pl.Buffered(1))


def _ln_in_kernel(x_ref, m_ref, g_ref, b_ref, o_ref, *, pad):
    t = pl.program_id(1)
    x = jnp.where(t == 0, m_ref[...], x_ref[...])
    mu = jnp.mean(x, axis=-1, keepdims=True)
    xc = x - mu
    var = jnp.mean(xc * xc, axis=-1, keepdims=True)
    y = xc * lax.rsqrt(var + LN_EPS) * g_ref[...] + b_ref[...]
    row = t * TILE + lax.broadcasted_iota(jnp.int32, (TILE, 1), 0)
    o_ref[...] = jnp.where(row >= pad, y, 0.0)


def _ln_in_call(x, meta_tile, g, b, pad):
    bsz, seq, d = x.shape
    return pl.pallas_call(
        functools.partial(_ln_in_kernel, pad=pad),
        grid=(bsz, seq // TILE + 1),
        in_specs=[pl.BlockSpec((None, TILE, d), lambda b_, t: (b_, jnp.maximum(t - 1, 0), 0)),
                  _full((TILE, d)), _full((1, d)), _full((1, d))],
        out_specs=pl.BlockSpec((None, TILE, d), lambda b_, t: (b_, t, 0)),
        out_shape=jax.ShapeDtypeStruct((bsz, seq + TILE, d), F32),
        compiler_params=_cparams(),
        name="ln_in",
    )(x, meta_tile, g.reshape(1, d), b.reshape(1, d))


S5_CG = 512
S5_NBLK = TILE // 8
PROJ_CHUNK = 512
GATE_CHUNK = 256
MIX_PLAN = "gsh" + "gfgfgfgfh" * 4 + "gs" * 8 + "gg" + "hDhA" + "hhBhh" + "C"


def _s5_scan_block(b, cg, carry, xr_ref, xi_ref, pw_ref):
    rows = slice(8 * b, 8 * b + 8)
    cs = slice(cg * S5_CG, (cg + 1) * S5_CG)
    cr, ci = carry
    r = xr_ref[rows, cs]
    im = xi_ref[rows, cs]
    for n, d in enumerate((1, 2, 4)):
        lr = pw_ref[16 + 16 * n:24 + 16 * n, cs]
        li = pw_ref[24 + 16 * n:32 + 16 * n, cs]
        rsh = pltpu.roll(r, d, 0)
        ish = pltpu.roll(im, d, 0)
        r, im = r + (lr * rsh - li * ish), im + (lr * ish + li * rsh)
    l8r = pw_ref[0:8, cs]
    l8i = pw_ref[8:16, cs]
    r, im = r + (l8r * cr - l8i * ci), im + (l8r * ci + l8i * cr)
    xr_ref[rows, cs] = r
    xi_ref[rows, cs] = im
    return r[7:8, :], im[7:8, :]


def _proj_s5_kernel(h_ref, w_ref, bw_ref, cre_ref, cim_ref, pw_ref, d_ref, g1_ref, g2_ref,
                    p_ref, yd_ref, xr_ref, xi_ref, c_ref):
    t = pl.program_id(1)

    @pl.when(t == 0)
    def _():
        c_ref[...] = jnp.zeros_like(c_ref)

    hb = h_ref[...].astype(BF)
    u = jnp.dot(hb, w_ref[:, OFF_S5:OFF_S5 + BRANCH_W], preferred_element_type=F32)
    ub = u.astype(BF)
    for j in range(4):
        bu = jnp.dot(ub[:, j * 128:(j + 1) * 128], bw_ref[j], preferred_element_type=F32)
        xr_ref[:, j * 512:(j + 1) * 512] = bu[:, :512]
        xi_ref[:, j * 512:(j + 1) * 512] = bu[:, 512:]

    ncg = S5_STATE // S5_CG
    carries = [(c_ref[0:1, cg * S5_CG:(cg + 1) * S5_CG], c_ref[1:2, cg * S5_CG:(cg + 1) * S5_CG])
               for cg in range(ncg)]
    work = [(b, cg) for b in range(S5_NBLK) for cg in range(ncg)]
    starts = list(range(0, NP, PROJ_CHUNK))
    per = -(-len(work) // len(starts))
    for n, j in enumerate(starts):
        w = min(PROJ_CHUNK, NP - j)
        p_ref[:, j:j + w] = jnp.dot(hb, w_ref[:, j:j + w], preferred_element_type=F32)
        for b, cg in work[n * per:(n + 1) * per]:
            carries[cg] = _s5_scan_block(b, cg, carries[cg], xr_ref, xi_ref, pw_ref)
    for cg in range(ncg):
        c_ref[0:1, cg * S5_CG:(cg + 1) * S5_CG] = carries[cg][0]
        c_ref[1:2, cg * S5_CG:(cg + 1) * S5_CG] = carries[cg][1]

    ys = []
    for j in range(4):
        cs = slice(j * 512, (j + 1) * 512)
        ys.append(_mm(xr_ref[:, cs], cre_ref[j]) - _mm(xi_ref[:, cs], cim_ref[j]))
    y = jnp.concatenate(ys, axis=1) + d_ref[...] * u
    y = 0.5 * y * (1.0 + jnp.tanh(math.sqrt(2.0 / math.pi) * (y + 0.044715 * (y * y * y))))
    yb = y.astype(BF)
    glu = jnp.dot(yb, g1_ref[...], preferred_element_type=F32) * _sigmoid(
        jnp.dot(yb, g2_ref[...], preferred_element_type=F32))
    z = jnp.dot(hb, w_ref[:, OFF_S5 + BRANCH_W:OFF_S5 + 2 * BRANCH_W], preferred_element_type=F32)
    yd_ref[...] = (glu * _silu(z)).astype(yd_ref.dtype)


def _proj_s5_call(h, w1, bw, cre, cim, pw, dvec, g1, g2):
    bsz, lp, d = h.shape
    return pl.pallas_call(
        _proj_s5_kernel,
        grid=(bsz, lp // TILE),
        in_specs=[pl.BlockSpec((None, TILE, d), lambda b_, t: (b_, t, 0)), _full((d, N1)),
                  _full((4, 128, 1024)), _full((4, 512, 128)), _full((4, 512, 128)), _full((64, S5_STATE)),
                  _full((1, 512)), _full((512, 512)), _full((512, 512))],
        out_specs=[pl.BlockSpec((None, TILE, NP), lambda b_, t: (b_, t, 0)),
                   pl.BlockSpec((None, TILE, BRANCH_W), lambda b_, t: (b_, t, 0))],
        out_shape=[jax.ShapeDtypeStruct((bsz, lp, NP), F32), jax.ShapeDtypeStruct((bsz, lp, BRANCH_W), BF)],
        scratch_shapes=[pltpu.VMEM((TILE, S5_STATE), F32), pltpu.VMEM((TILE, S5_STATE), F32),
                        pltpu.VMEM((2, S5_STATE), F32)],
        compiler_params=_cparams(),
        name="proj_s5",
    )(h, w1, bw, cre, cim, pw, dvec, g1, g2)


def _conv_chunk(xp_ref, cw, r0):
    x = xp_ref[r0 + 8:r0 + 8 + CH, :] * cw[CONV_K - 1:CONV_K, :]
    for j in range(CONV_K - 1):
        s = r0 + 8 - (CONV_K - 1) + j
        x = x + xp_ref[s:s + CH, :] * cw[j:j + 1, :]
    return x


def _gdn_steps(qkv_ref, z_ref, ba_ref, cw_ref, lane_ref, nw_ref, msk_ref, tri_ref, y_ref, xp_ref, s_ref):
    t = pl.program_id(1)

    @pl.when(t == 0)
    def _():
        xp_ref[0:8, :] = jnp.zeros((8, 3 * BRANCH_W), F32)
        s_ref[...] = jnp.zeros_like(s_ref)

    xp_ref[8:8 + TILE, :] = qkv_ref[...]
    yield
    cw = cw_ref[...]
    exp_a = lane_ref[0:1, :]
    dt_bias = lane_ref[1:2, :]
    nw = nw_ref[...]
    causal = msk_ref[0]
    strict = msk_ref[1]
    tri = tri_ref[...]
    items = []
    a_list = []
    for c in range(NCH):
        r0 = c * CH
        qkv = _silu(_conv_chunk(xp_ref, cw, r0))
        ba = ba_ref[r0:r0 + CH, :]
        beta_all = _sigmoid(ba)
        gl = -(exp_a * _softplus(ba + dt_bias))
        gc = _mm3_left(tri, gl)
        gc_t = gc.T
        for h in range(GDN_HEADS):
            sl = slice(h * GDN_DK, (h + 1) * GDN_DK)
            q = qkv[:, sl]
            k = qkv[:, BRANCH_W + h * GDN_DK:BRANCH_W + (h + 1) * GDN_DK]
            v = qkv[:, 2 * BRANCH_W + h * GDN_DK:2 * BRANCH_W + (h + 1) * GDN_DK]
            q = q * (lax.rsqrt(jnp.sum(q * q, axis=-1, keepdims=True) + RMS_EPS) * GDN_DK ** -0.5)
            k = k * lax.rsqrt(jnp.sum(k * k, axis=-1, keepdims=True) + RMS_EPS)
            beta = beta_all[:, h:h + 1]
            gcol = gc[:, 4 + h:5 + h]
            grow = gc_t[4 + h:5 + h, :]
            decay = jnp.exp(jnp.where(causal > 0, gcol - grow, -jnp.inf))
            kb = k * beta
            eg = jnp.exp(gcol)
            glast = gc[CH - 1:CH, 4 + h:5 + h]
            a_list.append(strict * (_mm_nt(kb, k) * decay))
            items.append(dict(
                rhs=jnp.concatenate([v * beta, kb * eg], axis=1),
                qk=causal * (_mm_nt(q, k) * decay),
                qd=q * eg,
                kd=k * jnp.exp(glast - gcol),
                gtot=jnp.exp(glast)))
            yield
    tinvs = [msk_ref[8] - a * msk_ref[2] for a in a_list]
    for lvl in range(1, 6):
        m = msk_ref[2 + lvl]
        t1 = [_mm(x, a * m) for x, a in zip(tinvs, a_list)]
        tinvs = [x - _mm(t_, x) for x, t_ in zip(tinvs, t1)]
        yield
    uws = [_mm(tinv, it["rhs"]) for tinv, it in zip(tinvs, items)]
    kd_uw = [_mm_tn(it["kd"], uw) for it, uw in zip(items, uws)]
    qk_uw = [_mm(it["qk"], uw) for it, uw in zip(items, uws)]
    yield
    states = [s_ref[h] for h in range(GDN_HEADS)]
    for c in range(NCH):
        r0 = c * CH
        for h in range(GDN_HEADS):
            sl = slice(h * GDN_DK, (h + 1) * GDN_DK)
            n = c * GDN_HEADS + h
            s = states[h]
            o = _mm(items[n]["qd"] - qk_uw[n][:, GDN_DK:], s) + qk_uw[n][:, :GDN_DK]
            states[h] = (s * items[n]["gtot"] - _mm(kd_uw[n][:, GDN_DK:], s)) + kd_uw[n][:, :GDN_DK]
            y = _rms_norm(o) * nw * _silu(z_ref[r0:r0 + CH, sl])
            y_ref[r0:r0 + CH, sl] = y.astype(y_ref.dtype)
        if c == NCH - 1:
            for h in range(GDN_HEADS):
                s_ref[h] = states[h]
            xp_ref[0:8, :] = xp_ref[TILE:TILE + 8, :]
        yield


def _ssd_steps(xbc_ref, zdt_ref, cw_ref, cb_ref, vec_ref, eyet_ref, maskt_ref, bd_ref, tri_ref, ones_ref,
               y_ref, xp_ref, s_ref, *, pad):
    t = pl.program_id(1)

    @pl.when(t == 0)
    def _():
        xp_ref[0:8, :] = jnp.zeros((8, 1024), F32)
        s_ref[...] = jnp.zeros_like(s_ref)

    xp_ref[8:8 + TILE, :] = xbc_ref[...]
    yield
    cw = cw_ref[...]
    cb = cb_ref[...]
    dt_bias = vec_ref[0:1, :]
    neg_a = vec_ref[1:2, :]
    d_skip = vec_ref[2:3, :]
    nw = vec_ref[3:4, :]
    tri = tri_ref[...]
    ones = ones_ref[...]
    gw = M2_HEADDIM * (M2_HEADS // M2_GROUPS)
    for c in range(NCH):
        r0 = c * CH
        xbc = _silu(_conv_chunk(xp_ref, cw, r0) + cb)
        xs = xbc[:, :BRANCH_W]
        row = t * TILE + r0 + lax.broadcasted_iota(jnp.int32, (CH, 1), 0)
        valid = (row >= pad).astype(F32)
        dt = _softplus(zdt_ref[r0:r0 + CH, BRANCH_W:] + dt_bias) * valid
        a = dt * neg_a
        acum = _mm3_left(tri, a)
        xdt = xs * dt
        z = zdt_ref[r0:r0 + CH, :BRANCH_W]
        for g in range(M2_GROUPS):
            gs = slice(g * gw, (g + 1) * gw)
            bg = xbc[:, BRANCH_W + g * M2_DSTATE:BRANCH_W + (g + 1) * M2_DSTATE]
            cg = xbc[:, BRANCH_W + M2_GROUPS * M2_DSTATE + g * M2_DSTATE:
                     BRANCH_W + M2_GROUPS * M2_DSTATE + (g + 1) * M2_DSTATE]
            ag = acum[:, gs]
            cbt = _mm_nt(cg, jnp.concatenate([bg] * 4, axis=0))
            arow = _mm3_left(ones, ag * eyet_ref[...])
            lm = jnp.exp(jnp.where(maskt_ref[...] > 0, ag - arow, -jnp.inf))
            xg = xdt[:, gs]
            xbd = jnp.concatenate([xg] * 4, axis=0) * bd_ref[...]
            y_diag = _mm(cbt * lm, xbd)
            alast = ag[CH - 1:CH, :]
            s = s_ref[g]
            y_off = _mm(cg, s) * jnp.exp(ag)
            s_ref[g] = s * jnp.exp(alast) + _mm_tn(bg, xg * jnp.exp(alast - ag))
            y = (y_diag + y_off + d_skip[:, gs] * xs[:, gs]) * _silu(z[:, gs])
            y = _rms_norm(y) * nw[:, gs]
            y_ref[r0:r0 + CH, gs] = y.astype(y_ref.dtype)
            if c == NCH - 1 and g == M2_GROUPS - 1:
                xp_ref[0:8, :] = xp_ref[TILE:TILE + 8, :]
            yield


def _hgrn_steps(p_ref, vec_ref, nw_ref, msk_ref, tri_ref, y_ref, s_ref):
    t = pl.program_id(1)

    @pl.when(t == 0)
    def _():
        s_ref[...] = jnp.zeros_like(s_ref)

    yield
    log_lb = vec_ref[0:1, :]
    log1m_lb = vec_ref[1:2, :]
    one_m_lb = vec_ref[2:3, :]
    nw = nw_ref[...]
    causal = msk_ref[0]
    tri = tri_ref[...]
    hsl = [slice(h * HG_DK, (h + 1) * HG_DK) for h in range(HG_HEADS)]
    nsub = CH // SUB
    prep = []
    for c in range(NCH):
        r0 = c * CH
        q = _silu(p_ref[r0:r0 + CH, 0:512])
        zf = p_ref[r0:r0 + CH, 512:1024]
        e = jnp.exp(-jnp.abs(zf))
        lsig = jnp.minimum(zf, 0.0) - jnp.log1p(e)
        bb = log1m_lb + lsig
        logf = jnp.maximum(log_lb, bb) + jnp.log1p(jnp.exp(-jnp.abs(log_lb - bb)))
        sig_neg = jnp.where(zf >= 0, e, 1.0) / (1.0 + e)
        k = one_m_lb * sig_neg
        gcum = _mm3_left(tri, logf)
        glast = gcum[CH - 1:CH, :]
        subs = []
        for i in range(nsub):
            rs = slice(i * SUB, (i + 1) * SUB)
            n = (i + 1) * SUB
            gref = gcum[i * SUB - 1:i * SUB, :] if i > 0 else jnp.zeros((1, BRANCH_W), F32)
            qt = q[rs] * jnp.exp(gcum[rs] - gref)
            kt = k[:n] * jnp.exp(jnp.minimum(gref - gcum[:n], HG_EXP_CLAMP))
            subs.append((qt, kt, qt * jnp.exp(gref)))
        prep.append(dict(kd=k * jnp.exp(glast - gcum), eg=jnp.exp(glast), subs=subs))
        yield
    incs = [[_mm_tn(p_ref[c * CH:(c + 1) * CH, 1024 + h * HG_DK:1024 + (h + 1) * HG_DK], prep[c]["kd"][:, hsl[h]])
             for h in range(HG_HEADS)] for c in range(NCH)]
    states = [[s_ref[h] for h in range(HG_HEADS)]]
    for c in range(NCH):
        states.append([states[c][h] * prep[c]["eg"][:, hsl[h]] + incs[c][h] for h in range(HG_HEADS)])
    for h in range(HG_HEADS):
        s_ref[h] = states[NCH][h]
    yield
    ams = [[[_mm_nt(prep[c]["subs"][i][0][:, hsl[h]], prep[c]["subs"][i][1][:, hsl[h]])
             * causal[i * SUB:(i + 1) * SUB, :(i + 1) * SUB]
             for h in range(HG_HEADS)] for i in range(nsub)] for c in range(NCH)]
    yield
    for c in range(NCH):
        r0 = c * CH
        for h in range(HG_HEADS):
            iv = p_ref[r0:r0 + CH, 1024 + h * HG_DK:1024 + (h + 1) * HG_DK]
            o = jnp.concatenate(
                [_mm(ams[c][i][h], iv[:(i + 1) * SUB]) + _mm_nt(prep[c]["subs"][i][2][:, hsl[h]], states[c][h])
                 for i in range(nsub)], axis=0)
            z = p_ref[r0:r0 + CH, 1536 + h * HG_DK:1536 + (h + 1) * HG_DK]
            y_ref[r0:r0 + CH, hsl[h]] = (_rms_norm(o) * nw * _silu(z)).astype(y_ref.dtype)
        yield


def _mix_merge_kernel(h_ref, qkv_ref, gz_ref, ba_ref, xbc_ref, zdt_ref, hg_ref, yd_ref,
                      gcw_ref, glane_ref, gnw_ref, scw_ref, scb_ref, svec_ref, eyet_ref, maskt_ref, bd_ref, ones_ref,
                      hvec_ref, hnw_ref, msk_ref, tri_ref, wg_ref, wb_ref, wo_ref, g_ref, b_ref,
                      o_ref, gxp_ref, gs_ref, sxp_ref, ss_ref, hs_ref, ya_ref, yb_ref, yc_ref, gate_ref, *, pad):
    t = pl.program_id(1)
    h = h_ref[...]
    hb = h.astype(BF)

    def gate_piece(j):
        cs = slice(j * GATE_CHUNK, (j + 1) * GATE_CHUNK)
        gate_ref[:, cs] = jnp.tanh(jnp.dot(hb, wg_ref[:, cs], preferred_element_type=F32)) + 1.0

    def branch_piece(br, y_ref):
        cs = slice(br * D_MODEL, (br + 1) * D_MODEL)
        gate_ref[:, cs] = gate_ref[:, cs] * jnp.dot(y_ref[...], wb_ref[br], preferred_element_type=F32)

    gdn = _gdn_steps(qkv_ref, gz_ref, ba_ref, gcw_ref, glane_ref, gnw_ref, msk_ref, tri_ref, ya_ref, gxp_ref, gs_ref)
    ssd = _ssd_steps(xbc_ref, zdt_ref, scw_ref, scb_ref, svec_ref, eyet_ref, maskt_ref, bd_ref, tri_ref, ones_ref,
                     yb_ref, sxp_ref, ss_ref, pad=pad)
    hgrn = _hgrn_steps(hg_ref, hvec_ref, hnw_ref, msk_ref, tri_ref, yc_ref, hs_ref)
    gates = iter([functools.partial(gate_piece, j) for j in range(N_BRANCH * D_MODEL // GATE_CHUNK)])
    emit = {"g": lambda: next(gdn), "s": lambda: next(ssd), "h": lambda: next(hgrn), "f": lambda: next(gates)(),
            "A": lambda: branch_piece(0, ya_ref), "B": lambda: branch_piece(1, yb_ref),
            "C": lambda: branch_piece(2, yc_ref), "D": lambda: branch_piece(3, yd_ref)}
    for tok in MIX_PLAN:
        emit[tok]()
    mixed = ((gate_ref[:, 0:D_MODEL] + gate_ref[:, D_MODEL:2 * D_MODEL])
             + (gate_ref[:, 2 * D_MODEL:3 * D_MODEL] + gate_ref[:, 3 * D_MODEL:4 * D_MODEL]))
    out = jnp.dot(mixed.astype(BF), wo_ref[...], preferred_element_type=F32)
    r = ALPHA * h + out
    mu = jnp.mean(r, axis=-1, keepdims=True)
    rc = r - mu
    var = jnp.mean(rc * rc, axis=-1, keepdims=True)
    y = rc * lax.rsqrt(var + LN_EPS) * g_ref[...] + b_ref[...]
    row = t * TILE + lax.broadcasted_iota(jnp.int32, (TILE, 1), 0)
    o_ref[...] = jnp.where(row >= pad, y, 0.0)


def _mix_merge_call(h, p, yd, gdn_prm, ssd_prm, hg_prm, msk, tri, wg, wb, wo, g, b, pad, drop_first_tile):
    bsz, lp, d = h.shape
    tok = lambda w, j=0: pl.BlockSpec((None, TILE, w), lambda b_, t: (b_, t, j))
    return pl.pallas_call(
        functools.partial(_mix_merge_kernel, pad=pad),
        grid=(bsz, lp // TILE),
        in_specs=[tok(d), tok(1536, OFF_QKV // 1536), tok(512, OFF_GZ // 512), tok(128, OFF_BA // 128),
                  tok(1024, OFF_XBC // 1024), tok(1024, OFF_MZ // 1024), tok(2048, OFF_HG // 2048), tok(BRANCH_W),
                  _full((CONV_K, 1536)), _full((8, 128)), _full((1, GDN_DK)),
                  _full((CONV_K, 1024)), _full((1, 1024)), _full((8, 512)),
                  _full((CH, 256)), _full((CH, 256)), _full((256, 256)), _full((CH, CH)),
                  _full((8, 512)), _full((1, HG_DK)), _full((9, CH, CH)), _full((CH, CH)),
                  _full((d, N_BRANCH * d)), _full((N_BRANCH, BRANCH_W, d)), _full((d, d)),
                  _full((1, d)), _full((1, d))],
        out_specs=(pl.BlockSpec((None, TILE, d), lambda b_, t: (b_, jnp.maximum(t - 1, 0), 0))
                   if drop_first_tile else tok(d)),
        out_shape=jax.ShapeDtypeStruct((bsz, lp - TILE if drop_first_tile else lp, d), F32),
        scratch_shapes=[pltpu.VMEM((TILE + 8, 1536), F32), pltpu.VMEM((GDN_HEADS, GDN_DK, GDN_DK), F32),
                        pltpu.VMEM((TILE + 8, 1024), F32), pltpu.VMEM((M2_GROUPS, M2_DSTATE, 256), F32),
                        pltpu.VMEM((HG_HEADS, HG_DK, HG_DK), F32),
                        pltpu.VMEM((TILE, BRANCH_W), BF), pltpu.VMEM((TILE, BRANCH_W), BF),
                        pltpu.VMEM((TILE, BRANCH_W), BF), pltpu.VMEM((TILE, N_BRANCH * D_MODEL), F32)],
        compiler_params=_cparams(),
        name="mix_merge",
    )(h, p, p, p, p, p, p, yd, *gdn_prm, *ssd_prm, *hg_prm, msk, tri, wg, wb, wo, g.reshape(1, d), b.reshape(1, d))


def _split_w_in(w):
    sizes = (1536, 512, 4, 4, 1024, 512, 8, 512, 512, 512, 512, 512, 512, 4096)
    offs = [0]
    for s in sizes:
        offs.append(offs[-1] + s)
    return [w[:, offs[i]:offs[i + 1]] for i in range(len(sizes))]


def _prep_w1(w):
    (qkv, gz, gb, ga, xbc, mz, mdt, hq, hf, hi, hz, su, sz, gates) = _split_w_in(w)
    d = w.shape[0]
    ba = jnp.concatenate([gb, ga, jnp.zeros((d, 120), w.dtype)], axis=1)
    mdt_e = jnp.repeat(mdt, M2_HEADDIM, axis=1)
    w1 = jnp.concatenate([qkv, gz, xbc, mz, mdt_e, hq, hf, hi, hz, ba, su, sz], axis=1)
    return w1.astype(BF), (0.5 * gates).astype(BF)


def _lane_row(vals, off, width=128):
    return jnp.zeros((width,), F32).at[off:off + vals.shape[0]].set(vals.astype(F32))


def _prep_s5(a_re, a_im, b_re, b_im, c_re, c_im, log_dt):
    f = lambda v: v.astype(F32)
    a_re, a_im, b_re, b_im, c_re, c_im = map(f, (a_re, a_im, b_re, b_im, c_re, c_im))
    dt = jnp.exp(f(log_dt))[:, None]
    mag = jnp.exp(a_re * dt)
    lam_re, lam_im = mag * jnp.cos(a_im * dt), mag * jnp.sin(a_im * dt)
    den = jnp.square(a_re) + jnp.square(a_im)
    nr, ni = lam_re - 1.0, lam_im
    z_re, z_im = (nr * a_re + ni * a_im) / den, (ni * a_re - nr * a_im) / den
    bb_re = z_re[..., None] * b_re - z_im[..., None] * b_im
    bb_im = z_re[..., None] * b_im + z_im[..., None] * b_re
    eye8 = jnp.eye(8, dtype=F32)

    def in_blocks(bb):
        blk = jnp.transpose(bb, (0, 2, 1)).reshape(4, 8, S5_GROUP, S5_P)
        return jnp.einsum('jicp,ik->jickp', blk, eye8).reshape(4, 128, 512)

    def out_blocks(cc):
        blk = jnp.transpose(cc, (0, 2, 1)).reshape(4, 8, S5_P, S5_GROUP)
        return jnp.einsum('jipc,ik->jipkc', blk, eye8).reshape(4, 512, 128)

    bw = jnp.concatenate([in_blocks(bb_re), in_blocks(bb_im)], axis=2).astype(BF)
    cre = out_blocks(c_re).astype(BF)
    cim = out_blocks(c_im).astype(BF)
    pw = jnp.arange(1, 9, dtype=F32)[:, None, None]
    pmag = jnp.exp(a_re[None] * dt[None] * pw)
    pang = a_im[None] * dt[None] * pw
    p_re = (pmag * jnp.cos(pang)).reshape(8, S5_STATE)
    p_im = (pmag * jnp.sin(pang)).reshape(8, S5_STATE)
    row = jnp.arange(8)[:, None]
    tabs = [p_re, p_im]
    for dd in (1, 2, 4):
        tabs.append(jnp.where(row >= dd, p_re[dd - 1][None, :], 0.0))
        tabs.append(jnp.where(row >= dd, p_im[dd - 1][None, :], 0.0))
    return bw, cre, cim, jnp.concatenate(tabs, axis=0)


@jax.jit
def kernel(x, meta_tokens, ln_in_g, ln_in_b, w_in, gdn_conv_w, gdn_A_log, gdn_dt_bias, gdn_norm_w, m2_conv_w, m2_conv_b, m2_dt_bias, m2_A_log, m2_D, m2_norm_w, hg_lb_logits, hg_norm_w, s5_A_re, s5_A_im, s5_B_re, s5_B_im, s5_C_re, s5_C_im, s5_D, s5_log_dt, s5_glu_w1, s5_glu_w2, w_branch, w_out, ln_g, ln_b):
    bsz, seq, d = x.shape
    ltot = N_META + seq
    lp = -(-ltot // TILE) * TILE
    pad = lp - ltot
    assert pad + N_META == TILE, "real tokens must start on the second tile"
    meta_tile = jnp.concatenate([jnp.zeros((pad, d), F32), meta_tokens.astype(F32)], axis=0)
    h = _ln_in_call(x.astype(F32), meta_tile, ln_in_g.astype(F32), ln_in_b.astype(F32), pad)

    msk = _masks64()
    tri = msk[0].astype(BF)
    ones = jnp.ones((CH, CH), BF)
    jj = jnp.arange(256)
    eyet = (jnp.arange(CH)[:, None] == (jj % CH)[None, :]).astype(F32)
    maskt = (jnp.arange(CH)[:, None] >= (jj % CH)[None, :]).astype(F32)
    bd = ((jj // CH)[:, None] == (jj // CH)[None, :]).astype(F32)

    cum = jnp.cumsum(jax.nn.softmax(hg_lb_logits.astype(F32), axis=0), axis=0)
    lower_bounds = cum - cum[0:1]

    for l in range(w_in.shape[0]):
        w1, wg = _prep_w1(w_in[l])
        bw, cre, cim, pw_tab = _prep_s5(s5_A_re[l], s5_A_im[l], s5_B_re[l], s5_B_im[l], s5_C_re[l], s5_C_im[l],
                                        s5_log_dt[l])
        p, yd = _proj_s5_call(h, w1, bw, cre, cim, pw_tab, s5_D[l].astype(F32).reshape(1, -1),
                              s5_glu_w1[l].astype(BF), s5_glu_w2[l].astype(BF))

        lane = jnp.stack([_lane_row(jnp.exp(gdn_A_log[l].astype(F32)), 4), _lane_row(gdn_dt_bias[l], 4)]
                         + [jnp.zeros((128,), F32)] * 6)
        gdn_prm = (gdn_conv_w[l].astype(F32), lane, gdn_norm_w[l].astype(F32).reshape(1, GDN_DK))

        rep = lambda v: jnp.repeat(v.astype(F32), M2_HEADDIM)
        vec = jnp.stack([rep(m2_dt_bias[l]), rep(-jnp.exp(m2_A_log[l].astype(F32))), rep(m2_D[l]),
                         m2_norm_w[l].astype(F32)] + [jnp.zeros((BRANCH_W,), F32)] * 4)
        ssd_prm = (m2_conv_w[l].astype(F32), m2_conv_b[l].astype(F32).reshape(1, -1), vec, eyet, maskt, bd, ones)

        lb = lower_bounds[l]
        hvec = jnp.stack([jnp.log(lb), jnp.log1p(-lb), 1.0 - lb] + [jnp.zeros((BRANCH_W,), F32)] * 5)
        hg_prm = (hvec, hg_norm_w[l].astype(F32).reshape(1, HG_DK))

        h = _mix_merge_call(h, p, yd, gdn_prm, ssd_prm, hg_prm, msk, tri, wg, (0.5 * w_branch[l]).astype(BF),
                            w_out[l].astype(BF), ln_g[l].astype(F32), ln_b[l].astype(F32), pad,
                            drop_first_tile=(l == w_in.shape[0] - 1))
    return h.astype(x.dtype)
```

```python
import functools
import math

import jax
import jax.numpy as jnp
from jax import lax
from jax.experimental import pallas as pl
from jax.experimental.pallas import tpu as pltpu

F32 = jnp.float32
BF = jnp.bfloat16

D_MODEL = 1024
DEPTH = 4
N_META = 16
CONV_K = 4
N_BRANCH = 4
BRANCH_W = 512
GDN_HEADS = 4
GDN_DK = 128
M2_HEADS = 8
M2_HEADDIM = 64
M2_GROUPS = 2
M2_DSTATE = 128
HG_HEADS = 4
HG_DK = 128
S5_GROUP = 16
S5_NG = 32
S5_P = 64
S5_STATE = S5_NG * S5_P
ALPHA = (2 * DEPTH) ** 0.25
LN_EPS = 1e-5
RMS_EPS = 1e-6

TILE = 256
CH = 64
NCH = TILE // CH
SUB = 16
HG_EXP_CLAMP = 80.0
VMEM_LIMIT = 56 * 1024 * 1024

OFF_QKV, OFF_GZ, OFF_XBC, OFF_MZ, OFF_MDT = 0, 1536, 2048, 3072, 3584
OFF_HG, OFF_BA, OFF_S5 = 4096, 6144, 6272
NP = 6272
N1 = 7296


def _sigmoid(x):
    return 0.5 * (jnp.tanh(0.5 * x) + 1.0)


def _silu(x):
    return x * _sigmoid(x)


def _softplus(x):
    return jnp.maximum(x, 0.0) + jnp.log1p(jnp.exp(-jnp.abs(x)))


def _mm(a, b):
    return jnp.dot(a.astype(BF), b.astype(BF), preferred_element_type=F32)


def _mm_nt(a, b):
    return lax.dot_general(a.astype(BF), b.astype(BF), (((1,), (1,)), ((), ())),
                           preferred_element_type=F32)


def _mm_tn(a, b):
    return lax.dot_general(a.astype(BF), b.astype(BF), (((0,), (0,)), ((), ())),
                           preferred_element_type=F32)


def _mm3_left(m_bf, x):
    hi = x.astype(BF)
    r1 = x - hi.astype(F32)
    mid = r1.astype(BF)
    lo = (r1 - mid.astype(F32)).astype(BF)
    d = lambda v: jnp.dot(m_bf, v, preferred_element_type=F32)
    return (d(lo) + d(mid)) + d(hi)


def _rms_norm(x):
    return x * lax.rsqrt(jnp.mean(x * x, axis=-1, keepdims=True) + RMS_EPS)


def _masks64():
    i = jnp.arange(CH)[:, None]
    j = jnp.arange(CH)[None, :]
    ms = [(i >= j), (i > j)]
    for s in (1, 2, 4, 8, 16, 32):
        ms.append((i // (2 * s) == j // (2 * s)) & (i % (2 * s) >= s) & (j % (2 * s) < s))
    ms.append(i == j)
    return jnp.stack(ms).astype(F32)


def _cparams():
    return pltpu.CompilerParams(dimension_semantics=("arbitrary", "arbitrary"),
                                vmem_limit_bytes=VMEM_LIMIT)


def _full(shape):
    n = len(shape)
    return pl.BlockSpec(shape, lambda b, t: (0,) * n, pipeline_mode=pl.Buffered(1))


def _ln_in_kernel(x_ref, m_ref, g_ref, b_ref, o_ref, *, pad):
    t = pl.program_id(1)
    x = jnp.where(t == 0, m_ref[...], x_ref[...])
    mu = jnp.mean(x, axis=-1, keepdims=True)
    xc = x - mu
    var = jnp.mean(xc * xc, axis=-1, keepdims=True)
    y = xc * lax.rsqrt(var + LN_EPS) * g_ref[...] + b_ref[...]
    row = t * TILE + lax.broadcasted_iota(jnp.int32, (TILE, 1), 0)
    o_ref[...] = jnp.where(row >= pad, y, 0.0)


def _ln_in_call(x, meta_tile, g, b, pad):
    bsz, seq, d = x.shape
    return pl.pallas_call(
        functools.partial(_ln_in_kernel, pad=pad),
        grid=(bsz, seq // TILE + 1),
        in_specs=[pl.BlockSpec((None, TILE, d), lambda b_, t: (b_, jnp.maximum(t - 1, 0), 0)),
                  _full((TILE, d)), _full((1, d)), _full((1, d))],
        out_specs=pl.BlockSpec((None, TILE, d), lambda b_, t: (b_, t, 0)),
        out_shape=jax.ShapeDtypeStruct((bsz, seq + TILE, d), F32),
        compiler_params=_cparams(),
        name="ln_in",
    )(x, meta_tile, g.reshape(1, d), b.reshape(1, d))


S5_CK = 8
S5_NCK = TILE // S5_CK
S5_PACK = 2 * S5_P
S5_LANES = S5_NG * S5_PACK
PROJ_CHUNK = 512
GATE_CHUNK = 256
MIX_PLAN = "gsh" + "gfgfgfgfh" * 4 + "gs" * 8 + "gg" + "hDhA" + "hhBhh" + "C"


def _transpose_pieces(arrs, lm_ref):
    for k, d in enumerate((4, 2, 1)):
        m = lm_ref[k:k + 1, :] > 0
        new = list(arrs)
        for s_ in range(8):
            if s_ & d == 0:
                lo, hi = arrs[s_], arrs[s_ + d]
                new[s_] = jnp.where(m, pltpu.roll(hi, 16 * d, 1), lo)
                new[s_ + d] = jnp.where(m, hi, pltpu.roll(lo, 128 - 16 * d, 1))
        arrs = new
    return arrs


def _s5_scan_block(b, cs, carry, tabs, st_ref):
    rows = slice(8 * b, 8 * b + 8)
    v = st_ref[rows, cs]
    for n, d in enumerate((1, 2, 4)):
        vsh = pltpu.roll(v, d, 0)
        v = v + (tabs[2 + 2 * n] * vsh + tabs[3 + 2 * n] * pltpu.roll(vsh, S5_P, 1))
    cb = jnp.broadcast_to(carry, (8, S5_PACK))
    v = v + (tabs[0] * cb + tabs[1] * pltpu.roll(cb, S5_P, 1))
    st_ref[rows, cs] = v
    return v[7:8, :]


def _proj_s5_kernel(h_ref, w_ref, m1_ref, q_ref, pw_ref, lm_ref, d_ref, g1_ref, g2_ref,
                    p_ref, yd_ref, u_ref, ug_ref, yq_ref, st_ref, y_ref, c_ref):
    t = pl.program_id(1)

    @pl.when(t == 0)
    def _():
        c_ref[...] = jnp.zeros_like(c_ref)

    hb = h_ref[...].astype(BF)
    u = jnp.dot(hb, w_ref[:, OFF_S5:OFF_S5 + BRANCH_W], preferred_element_type=F32)
    for j in range(4):
        u_ref[j] = u[:, j * 128:(j + 1) * 128]
    for j in range(4):
        arrs = [u_ref[j, pl.ds(s_, S5_NCK, stride=S5_CK), :] for s_ in range(S5_CK)]
        arrs = _transpose_pieces(arrs, lm_ref)
        for i in range(8):
            g = 8 * j + i
            ug_ref[:, g * 128:(g + 1) * 128] = arrs[i].astype(BF)
    row0 = lax.broadcasted_iota(jnp.int32, (S5_NCK, S5_PACK), 0) == 0
    last_inc = []
    for g in range(S5_NG):
        cs = slice(g * S5_PACK, (g + 1) * S5_PACK)
        tw = jnp.dot(ug_ref[:, cs], m1_ref[g], preferred_element_type=F32)
        yq_ref[:, cs] = tw[:, :S5_PACK]
        inc = tw[:, S5_PACK:]
        last_inc.append(inc[S5_NCK - 1:S5_NCK, :])
        st_ref[:, cs] = jnp.where(row0, c_ref[:, cs], pltpu.roll(inc, 1, 0))

    work = [(b, g) for g in range(S5_NG) for b in range(S5_NCK // 8)]
    starts = list(range(0, NP, PROJ_CHUNK))
    per = -(-len(work) // len(starts))
    carries = {}
    tabs = {}
    for n, j in enumerate(starts):
        w = min(PROJ_CHUNK, NP - j)
        p_ref[:, j:j + w] = jnp.dot(hb, w_ref[:, j:j + w], preferred_element_type=F32)
        for b, g in work[n * per:(n + 1) * per]:
            cs = slice(g * S5_PACK, (g + 1) * S5_PACK)
            if b == 0:
                tabs[g] = [pw_ref[8 * k:8 * k + 8, cs] for k in range(8)]
                carries[g] = jnp.zeros((1, S5_PACK), F32)
            carries[g] = _s5_scan_block(b, cs, carries[g], tabs[g], st_ref)
            if b == S5_NCK // 8 - 1:
                x = carries[g]
                c_ref[:, cs] = last_inc[g] + (tabs[g][0][0:1] * x + tabs[g][1][0:1] * pltpu.roll(
                    jnp.broadcast_to(x, (8, S5_PACK)), S5_P, 1)[0:1])

    for g in range(S5_NG):
        cs = slice(g * S5_PACK, (g + 1) * S5_PACK)
        yq_ref[:, cs] = yq_ref[:, cs] + _mm(st_ref[:, cs], q_ref[g])
    for j in range(4):
        arrs = _transpose_pieces([yq_ref[:, (8 * j + i) * 128:(8 * j + i + 1) * 128] for i in range(8)], lm_ref)
        for s_ in range(S5_CK):
            y_ref[j, pl.ds(s_, S5_NCK, stride=S5_CK), :] = arrs[s_]
    y = jnp.concatenate([y_ref[j] for j in range(4)], axis=1) + d_ref[...] * u
    y = 0.5 * y * (1.0 + jnp.tanh(math.sqrt(2.0 / math.pi) * (y + 0.044715 * (y * y * y))))
    yb = y.astype(BF)
    glu = jnp.dot(yb, g1_ref[...], preferred_element_type=F32) * _sigmoid(
        jnp.dot(yb, g2_ref[...], preferred_element_type=F32))
    z = jnp.dot(hb, w_ref[:, OFF_S5 + BRANCH_W:OFF_S5 + 2 * BRANCH_W], preferred_element_type=F32)
    yd_ref[...] = (glu * _silu(z)).astype(yd_ref.dtype)


def _proj_s5_call(h, w1, m1, q, pw, lm, dvec, g1, g2):
    bsz, lp, d = h.shape
    return pl.pallas_call(
        _proj_s5_kernel,
        grid=(bsz, lp // TILE),
        in_specs=[pl.BlockSpec((None, TILE, d), lambda b_, t: (b_, t, 0)), _full((d, N1)),
                  _full((S5_NG, 128, 256)), _full((S5_NG, 128, 128)), _full((64, S5_LANES)), _full((8, 128)),
                  _full((1, 512)), _full((512, 512)), _full((512, 512))],
        out_specs=[pl.BlockSpec((None, TILE, NP), lambda b_, t: (b_, t, 0)),
                   pl.BlockSpec((None, TILE, BRANCH_W), lambda b_, t: (b_, t, 0))],
        out_shape=[jax.ShapeDtypeStruct((bsz, lp, NP), F32), jax.ShapeDtypeStruct((bsz, lp, BRANCH_W), BF)],
        scratch_shapes=[pltpu.VMEM((4, TILE, 128), F32), pltpu.VMEM((S5_NCK, S5_LANES), BF),
                        pltpu.VMEM((S5_NCK, S5_LANES), F32), pltpu.VMEM((S5_NCK, S5_LANES), F32),
                        pltpu.VMEM((4, TILE, 128), F32), pltpu.VMEM((1, S5_LANES), F32)],
        compiler_params=_cparams(),
        name="proj_s5",
    )(h, w1, m1, q, pw, lm, dvec, g1, g2)


def _conv_chunk(xp_ref, cw, r0):
    x = xp_ref[r0 + 8:r0 + 8 + CH, :] * cw[CONV_K - 1:CONV_K, :]
    for j in range(CONV_K - 1):
        s = r0 + 8 - (CONV_K - 1) + j
        x = x + xp_ref[s:s + CH, :] * cw[j:j + 1, :]
    return x


def _gdn_steps(qkv_ref, z_ref, ba_ref, cw_ref, lane_ref, nw_ref, msk_ref, tri_ref, y_ref, xp_ref, s_ref):
    t = pl.program_id(1)

    @pl.when(t == 0)
    def _():
        xp_ref[0:8, :] = jnp.zeros((8, 3 * BRANCH_W), F32)
        s_ref[...] = jnp.zeros_like(s_ref)

    xp_ref[8:8 + TILE, :] = qkv_ref[...]
    yield
    cw = cw_ref[...]
    exp_a = lane_ref[0:1, :]
    dt_bias = lane_ref[1:2, :]
    nw = nw_ref[...]
    causal = msk_ref[0]
    strict = msk_ref[1]
    tri = tri_ref[...]
    items = []
    a_list = []
    for c in range(NCH):
        r0 = c * CH
        qkv = _silu(_conv_chunk(xp_ref, cw, r0))
        ba = ba_ref[r0:r0 + CH, :]
        beta_all = _sigmoid(ba)
        gl = -(exp_a * _softplus(ba + dt_bias))
        gc = _mm3_left(tri, gl)
        gc_t = gc.T
        for h in range(GDN_HEADS):
            sl = slice(h * GDN_DK, (h + 1) * GDN_DK)
            q = qkv[:, sl]
            k = qkv[:, BRANCH_W + h * GDN_DK:BRANCH_W + (h + 1) * GDN_DK]
            v = qkv[:, 2 * BRANCH_W + h * GDN_DK:2 * BRANCH_W + (h + 1) * GDN_DK]
            q = q * (lax.rsqrt(jnp.sum(q * q, axis=-1, keepdims=True) + RMS_EPS) * GDN_DK ** -0.5)
            k = k * lax.rsqrt(jnp.sum(k * k, axis=-1, keepdims=True) + RMS_EPS)
            beta = beta_all[:, h:h + 1]
            gcol = gc[:, 4 + h:5 + h]
            grow = gc_t[4 + h:5 + h, :]
            decay = jnp.exp(jnp.where(causal > 0, gcol - grow, -jnp.inf))
            kb = k * beta
            eg = jnp.exp(gcol)
            glast = gc[CH - 1:CH, 4 + h:5 + h]
            a_list.append(strict * (_mm_nt(kb, k) * decay))
            items.append(dict(
                rhs=jnp.concatenate([v * beta, kb * eg], axis=1),
                qk=causal * (_mm_nt(q, k) * decay),
                qd=q * eg,
                kd=k * jnp.exp(glast - gcol),
                gtot=jnp.exp(glast)))
            yield
    tinvs = [msk_ref[8] - a * msk_ref[2] for a in a_list]
    for lvl in range(1, 6):
        m = msk_ref[2 + lvl]
        t1 = [_mm(x, a * m) for x, a in zip(tinvs, a_list)]
        tinvs = [x - _mm(t_, x) for x, t_ in zip(tinvs, t1)]
        yield
    uws = [_mm(tinv, it["rhs"]) for tinv, it in zip(tinvs, items)]
    kd_uw = [_mm_tn(it["kd"], uw) for it, uw in zip(items, uws)]
    qk_uw = [_mm(it["qk"], uw) for it, uw in zip(items, uws)]
    yield
    states = [s_ref[h] for h in range(GDN_HEADS)]
    for c in range(NCH):
        r0 = c * CH
        for h in range(GDN_HEADS):
            sl = slice(h * GDN_DK, (h + 1) * GDN_DK)
            n = c * GDN_HEADS + h
            s = states[h]
            o = _mm(items[n]["qd"] - qk_uw[n][:, GDN_DK:], s) + qk_uw[n][:, :GDN_DK]
            states[h] = (s * items[n]["gtot"] - _mm(kd_uw[n][:, GDN_DK:], s)) + kd_uw[n][:, :GDN_DK]
            y = _rms_norm(o) * nw * _silu(z_ref[r0:r0 + CH, sl])
            y_ref[r0:r0 + CH, sl] = y.astype(y_ref.dtype)
        if c == NCH - 1:
            for h in range(GDN_HEADS):
                s_ref[h] = states[h]
            xp_ref[0:8, :] = xp_ref[TILE:TILE + 8, :]
        yield


def _ssd_steps(xbc_ref, zdt_ref, cw_ref, cb_ref, vec_ref, eyet_ref, maskt_ref, bd_ref, tri_ref, ones_ref,
               y_ref, xp_ref, s_ref, *, pad):
    t = pl.program_id(1)

    @pl.when(t == 0)
    def _():
        xp_ref[0:8, :] = jnp.zeros((8, 1024), F32)
        s_ref[...] = jnp.zeros_like(s_ref)

    xp_ref[8:8 + TILE, :] = xbc_ref[...]
    yield
    cw = cw_ref[...]
    cb = cb_ref[...]
    dt_bias = vec_ref[0:1, :]
    neg_a = vec_ref[1:2, :]
    d_skip = vec_ref[2:3, :]
    nw = vec_ref[3:4, :]
    tri = tri_ref[...]
    ones = ones_ref[...]
    gw = M2_HEADDIM * (M2_HEADS // M2_GROUPS)
    for c in range(NCH):
        r0 = c * CH
        xbc = _silu(_conv_chunk(xp_ref, cw, r0) + cb)
        xs = xbc[:, :BRANCH_W]
        row = t * TILE + r0 + lax.broadcasted_iota(jnp.int32, (CH, 1), 0)
        valid = (row >= pad).astype(F32)
        dt = _softplus(zdt_ref[r0:r0 + CH, BRANCH_W:] + dt_bias) * valid
        a = dt * neg_a
        acum = _mm3_left(tri, a)
        xdt = xs * dt
        z = zdt_ref[r0:r0 + CH, :BRANCH_W]
        for g in range(M2_GROUPS):
            gs = slice(g * gw, (g + 1) * gw)
            bg = xbc[:, BRANCH_W + g * M2_DSTATE:BRANCH_W + (g + 1) * M2_DSTATE]
            cg = xbc[:, BRANCH_W + M2_GROUPS * M2_DSTATE + g * M2_DSTATE:
                     BRANCH_W + M2_GROUPS * M2_DSTATE + (g + 1) * M2_DSTATE]
            ag = acum[:, gs]
            cbt = _mm_nt(cg, jnp.concatenate([bg] * 4, axis=0))
            arow = _mm3_left(ones, ag * eyet_ref[...])
            lm = jnp.exp(jnp.where(maskt_ref[...] > 0, ag - arow, -jnp.inf))
            xg = xdt[:, gs]
            xbd = jnp.concatenate([xg] * 4, axis=0) * bd_ref[...]
            y_diag = _mm(cbt * lm, xbd)
            alast = ag[CH - 1:CH, :]
            s = s_ref[g]
            y_off = _mm(cg, s) * jnp.exp(ag)
            s_ref[g] = s * jnp.exp(alast) + _mm_tn(bg, xg * jnp.exp(alast - ag))
            y = (y_diag + y_off + d_skip[:, gs] * xs[:, gs]) * _silu(z[:, gs])
            y = _rms_norm(y) * nw[:, gs]
            y_ref[r0:r0 + CH, gs] = y.astype(y_ref.dtype)
            if c == NCH - 1 and g == M2_GROUPS - 1:
                xp_ref[0:8, :] = xp_ref[TILE:TILE + 8, :]
            yield


def _hgrn_steps(p_ref, vec_ref, nw_ref, msk_ref, tri_ref, y_ref, s_ref):
    t = pl.program_id(1)

    @pl.when(t == 0)
    def _():
        s_ref[...] = jnp.zeros_like(s_ref)

    yield
    log_lb = vec_ref[0:1, :]
    log1m_lb = vec_ref[1:2, :]
    one_m_lb = vec_ref[2:3, :]
    nw = nw_ref[...]
    causal = msk_ref[0]
    tri = tri_ref[...]
    hsl = [slice(h * HG_DK, (h + 1) * HG_DK) for h in range(HG_HEADS)]
    nsub = CH // SUB
    prep = []
    for c in range(NCH):
        r0 = c * CH
        q = _silu(p_ref[r0:r0 + CH, 0:512])
        zf = p_ref[r0:r0 + CH, 512:1024]
        e = jnp.exp(-jnp.abs(zf))
        lsig = jnp.minimum(zf, 0.0) - jnp.log1p(e)
        bb = log1m_lb + lsig
        logf = jnp.maximum(log_lb, bb) + jnp.log1p(jnp.exp(-jnp.abs(log_lb - bb)))
        sig_neg = jnp.where(zf >= 0, e, 1.0) / (1.0 + e)
        k = one_m_lb * sig_neg
        gcum = _mm3_left(tri, logf)
        glast = gcum[CH - 1:CH, :]
        subs = []
        for i in range(nsub):
            rs = slice(i * SUB, (i + 1) * SUB)
            n = (i + 1) * SUB
            gref = gcum[i * SUB - 1:i * SUB, :] if i > 0 else jnp.zeros((1, BRANCH_W), F32)
            qt = q[rs] * jnp.exp(gcum[rs] - gref)
            kt = k[:n] * jnp.exp(jnp.minimum(gref - gcum[:n], HG_EXP_CLAMP))
            subs.append((qt, kt, qt * jnp.exp(gref)))
        prep.append(dict(kd=k * jnp.exp(glast - gcum), eg=jnp.exp(glast), subs=subs))
        yield
    incs = [[_mm_tn(p_ref[c * CH:(c + 1) * CH, 1024 + h * HG_DK:1024 + (h + 1) * HG_DK], prep[c]["kd"][:, hsl[h]])
             for h in range(HG_HEADS)] for c in range(NCH)]
    states = [[s_ref[h] for h in range(HG_HEADS)]]
    for c in range(NCH):
        states.append([states[c][h] * prep[c]["eg"][:, hsl[h]] + incs[c][h] for h in range(HG_HEADS)])
    for h in range(HG_HEADS):
        s_ref[h] = states[NCH][h]
    yield
    ams = [[[_mm_nt(prep[c]["subs"][i][0][:, hsl[h]], prep[c]["subs"][i][1][:, hsl[h]])
             * causal[i * SUB:(i + 1) * SUB, :(i + 1) * SUB]
             for h in range(HG_HEADS)] for i in range(nsub)] for c in range(NCH)]
    yield
    for c in range(NCH):
        r0 = c * CH
        for h in range(HG_HEADS):
            iv = p_ref[r0:r0 + CH, 1024 + h * HG_DK:1024 + (h + 1) * HG_DK]
            o = jnp.concatenate(
                [_mm(ams[c][i][h], iv[:(i + 1) * SUB]) + _mm_nt(prep[c]["subs"][i][2][:, hsl[h]], states[c][h])
                 for i in range(nsub)], axis=0)
            z = p_ref[r0:r0 + CH, 1536 + h * HG_DK:1536 + (h + 1) * HG_DK]
            y_ref[r0:r0 + CH, hsl[h]] = (_rms_norm(o) * nw * _silu(z)).astype(y_ref.dtype)
        yield


def _mix_merge_kernel(h_ref, qkv_ref, gz_ref, ba_ref, xbc_ref, zdt_ref, hg_ref, yd_ref,
                      gcw_ref, glane_ref, gnw_ref, scw_ref, scb_ref, svec_ref, eyet_ref, maskt_ref, bd_ref, ones_ref,
                      hvec_ref, hnw_ref, msk_ref, tri_ref, wg_ref, wb_ref, wo_ref, g_ref, b_ref,
                      o_ref, gxp_ref, gs_ref, sxp_ref, ss_ref, hs_ref, ya_ref, yb_ref, yc_ref, gate_ref, *, pad):
    t = pl.program_id(1)
    h = h_ref[...]
    hb = h.astype(BF)

    def gate_piece(j):
        cs = slice(j * GATE_CHUNK, (j + 1) * GATE_CHUNK)
        gate_ref[:, cs] = jnp.tanh(jnp.dot(hb, wg_ref[:, cs], preferred_element_type=F32)) + 1.0

    def branch_piece(br, y_ref):
        cs = slice(br * D_MODEL, (br + 1) * D_MODEL)
        gate_ref[:, cs] = gate_ref[:, cs] * jnp.dot(y_ref[...], wb_ref[br], preferred_element_type=F32)

    gdn = _gdn_steps(qkv_ref, gz_ref, ba_ref, gcw_ref, glane_ref, gnw_ref, msk_ref, tri_ref, ya_ref, gxp_ref, gs_ref)
    ssd = _ssd_steps(xbc_ref, zdt_ref, scw_ref, scb_ref, svec_ref, eyet_ref, maskt_ref, bd_ref, tri_ref, ones_ref,
                     yb_ref, sxp_ref, ss_ref, pad=pad)
    hgrn = _hgrn_steps(hg_ref, hvec_ref, hnw_ref, msk_ref, tri_ref, yc_ref, hs_ref)
    gates = iter([functools.partial(gate_piece, j) for j in range(N_BRANCH * D_MODEL // GATE_CHUNK)])
    emit = {"g": lambda: next(gdn), "s": lambda: next(ssd), "h": lambda: next(hgrn), "f": lambda: next(gates)(),
            "A": lambda: branch_piece(0, ya_ref), "B": lambda: branch_piece(1, yb_ref),
            "C": lambda: branch_piece(2, yc_ref), "D": lambda: branch_piece(3, yd_ref)}
    for tok in MIX_PLAN:
        emit[tok]()
    mixed = ((gate_ref[:, 0:D_MODEL] + gate_ref[:, D_MODEL:2 * D_MODEL])
             + (gate_ref[:, 2 * D_MODEL:3 * D_MODEL] + gate_ref[:, 3 * D_MODEL:4 * D_MODEL]))
    out = jnp.dot(mixed.astype(BF), wo_ref[...], preferred_element_type=F32)
    r = ALPHA * h + out
    mu = jnp.mean(r, axis=-1, keepdims=True)
    rc = r - mu
    var = jnp.mean(rc * rc, axis=-1, keepdims=True)
    y = rc * lax.rsqrt(var + LN_EPS) * g_ref[...] + b_ref[...]
    row = t * TILE + lax.broadcasted_iota(jnp.int32, (TILE, 1), 0)
    o_ref[...] = jnp.where(row >= pad, y, 0.0)


def _mix_merge_call(h, p, yd, gdn_prm, ssd_prm, hg_prm, msk, tri, wg, wb, wo, g, b, pad, drop_first_tile):
    bsz, lp, d = h.shape
    tok = lambda w, j=0: pl.BlockSpec((None, TILE, w), lambda b_, t: (b_, t, j))
    return pl.pallas_call(
        functools.partial(_mix_merge_kernel, pad=pad),
        grid=(bsz, lp // TILE),
        in_specs=[tok(d), tok(1536, OFF_QKV // 1536), tok(512, OFF_GZ // 512), tok(128, OFF_BA // 128),
                  tok(1024, OFF_XBC // 1024), tok(1024, OFF_MZ // 1024), tok(2048, OFF_HG // 2048), tok(BRANCH_W),
                  _full((CONV_K, 1536)), _full((8, 128)), _full((1, GDN_DK)),
                  _full((CONV_K, 1024)), _full((1, 1024)), _full((8, 512)),
                  _full((CH, 256)), _full((CH, 256)), _full((256, 256)), _full((CH, CH)),
                  _full((8, 512)), _full((1, HG_DK)), _full((9, CH, CH)), _full((CH, CH)),
                  _full((d, N_BRANCH * d)), _full((N_BRANCH, BRANCH_W, d)), _full((d, d)),
                  _full((1, d)), _full((1, d))],
        out_specs=(pl.BlockSpec((None, TILE, d), lambda b_, t: (b_, jnp.maximum(t - 1, 0), 0))
                   if drop_first_tile else tok(d)),
        out_shape=jax.ShapeDtypeStruct((bsz, lp - TILE if drop_first_tile else lp, d), F32),
        scratch_shapes=[pltpu.VMEM((TILE + 8, 1536), F32), pltpu.VMEM((GDN_HEADS, GDN_DK, GDN_DK), F32),
                        pltpu.VMEM((TILE + 8, 1024), F32), pltpu.VMEM((M2_GROUPS, M2_DSTATE, 256), F32),
                        pltpu.VMEM((HG_HEADS, HG_DK, HG_DK), F32),
                        pltpu.VMEM((TILE, BRANCH_W), BF), pltpu.VMEM((TILE, BRANCH_W), BF),
                        pltpu.VMEM((TILE, BRANCH_W), BF), pltpu.VMEM((TILE, N_BRANCH * D_MODEL), F32)],
        compiler_params=_cparams(),
        name="mix_merge",
    )(h, p, p, p, p, p, p, yd, *gdn_prm, *ssd_prm, *hg_prm, msk, tri, wg, wb, wo, g.reshape(1, d), b.reshape(1, d))


def _split_w_in(w):
    sizes = (1536, 512, 4, 4, 1024, 512, 8, 512, 512, 512, 512, 512, 512, 4096)
    offs = [0]
    for s in sizes:
        offs.append(offs[-1] + s)
    return [w[:, offs[i]:offs[i + 1]] for i in range(len(sizes))]


def _prep_w1(w):
    (qkv, gz, gb, ga, xbc, mz, mdt, hq, hf, hi, hz, su, sz, gates) = _split_w_in(w)
    d = w.shape[0]
    ba = jnp.concatenate([gb, ga, jnp.zeros((d, 120), w.dtype)], axis=1)
    mdt_e = jnp.repeat(mdt, M2_HEADDIM, axis=1)
    w1 = jnp.concatenate([qkv, gz, xbc, mz, mdt_e, hq, hf, hi, hz, ba, su, sz], axis=1)
    return w1.astype(BF), (0.5 * gates).astype(BF)


def _lane_row(vals, off, width=128):
    return jnp.zeros((width,), F32).at[off:off + vals.shape[0]].set(vals.astype(F32))


def _prep_s5(a_re, a_im, b_re, b_im, c_re, c_im, log_dt):
    f = lambda v: v.astype(F32)
    a_re, a_im, b_re, b_im, c_re, c_im = map(f, (a_re, a_im, b_re, b_im, c_re, c_im))
    dt = jnp.exp(f(log_dt))[:, None]

    def lam_pow(k):
        mag = jnp.exp(a_re * dt * k)
        return mag * jnp.cos(a_im * dt * k), mag * jnp.sin(a_im * dt * k)

    lam_re, lam_im = lam_pow(1.0)
    den = jnp.square(a_re) + jnp.square(a_im)
    nr, ni = lam_re - 1.0, lam_im
    z_re, z_im = (nr * a_re + ni * a_im) / den, (ni * a_re - nr * a_im) / den
    bb_re = z_re[..., None] * b_re - z_im[..., None] * b_im
    bb_im = z_re[..., None] * b_im + z_im[..., None] * b_re
    ks = jnp.arange(S5_CK, dtype=F32)[:, None, None]
    pr, pi = lam_pow(ks)
    p1r, p1i = lam_pow(ks + 1.0)
    mr, mi = lam_pow(S5_CK * (ks + 1.0))

    def c_times(qr, qi):
        return (c_re[None] * qr[:, :, None, :] - c_im[None] * qi[:, :, None, :],
                c_re[None] * qi[:, :, None, :] + c_im[None] * qr[:, :, None, :])

    cl_re, cl_im = c_times(pr, pi)
    kd = (jnp.einsum('dgcp,gpe->dgce', cl_re, bb_re) - jnp.einsum('dgcp,gpe->dgce', cl_im, bb_im))
    s_i = jnp.arange(S5_CK)[:, None]
    t_i = jnp.arange(S5_CK)[None, :]
    kt = jnp.where((t_i >= s_i)[:, :, None, None, None], kd[jnp.clip(t_i - s_i, 0, S5_CK - 1)], 0.0)
    toep = jnp.transpose(kt, (2, 0, 4, 1, 3)).reshape(S5_NG, 128, 128)
    rr, ri = pr[::-1], pi[::-1]
    w_re = rr[..., None] * bb_re[None] - ri[..., None] * bb_im[None]
    w_im = rr[..., None] * bb_im[None] + ri[..., None] * bb_re[None]
    to_rows = lambda w: jnp.transpose(w, (1, 0, 3, 2)).reshape(S5_NG, 128, S5_P)
    m1 = jnp.concatenate([toep, to_rows(w_re), to_rows(w_im)], axis=2).astype(BF)
    q_re, q_im = c_times(p1r, p1i)
    to_cols = lambda v: jnp.transpose(v, (1, 3, 0, 2)).reshape(S5_NG, S5_P, 128)
    q = jnp.concatenate([to_cols(q_re), -to_cols(q_im)], axis=1).astype(BF)
    pack1 = lambda v: jnp.concatenate([v, v], axis=-1).reshape(v.shape[0], S5_LANES)
    pack2 = lambda v: jnp.concatenate([-v, v], axis=-1).reshape(v.shape[0], S5_LANES)
    row = jnp.arange(8)[:, None]
    tabs = [pack1(mr), pack2(mi)]
    for dd in (1, 2, 4):
        tabs.append(jnp.where(row >= dd, pack1(mr[dd - 1:dd]), 0.0))
        tabs.append(jnp.where(row >= dd, pack2(mi[dd - 1:dd]), 0.0))
    lane_piece = jnp.arange(128) // 16
    lm = jnp.stack([(lane_piece & dd) != 0 for dd in (4, 2, 1)] + [jnp.zeros((128,), bool)] * 5).astype(F32)
    return m1, q, jnp.concatenate(tabs, axis=0), lm


@jax.jit
def kernel(x, meta_tokens, ln_in_g, ln_in_b, w_in, gdn_conv_w, gdn_A_log, gdn_dt_bias, gdn_norm_w, m2_conv_w, m2_conv_b, m2_dt_bias, m2_A_log, m2_D, m2_norm_w, hg_lb_logits, hg_norm_w, s5_A_re, s5_A_im, s5_B_re, s5_B_im, s5_C_re, s5_C_im, s5_D, s5_log_dt, s5_glu_w1, s5_glu_w2, w_branch, w_out, ln_g, ln_b):
    bsz, seq, d = x.shape
    ltot = N_META + seq
    lp = -(-ltot // TILE) * TILE
    pad = lp - ltot
    assert pad + N_META == TILE, "real tokens must start on the second tile"
    meta_tile = jnp.concatenate([jnp.zeros((pad, d), F32), meta_tokens.astype(F32)], axis=0)
    h = _ln_in_call(x.astype(F32), meta_tile, ln_in_g.astype(F32), ln_in_b.astype(F32), pad)

    msk = _masks64()
    tri = msk[0].astype(BF)
    ones = jnp.ones((CH, CH), BF)
    jj = jnp.arange(256)
    eyet = (jnp.arange(CH)[:, None] == (jj % CH)[None, :]).astype(F32)
    maskt = (jnp.arange(CH)[:, None] >= (jj % CH)[None, :]).astype(F32)
    bd = ((jj // CH)[:, None] == (jj // CH)[None, :]).astype(F32)

    cum = jnp.cumsum(jax.nn.softmax(hg_lb_logits.astype(F32), axis=0), axis=0)
    lower_bounds = cum - cum[0:1]

    for l in range(w_in.shape[0]):
        w1, wg = _prep_w1(w_in[l])
        m1, q, pw_tab, lm = _prep_s5(s5_A_re[l], s5_A_im[l], s5_B_re[l], s5_B_im[l], s5_C_re[l], s5_C_im[l],
                                     s5_log_dt[l])
        p, yd = _proj_s5_call(h, w1, m1, q, pw_tab, lm, s5_D[l].astype(F32).reshape(1, -1),
                              s5_glu_w1[l].astype(BF), s5_glu_w2[l].astype(BF))

        lane = jnp.stack([_lane_row(jnp.exp(gdn_A_log[l].astype(F32)), 4), _lane_row(gdn_dt_bias[l], 4)]
                         + [jnp.zeros((128,), F32)] * 6)
        gdn_prm = (gdn_conv_w[l].astype(F32), lane, gdn_norm_w[l].astype(F32).reshape(1, GDN_DK))

        rep = lambda v: jnp.repeat(v.astype(F32), M2_HEADDIM)
        vec = jnp.stack([rep(m2_dt_bias[l]), rep(-jnp.exp(m2_A_log[l].astype(F32))), rep(m2_D[l]),
                         m2_norm_w[l].astype(F32)] + [jnp.zeros((BRANCH_W,), F32)] * 4)
        ssd_prm = (m2_conv_w[l].astype(F32), m2_conv_b[l].astype(F32).reshape(1, -1), vec, eyet, maskt, bd, ones)

        lb = lower_bounds[l]
        hvec = jnp.stack([jnp.log(lb), jnp.log1p(-lb), 1.0 - lb] + [jnp.zeros((BRANCH_W,), F32)] * 5)
        hg_prm = (hvec, hg_norm_w[l].astype(F32).reshape(1, HG_DK))

        h = _mix_merge_call(h, p, yd, gdn_prm, ssd_prm, hg_prm, msk, tri, wg, (0.5 * w_branch[l]).astype(BF),
                            w_out[l].astype(BF), ln_g[l].astype(F32), ln_b[l].astype(F32), pad,
                            drop_first_tile=(l == w_in.shape[0] - 1))
    return h.astype(x.dtype)
```

```python
import functools
import itertools
import math

import jax
import jax.numpy as jnp
from jax import lax
from jax.experimental import pallas as pl
from jax.experimental.pallas import tpu as pltpu

F32 = jnp.float32
BF = jnp.bfloat16

D_MODEL = 1024
DEPTH = 4
N_META = 16
CONV_K = 4
N_BRANCH = 4
BRANCH_W = 512
GDN_HEADS = 4
GDN_DK = 128
M2_HEADS = 8
M2_HEADDIM = 64
M2_GROUPS = 2
M2_DSTATE = 128
HG_HEADS = 4
HG_DK = 128
S5_GROUP = 16
S5_NG = 32
S5_P = 64
S5_STATE = S5_NG * S5_P
ALPHA = (2 * DEPTH) ** 0.25
LN_EPS = 1e-5
RMS_EPS = 1e-6

TILE = 256
CH = 64
NCH = TILE // CH
SUB = 16
HG_EXP_CLAMP = 80.0
VMEM_LIMIT = 56 * 1024 * 1024

OFF_QKV, OFF_GZ, OFF_XBC, OFF_MZ, OFF_MDT = 0, 1536, 2048, 3072, 3584
OFF_HG, OFF_BA, OFF_S5 = 4096, 6144, 6272
NP = 6272
N1 = 7296


def _sigmoid(x):
    return 0.5 * (jnp.tanh(0.5 * x) + 1.0)


def _silu(x):
    return x * _sigmoid(x)


def _softplus(x):
    return jnp.maximum(x, 0.0) + jnp.log1p(jnp.exp(-jnp.abs(x)))


def _mm(a, b):
    return jnp.dot(a.astype(BF), b.astype(BF), preferred_element_type=F32)


def _mm_nt(a, b):
    return lax.dot_general(a.astype(BF), b.astype(BF), (((1,), (1,)), ((), ())),
                           preferred_element_type=F32)


def _mm_tn(a, b):
    return lax.dot_general(a.astype(BF), b.astype(BF), (((0,), (0,)), ((), ())),
                           preferred_element_type=F32)


def _mm3_left(m_bf, x):
    hi = x.astype(BF)
    r1 = x - hi.astype(F32)
    mid = r1.astype(BF)
    lo = (r1 - mid.astype(F32)).astype(BF)
    d = lambda v: jnp.dot(m_bf, v, preferred_element_type=F32)
    return (d(lo) + d(mid)) + d(hi)


def _rms_norm(x):
    return x * lax.rsqrt(jnp.mean(x * x, axis=-1, keepdims=True) + RMS_EPS)


def _masks64():
    i = jnp.arange(CH)[:, None]
    j = jnp.arange(CH)[None, :]
    ms = [(i >= j), (i > j)]
    for s in (1, 2, 4, 8, 16, 32):
        ms.append((i // (2 * s) == j // (2 * s)) & (i % (2 * s) >= s) & (j % (2 * s) < s))
    ms.append(i == j)
    return jnp.stack(ms).astype(F32)


def _cparams():
    return pltpu.CompilerParams(dimension_semantics=("arbitrary", "arbitrary"),
                                vmem_limit_bytes=VMEM_LIMIT)


def _full(shape):
    n = len(shape)
    return pl.BlockSpec(shape, lambda b, t: (0,) * n, pipeline_mode=pl.Buffered(1))


def _ln_in_kernel(x_ref, m_ref, g_ref, b_ref, o_ref, *, pad):
    t = pl.program_id(1)
    x = jnp.where(t == 0, m_ref[...], x_ref[...])
    mu = jnp.mean(x, axis=-1, keepdims=True)
    xc = x - mu
    var = jnp.mean(xc * xc, axis=-1, keepdims=True)
    y = xc * lax.rsqrt(var + LN_EPS) * g_ref[...] + b_ref[...]
    row = t * TILE + lax.broadcasted_iota(jnp.int32, (TILE, 1), 0)
    o_ref[...] = jnp.where(row >= pad, y, 0.0)


def _ln_in_call(x, meta_tile, g, b, pad):
    bsz, seq, d = x.shape
    return pl.pallas_call(
        functools.partial(_ln_in_kernel, pad=pad),
        grid=(bsz, seq // TILE + 1),
        in_specs=[pl.BlockSpec((None, TILE, d), lambda b_, t: (b_, jnp.maximum(t - 1, 0), 0)),
                  _full((TILE, d)), _full((1, d)), _full((1, d))],
        out_specs=pl.BlockSpec((None, TILE, d), lambda b_, t: (b_, t, 0)),
        out_shape=jax.ShapeDtypeStruct((bsz, seq + TILE, d), F32),
        compiler_params=_cparams(),
        name="ln_in",
    )(x, meta_tile, g.reshape(1, d), b.reshape(1, d))


S5_CK = 8
S5_NCK = TILE // S5_CK
S5_PACK = 2 * S5_P
S5_LANES = S5_NG * S5_PACK
PROJ_CHUNK = 512
GATE_CHUNK = 256
MIX_PLAN = "gsh" + "gfgfgfgfh" * 4 + "gs" * 8 + "gg" + "hDhA" + "hhBhh" + "C"


def _transpose_pieces(arrs, lm_ref):
    for k, d in enumerate((4, 2, 1)):
        m = lm_ref[k:k + 1, :] > 0
        new = list(arrs)
        for s_ in range(8):
            if s_ & d == 0:
                lo, hi = arrs[s_], arrs[s_ + d]
                new[s_] = jnp.where(m, pltpu.roll(hi, 16 * d, 1), lo)
                new[s_ + d] = jnp.where(m, hi, pltpu.roll(lo, 128 - 16 * d, 1))
        arrs = new
    return arrs


def _s5_scan_block(b, cs, carry, tabs, st_ref):
    rows = slice(8 * b, 8 * b + 8)
    v = st_ref[rows, cs]
    for n, d in enumerate((1, 2, 4)):
        vsh = pltpu.roll(v, d, 0)
        v = v + (tabs[2 + 2 * n] * vsh + tabs[3 + 2 * n] * pltpu.roll(vsh, S5_P, 1))
    cb = jnp.broadcast_to(carry, (8, S5_PACK))
    v = v + (tabs[0] * cb + tabs[1] * pltpu.roll(cb, S5_P, 1))
    st_ref[rows, cs] = v
    return v[7:8, :]


def _proj_s5_kernel(h_ref, w_ref, m1_ref, q_ref, pw_ref, lm_ref, d_ref, g1_ref, g2_ref,
                    p_ref, yd_ref, u_ref, ug_ref, yq_ref, st_ref, y_ref, c_ref):
    t = pl.program_id(1)

    @pl.when(t == 0)
    def _():
        c_ref[...] = jnp.zeros_like(c_ref)

    hb = h_ref[...].astype(BF)
    u = jnp.dot(hb, w_ref[:, OFF_S5:OFF_S5 + BRANCH_W], preferred_element_type=F32)
    for j in range(4):
        u_ref[j] = u[:, j * 128:(j + 1) * 128]

    starts = iter(range(0, NP, PROJ_CHUNK))

    def proj_chunks(n):
        for j in itertools.islice(starts, n):
            w = min(PROJ_CHUNK, NP - j)
            p_ref[:, j:j + w] = jnp.dot(hb, w_ref[:, j:j + w], preferred_element_type=F32)

    for j in range(4):
        proj_chunks(1)
        arrs = [u_ref[j, pl.ds(s_, S5_NCK, stride=S5_CK), :] for s_ in range(S5_CK)]
        arrs = _transpose_pieces(arrs, lm_ref)
        for i in range(8):
            g = 8 * j + i
            ug_ref[:, g * 128:(g + 1) * 128] = arrs[i].astype(BF)
    row0 = lax.broadcasted_iota(jnp.int32, (S5_NCK, S5_PACK), 0) == 0
    last_inc = []
    for g in range(S5_NG):
        if g % 11 == 10:
            proj_chunks(1)
        cs = slice(g * S5_PACK, (g + 1) * S5_PACK)
        tw = jnp.dot(ug_ref[:, cs], m1_ref[g], preferred_element_type=F32)
        yq_ref[:, cs] = tw[:, :S5_PACK]
        inc = tw[:, S5_PACK:]
        last_inc.append(inc[S5_NCK - 1:S5_NCK, :])
        st_ref[:, cs] = jnp.where(row0, c_ref[:, cs], pltpu.roll(inc, 1, 0))

    for g in range(S5_NG):
        if g % 8 == 0:
            proj_chunks(1)
        cs = slice(g * S5_PACK, (g + 1) * S5_PACK)
        tabs = [pw_ref[8 * k:8 * k + 8, cs] for k in range(8)]
        x = jnp.zeros((1, S5_PACK), F32)
        for b in range(S5_NCK // 8):
            x = _s5_scan_block(b, cs, x, tabs, st_ref)
        c_ref[:, cs] = last_inc[g] + (tabs[0][0:1] * x + tabs[1][0:1] * pltpu.roll(
            jnp.broadcast_to(x, (8, S5_PACK)), S5_P, 1)[0:1])

    for g in range(S5_NG):
        if g % 16 == 0:
            proj_chunks(1)
        cs = slice(g * S5_PACK, (g + 1) * S5_PACK)
        yq_ref[:, cs] = yq_ref[:, cs] + _mm(st_ref[:, cs], q_ref[g])
    proj_chunks(NP)
    for j in range(4):
        arrs = _transpose_pieces([yq_ref[:, (8 * j + i) * 128:(8 * j + i + 1) * 128] for i in range(8)], lm_ref)
        for s_ in range(S5_CK):
            y_ref[j, pl.ds(s_, S5_NCK, stride=S5_CK), :] = arrs[s_]
    y = jnp.concatenate([y_ref[j] for j in range(4)], axis=1) + d_ref[...] * u
    y = 0.5 * y * (1.0 + jnp.tanh(math.sqrt(2.0 / math.pi) * (y + 0.044715 * (y * y * y))))
    yb = y.astype(BF)
    glu = jnp.dot(yb, g1_ref[...], preferred_element_type=F32) * _sigmoid(
        jnp.dot(yb, g2_ref[...], preferred_element_type=F32))
    z = jnp.dot(hb, w_ref[:, OFF_S5 + BRANCH_W:OFF_S5 + 2 * BRANCH_W], preferred_element_type=F32)
    yd_ref[...] = (glu * _silu(z)).astype(yd_ref.dtype)


def _proj_s5_call(h, w1, m1, q, pw, lm, dvec, g1, g2):
    bsz, lp, d = h.shape
    return pl.pallas_call(
        _proj_s5_kernel,
        grid=(bsz, lp // TILE),
        in_specs=[pl.BlockSpec((None, TILE, d), lambda b_, t: (b_, t, 0)), _full((d, N1)),
                  _full((S5_NG, 128, 256)), _full((S5_NG, 128, 128)), _full((64, S5_LANES)), _full((8, 128)),
                  _full((1, 512)), _full((512, 512)), _full((512, 512))],
        out_specs=[pl.BlockSpec((None, TILE, NP), lambda b_, t: (b_, t, 0)),
                   pl.BlockSpec((None, TILE, BRANCH_W), lambda b_, t: (b_, t, 0))],
        out_shape=[jax.ShapeDtypeStruct((bsz, lp, NP), F32), jax.ShapeDtypeStruct((bsz, lp, BRANCH_W), BF)],
        scratch_shapes=[pltpu.VMEM((4, TILE, 128), F32), pltpu.VMEM((S5_NCK, S5_LANES), BF),
                        pltpu.VMEM((S5_NCK, S5_LANES), F32), pltpu.VMEM((S5_NCK, S5_LANES), F32),
                        pltpu.VMEM((4, TILE, 128), F32), pltpu.VMEM((1, S5_LANES), F32)],
        compiler_params=_cparams(),
        name="proj_s5",
    )(h, w1, m1, q, pw, lm, dvec, g1, g2)


def _conv_chunk(xp_ref, cw, r0):
    x = xp_ref[r0 + 8:r0 + 8 + CH, :] * cw[CONV_K - 1:CONV_K, :]
    for j in range(CONV_K - 1):
        s = r0 + 8 - (CONV_K - 1) + j
        x = x + xp_ref[s:s + CH, :] * cw[j:j + 1, :]
    return x


def _gdn_steps(qkv_ref, z_ref, ba_ref, cw_ref, lane_ref, nw_ref, msk_ref, tri_ref, y_ref, xp_ref, s_ref):
    t = pl.program_id(1)

    @pl.when(t == 0)
    def _():
        xp_ref[0:8, :] = jnp.zeros((8, 3 * BRANCH_W), F32)
        s_ref[...] = jnp.zeros_like(s_ref)

    xp_ref[8:8 + TILE, :] = qkv_ref[...]
    yield
    cw = cw_ref[...]
    exp_a = lane_ref[0:1, :]
    dt_bias = lane_ref[1:2, :]
    nw = nw_ref[...]
    causal = msk_ref[0]
    strict = msk_ref[1]
    tri = tri_ref[...]
    items = []
    a_list = []
    for c in range(NCH):
        r0 = c * CH
        qkv = _silu(_conv_chunk(xp_ref, cw, r0))
        ba = ba_ref[r0:r0 + CH, :]
        beta_all = _sigmoid(ba)
        gl = -(exp_a * _softplus(ba + dt_bias))
        gc = _mm3_left(tri, gl)
        gc_t = gc.T
        for h in range(GDN_HEADS):
            sl = slice(h * GDN_DK, (h + 1) * GDN_DK)
            q = qkv[:, sl]
            k = qkv[:, BRANCH_W + h * GDN_DK:BRANCH_W + (h + 1) * GDN_DK]
            v = qkv[:, 2 * BRANCH_W + h * GDN_DK:2 * BRANCH_W + (h + 1) * GDN_DK]
            q = q * (lax.rsqrt(jnp.sum(q * q, axis=-1, keepdims=True) + RMS_EPS) * GDN_DK ** -0.5)
            k = k * lax.rsqrt(jnp.sum(k * k, axis=-1, keepdims=True) + RMS_EPS)
            beta = beta_all[:, h:h + 1]
            gcol = gc[:, 4 + h:5 + h]
            grow = gc_t[4 + h:5 + h, :]
            decay = jnp.exp(jnp.where(causal > 0, gcol - grow, -jnp.inf))
            kb = k * beta
            eg = jnp.exp(gcol)
            glast = gc[CH - 1:CH, 4 + h:5 + h]
            a_list.append(strict * (_mm_nt(kb, k) * decay))
            items.append(dict(
                rhs=jnp.concatenate([v * beta, kb * eg], axis=1),
                qk=causal * (_mm_nt(q, k) * decay),
                qd=q * eg,
                kd=k * jnp.exp(glast - gcol),
                gtot=jnp.exp(glast)))
            yield
    tinvs = [msk_ref[8] - a * msk_ref[2] for a in a_list]
    for lvl in range(1, 6):
        m = msk_ref[2 + lvl]
        t1 = [_mm(x, a * m) for x, a in zip(tinvs, a_list)]
        tinvs = [x - _mm(t_, x) for x, t_ in zip(tinvs, t1)]
        yield
    uws = [_mm(tinv, it["rhs"]) for tinv, it in zip(tinvs, items)]
    kd_uw = [_mm_tn(it["kd"], uw) for it, uw in zip(items, uws)]
    qk_uw = [_mm(it["qk"], uw) for it, uw in zip(items, uws)]
    yield
    states = [s_ref[h] for h in range(GDN_HEADS)]
    for c in range(NCH):
        r0 = c * CH
        for h in range(GDN_HEADS):
            sl = slice(h * GDN_DK, (h + 1) * GDN_DK)
            n = c * GDN_HEADS + h
            s = states[h]
            o = _mm(items[n]["qd"] - qk_uw[n][:, GDN_DK:], s) + qk_uw[n][:, :GDN_DK]
            states[h] = (s * items[n]["gtot"] - _mm(kd_uw[n][:, GDN_DK:], s)) + kd_uw[n][:, :GDN_DK]
            y = _rms_norm(o) * nw * _silu(z_ref[r0:r0 + CH, sl])
            y_ref[r0:r0 + CH, sl] = y.astype(y_ref.dtype)
        if c == NCH - 1:
            for h in range(GDN_HEADS):
                s_ref[h] = states[h]
            xp_ref[0:8, :] = xp_ref[TILE:TILE + 8, :]
        yield


def _ssd_steps(xbc_ref, zdt_ref, cw_ref, cb_ref, vec_ref, eyet_ref, maskt_ref, bd_ref, tri_ref, ones_ref,
               y_ref, xp_ref, s_ref, *, pad):
    t = pl.program_id(1)

    @pl.when(t == 0)
    def _():
        xp_ref[0:8, :] = jnp.zeros((8, 1024), F32)
        s_ref[...] = jnp.zeros_like(s_ref)

    xp_ref[8:8 + TILE, :] = xbc_ref[...]
    yield
    cw = cw_ref[...]
    cb = cb_ref[...]
    dt_bias = vec_ref[0:1, :]
    neg_a = vec_ref[1:2, :]
    d_skip = vec_ref[2:3, :]
    nw = vec_ref[3:4, :]
    tri = tri_ref[...]
    ones = ones_ref[...]
    gw = M2_HEADDIM * (M2_HEADS // M2_GROUPS)
    for c in range(NCH):
        r0 = c * CH
        xbc = _silu(_conv_chunk(xp_ref, cw, r0) + cb)
        xs = xbc[:, :BRANCH_W]
        row = t * TILE + r0 + lax.broadcasted_iota(jnp.int32, (CH, 1), 0)
        valid = (row >= pad).astype(F32)
        dt = _softplus(zdt_ref[r0:r0 + CH, BRANCH_W:] + dt_bias) * valid
        a = dt * neg_a
        acum = _mm3_left(tri, a)
        xdt = xs * dt
        z = zdt_ref[r0:r0 + CH, :BRANCH_W]
        for g in range(M2_GROUPS):
            gs = slice(g * gw, (g + 1) * gw)
            bg = xbc[:, BRANCH_W + g * M2_DSTATE:BRANCH_W + (g + 1) * M2_DSTATE]
            cg = xbc[:, BRANCH_W + M2_GROUPS * M2_DSTATE + g * M2_DSTATE:
                     BRANCH_W + M2_GROUPS * M2_DSTATE + (g + 1) * M2_DSTATE]
            ag = acum[:, gs]
            cbt = _mm_nt(cg, jnp.concatenate([bg] * 4, axis=0))
            arow = _mm3_left(ones, ag * eyet_ref[...])
            lm = jnp.exp(jnp.where(maskt_ref[...] > 0, ag - arow, -jnp.inf))
            xg = xdt[:, gs]
            xbd = jnp.concatenate([xg] * 4, axis=0) * bd_ref[...]
            y_diag = _mm(cbt * lm, xbd)
            alast = ag[CH - 1:CH, :]
            s = s_ref[g]
            y_off = _mm(cg, s) * jnp.exp(ag)
            s_ref[g] = s * jnp.exp(alast) + _mm_tn(bg, xg * jnp.exp(alast - ag))
            y = (y_diag + y_off + d_skip[:, gs] * xs[:, gs]) * _silu(z[:, gs])
            y = _rms_norm(y) * nw[:, gs]
            y_ref[r0:r0 + CH, gs] = y.astype(y_ref.dtype)
            if c == NCH - 1 and g == M2_GROUPS - 1:
                xp_ref[0:8, :] = xp_ref[TILE:TILE + 8, :]
            yield


def _hgrn_steps(p_ref, vec_ref, nw_ref, msk_ref, tri_ref, y_ref, s_ref):
    t = pl.program_id(1)

    @pl.when(t == 0)
    def _():
        s_ref[...] = jnp.zeros_like(s_ref)

    yield
    log_lb = vec_ref[0:1, :]
    log1m_lb = vec_ref[1:2, :]
    one_m_lb = vec_ref[2:3, :]
    nw = nw_ref[...]
    causal = msk_ref[0]
    tri = tri_ref[...]
    hsl = [slice(h * HG_DK, (h + 1) * HG_DK) for h in range(HG_HEADS)]
    nsub = CH // SUB
    prep = []
    for c in range(NCH):
        r0 = c * CH
        q = _silu(p_ref[r0:r0 + CH, 0:512])
        zf = p_ref[r0:r0 + CH, 512:1024]
        e = jnp.exp(-jnp.abs(zf))
        lsig = jnp.minimum(zf, 0.0) - jnp.log1p(e)
        bb = log1m_lb + lsig
        logf = jnp.maximum(log_lb, bb) + jnp.log1p(jnp.exp(-jnp.abs(log_lb - bb)))
        sig_neg = jnp.where(zf >= 0, e, 1.0) / (1.0 + e)
        k = one_m_lb * sig_neg
        gcum = _mm3_left(tri, logf)
        glast = gcum[CH - 1:CH, :]
        subs = []
        for i in range(nsub):
            rs = slice(i * SUB, (i + 1) * SUB)
            n = (i + 1) * SUB
            gref = gcum[i * SUB - 1:i * SUB, :] if i > 0 else jnp.zeros((1, BRANCH_W), F32)
            qt = q[rs] * jnp.exp(gcum[rs] - gref)
            kt = k[:n] * jnp.exp(jnp.minimum(gref - gcum[:n], HG_EXP_CLAMP))
            subs.append((qt, kt, qt * jnp.exp(gref)))
        prep.append(dict(kd=k * jnp.exp(glast - gcum), eg=jnp.exp(glast), subs=subs))
        yield
    incs = [[_mm_tn(p_ref[c * CH:(c + 1) * CH, 1024 + h * HG_DK:1024 + (h + 1) * HG_DK], prep[c]["kd"][:, hsl[h]])
             for h in range(HG_HEADS)] for c in range(NCH)]
    states = [[s_ref[h] for h in range(HG_HEADS)]]
    for c in range(NCH):
        states.append([states[c][h] * prep[c]["eg"][:, hsl[h]] + incs[c][h] for h in range(HG_HEADS)])
    for h in range(HG_HEADS):
        s_ref[h] = states[NCH][h]
    yield
    ams = [[[_mm_nt(prep[c]["subs"][i][0][:, hsl[h]], prep[c]["subs"][i][1][:, hsl[h]])
             * causal[i * SUB:(i + 1) * SUB, :(i + 1) * SUB]
             for h in range(HG_HEADS)] for i in range(nsub)] for c in range(NCH)]
    yield
    for c in range(NCH):
        r0 = c * CH
        for h in range(HG_HEADS):
            iv = p_ref[r0:r0 + CH, 1024 + h * HG_DK:1024 + (h + 1) * HG_DK]
            o = jnp.concatenate(
                [_mm(ams[c][i][h], iv[:(i + 1) * SUB]) + _mm_nt(prep[c]["subs"][i][2][:, hsl[h]], states[c][h])
                 for i in range(nsub)], axis=0)
            z = p_ref[r0:r0 + CH, 1536 + h * HG_DK:1536 + (h + 1) * HG_DK]
            y_ref[r0:r0 + CH, hsl[h]] = (_rms_norm(o) * nw * _silu(z)).astype(y_ref.dtype)
        yield


def _mix_merge_kernel(h_ref, qkv_ref, gz_ref, ba_ref, xbc_ref, zdt_ref, hg_ref, yd_ref,
                      gcw_ref, glane_ref, gnw_ref, scw_ref, scb_ref, svec_ref, eyet_ref, maskt_ref, bd_ref, ones_ref,
                      hvec_ref, hnw_ref, msk_ref, tri_ref, wg_ref, wb_ref, wo_ref, g_ref, b_ref,
                      o_ref, gxp_ref, gs_ref, sxp_ref, ss_ref, hs_ref, ya_ref, yb_ref, yc_ref, gate_ref, *, pad):
    t = pl.program_id(1)
    h = h_ref[...]
    hb = h.astype(BF)

    def gate_piece(j):
        cs = slice(j * GATE_CHUNK, (j + 1) * GATE_CHUNK)
        gate_ref[:, cs] = jnp.tanh(jnp.dot(hb, wg_ref[:, cs], preferred_element_type=F32)) + 1.0

    def branch_piece(br, y_ref):
        cs = slice(br * D_MODEL, (br + 1) * D_MODEL)
        gate_ref[:, cs] = gate_ref[:, cs] * jnp.dot(y_ref[...], wb_ref[br], preferred_element_type=F32)

    gdn = _gdn_steps(qkv_ref, gz_ref, ba_ref, gcw_ref, glane_ref, gnw_ref, msk_ref, tri_ref, ya_ref, gxp_ref, gs_ref)
    ssd = _ssd_steps(xbc_ref, zdt_ref, scw_ref, scb_ref, svec_ref, eyet_ref, maskt_ref, bd_ref, tri_ref, ones_ref,
                     yb_ref, sxp_ref, ss_ref, pad=pad)
    hgrn = _hgrn_steps(hg_ref, hvec_ref, hnw_ref, msk_ref, tri_ref, yc_ref, hs_ref)
    gates = iter([functools.partial(gate_piece, j) for j in range(N_BRANCH * D_MODEL // GATE_CHUNK)])
    emit = {"g": lambda: next(gdn), "s": lambda: next(ssd), "h": lambda: next(hgrn), "f": lambda: next(gates)(),
            "A": lambda: branch_piece(0, ya_ref), "B": lambda: branch_piece(1, yb_ref),
            "C": lambda: branch_piece(2, yc_ref), "D": lambda: branch_piece(3, yd_ref)}
    for tok in MIX_PLAN:
        emit[tok]()
    mixed = ((gate_ref[:, 0:D_MODEL] + gate_ref[:, D_MODEL:2 * D_MODEL])
             + (gate_ref[:, 2 * D_MODEL:3 * D_MODEL] + gate_ref[:, 3 * D_MODEL:4 * D_MODEL]))
    out = jnp.dot(mixed.astype(BF), wo_ref[...], preferred_element_type=F32)
    r = ALPHA * h + out
    mu = jnp.mean(r, axis=-1, keepdims=True)
    rc = r - mu
    var = jnp.mean(rc * rc, axis=-1, keepdims=True)
    y = rc * lax.rsqrt(var + LN_EPS) * g_ref[...] + b_ref[...]
    row = t * TILE + lax.broadcasted_iota(jnp.int32, (TILE, 1), 0)
    o_ref[...] = jnp.where(row >= pad, y, 0.0)


def _mix_merge_call(h, p, yd, gdn_prm, ssd_prm, hg_prm, msk, tri, wg, wb, wo, g, b, pad, drop_first_tile):
    bsz, lp, d = h.shape
    tok = lambda w, j=0: pl.BlockSpec((None, TILE, w), lambda b_, t: (b_, t, j))
    return pl.pallas_call(
        functools.partial(_mix_merge_kernel, pad=pad),
        grid=(bsz, lp // TILE),
        in_specs=[tok(d), tok(1536, OFF_QKV // 1536), tok(512, OFF_GZ // 512), tok(128, OFF_BA // 128),
                  tok(1024, OFF_XBC // 1024), tok(1024, OFF_MZ // 1024), tok(2048, OFF_HG // 2048), tok(BRANCH_W),
                  _full((CONV_K, 1536)), _full((8, 128)), _full((1, GDN_DK)),
                  _full((CONV_K, 1024)), _full((1, 1024)), _full((8, 512)),
                  _full((CH, 256)), _full((CH, 256)), _full((256, 256)), _full((CH, CH)),
                  _full((8, 512)), _full((1, HG_DK)), _full((9, CH, CH)), _full((CH, CH)),
                  _full((d, N_BRANCH * d)), _full((N_BRANCH, BRANCH_W, d)), _full((d, d)),
                  _full((1, d)), _full((1, d))],
        out_specs=(pl.BlockSpec((None, TILE, d), lambda b_, t: (b_, jnp.maximum(t - 1, 0), 0))
                   if drop_first_tile else tok(d)),
        out_shape=jax.ShapeDtypeStruct((bsz, lp - TILE if drop_first_tile else lp, d), F32),
        scratch_shapes=[pltpu.VMEM((TILE + 8, 1536), F32), pltpu.VMEM((GDN_HEADS, GDN_DK, GDN_DK), F32),
                        pltpu.VMEM((TILE + 8, 1024), F32), pltpu.VMEM((M2_GROUPS, M2_DSTATE, 256), F32),
                        pltpu.VMEM((HG_HEADS, HG_DK, HG_DK), F32),
                        pltpu.VMEM((TILE, BRANCH_W), BF), pltpu.VMEM((TILE, BRANCH_W), BF),
                        pltpu.VMEM((TILE, BRANCH_W), BF), pltpu.VMEM((TILE, N_BRANCH * D_MODEL), F32)],
        compiler_params=_cparams(),
        name="mix_merge",
    )(h, p, p, p, p, p, p, yd, *gdn_prm, *ssd_prm, *hg_prm, msk, tri, wg, wb, wo, g.reshape(1, d), b.reshape(1, d))


def _split_w_in(w):
    sizes = (1536, 512, 4, 4, 1024, 512, 8, 512, 512, 512, 512, 512, 512, 4096)
    offs = [0]
    for s in sizes:
        offs.append(offs[-1] + s)
    return [w[:, offs[i]:offs[i + 1]] for i in range(len(sizes))]


def _prep_w1(w):
    (qkv, gz, gb, ga, xbc, mz, mdt, hq, hf, hi, hz, su, sz, gates) = _split_w_in(w)
    d = w.shape[0]
    ba = jnp.concatenate([gb, ga, jnp.zeros((d, 120), w.dtype)], axis=1)
    mdt_e = jnp.repeat(mdt, M2_HEADDIM, axis=1)
    w1 = jnp.concatenate([qkv, gz, xbc, mz, mdt_e, hq, hf, hi, hz, ba, su, sz], axis=1)
    return w1.astype(BF), (0.5 * gates).astype(BF)


def _lane_row(vals, off, width=128):
    return jnp.zeros((width,), F32).at[off:off + vals.shape[0]].set(vals.astype(F32))


def _prep_s5(a_re, a_im, b_re, b_im, c_re, c_im, log_dt):
    f = lambda v: v.astype(F32)
    a_re, a_im, b_re, b_im, c_re, c_im = map(f, (a_re, a_im, b_re, b_im, c_re, c_im))
    dt = jnp.exp(f(log_dt))[:, None]

    def lam_pow(k):
        mag = jnp.exp(a_re * dt * k)
        return mag * jnp.cos(a_im * dt * k), mag * jnp.sin(a_im * dt * k)

    lam_re, lam_im = lam_pow(1.0)
    den = jnp.square(a_re) + jnp.square(a_im)
    nr, ni = lam_re - 1.0, lam_im
    z_re, z_im = (nr * a_re + ni * a_im) / den, (ni * a_re - nr * a_im) / den
    bb_re = z_re[..., None] * b_re - z_im[..., None] * b_im
    bb_im = z_re[..., None] * b_im + z_im[..., None] * b_re
    ks = jnp.arange(S5_CK, dtype=F32)[:, None, None]
    pr, pi = lam_pow(ks)
    p1r, p1i = lam_pow(ks + 1.0)
    mr, mi = lam_pow(S5_CK * (ks + 1.0))

    def c_times(qr, qi):
        return (c_re[None] * qr[:, :, None, :] - c_im[None] * qi[:, :, None, :],
                c_re[None] * qi[:, :, None, :] + c_im[None] * qr[:, :, None, :])

    cl_re, cl_im = c_times(pr, pi)
    kd = (jnp.einsum('dgcp,gpe->dgce', cl_re, bb_re) - jnp.einsum('dgcp,gpe->dgce', cl_im, bb_im))
    s_i = jnp.arange(S5_CK)[:, None]
    t_i = jnp.arange(S5_CK)[None, :]
    kt = jnp.where((t_i >= s_i)[:, :, None, None, None], kd[jnp.clip(t_i - s_i, 0, S5_CK - 1)], 0.0)
    toep = jnp.transpose(kt, (2, 0, 4, 1, 3)).reshape(S5_NG, 128, 128)
    rr, ri = pr[::-1], pi[::-1]
    w_re = rr[..., None] * bb_re[None] - ri[..., None] * bb_im[None]
    w_im = rr[..., None] * bb_im[None] + ri[..., None] * bb_re[None]
    to_rows = lambda w: jnp.transpose(w, (1, 0, 3, 2)).reshape(S5_NG, 128, S5_P)
    m1 = jnp.concatenate([toep, to_rows(w_re), to_rows(w_im)], axis=2).astype(BF)
    q_re, q_im = c_times(p1r, p1i)
    to_cols = lambda v: jnp.transpose(v, (1, 3, 0, 2)).reshape(S5_NG, S5_P, 128)
    q = jnp.concatenate([to_cols(q_re), -to_cols(q_im)], axis=1).astype(BF)
    pack1 = lambda v: jnp.concatenate([v, v], axis=-1).reshape(v.shape[0], S5_LANES)
    pack2 = lambda v: jnp.concatenate([-v, v], axis=-1).reshape(v.shape[0], S5_LANES)
    row = jnp.arange(8)[:, None]
    tabs = [pack1(mr), pack2(mi)]
    for dd in (1, 2, 4):
        tabs.append(jnp.where(row >= dd, pack1(mr[dd - 1:dd]), 0.0))
        tabs.append(jnp.where(row >= dd, pack2(mi[dd - 1:dd]), 0.0))
    lane_piece = jnp.arange(128) // 16
    lm = jnp.stack([(lane_piece & dd) != 0 for dd in (4, 2, 1)] + [jnp.zeros((128,), bool)] * 5).astype(F32)
    return m1, q, jnp.concatenate(tabs, axis=0), lm


@jax.jit
def kernel(x, meta_tokens, ln_in_g, ln_in_b, w_in, gdn_conv_w, gdn_A_log, gdn_dt_bias, gdn_norm_w, m2_conv_w, m2_conv_b, m2_dt_bias, m2_A_log, m2_D, m2_norm_w, hg_lb_logits, hg_norm_w, s5_A_re, s5_A_im, s5_B_re, s5_B_im, s5_C_re, s5_C_im, s5_D, s5_log_dt, s5_glu_w1, s5_glu_w2, w_branch, w_out, ln_g, ln_b):
    bsz, seq, d = x.shape
    ltot = N_META + seq
    lp = -(-ltot // TILE) * TILE
    pad = lp - ltot
    assert pad + N_META == TILE, "real tokens must start on the second tile"
    meta_tile = jnp.concatenate([jnp.zeros((pad, d), F32), meta_tokens.astype(F32)], axis=0)
    h = _ln_in_call(x.astype(F32), meta_tile, ln_in_g.astype(F32), ln_in_b.astype(F32), pad)

    msk = _masks64()
    tri = msk[0].astype(BF)
    ones = jnp.ones((CH, CH), BF)
    jj = jnp.arange(256)
    eyet = (jnp.arange(CH)[:, None] == (jj % CH)[None, :]).astype(F32)
    maskt = (jnp.arange(CH)[:, None] >= (jj % CH)[None, :]).astype(F32)
    bd = ((jj // CH)[:, None] == (jj // CH)[None, :]).astype(F32)

    cum = jnp.cumsum(jax.nn.softmax(hg_lb_logits.astype(F32), axis=0), axis=0)
    lower_bounds = cum - cum[0:1]

    for l in range(w_in.shape[0]):
        w1, wg = _prep_w1(w_in[l])
        m1, q, pw_tab, lm = _prep_s5(s5_A_re[l], s5_A_im[l], s5_B_re[l], s5_B_im[l], s5_C_re[l], s5_C_im[l],
                                     s5_log_dt[l])
        p, yd = _proj_s5_call(h, w1, m1, q, pw_tab, lm, s5_D[l].astype(F32).reshape(1, -1),
                              s5_glu_w1[l].astype(BF), s5_glu_w2[l].astype(BF))

        lane = jnp.stack([_lane_row(jnp.exp(gdn_A_log[l].astype(F32)), 4), _lane_row(gdn_dt_bias[l], 4)]
                         + [jnp.zeros((128,), F32)] * 6)
        gdn_prm = (gdn_conv_w[l].astype(F32), lane, gdn_norm_w[l].astype(F32).reshape(1, GDN_DK))

        rep = lambda v: jnp.repeat(v.astype(F32), M2_HEADDIM)
        vec = jnp.stack([rep(m2_dt_bias[l]), rep(-jnp.exp(m2_A_log[l].astype(F32))), rep(m2_D[l]),
                         m2_norm_w[l].astype(F32)] + [jnp.zeros((BRANCH_W,), F32)] * 4)
        ssd_prm = (m2_conv_w[l].astype(F32), m2_conv_b[l].astype(F32).reshape(1, -1), vec, eyet, maskt, bd, ones)

        lb = lower_bounds[l]
        hvec = jnp.stack([jnp.log(lb), jnp.log1p(-lb), 1.0 - lb] + [jnp.zeros((BRANCH_W,), F32)] * 5)
        hg_prm = (hvec, hg_norm_w[l].astype(F32).reshape(1, HG_DK))

        h = _mix_merge_call(h, p, yd, gdn_prm, ssd_prm, hg_prm, msk, tri, wg, (0.5 * w_branch[l]).astype(BF),
                            w_out[l].astype(BF), ln_g[l].astype(F32), ln_b[l].astype(F32), pad,
                            drop_first_tile=(l == w_in.shape[0] - 1))
    return h.astype(x.dtype)
```

```python
import functools
import itertools
import math

import jax
import jax.numpy as jnp
from jax import lax
from jax.experimental import pallas as pl
from jax.experimental.pallas import tpu as pltpu

F32 = jnp.float32
BF = jnp.bfloat16

D_MODEL = 1024
DEPTH = 4
N_META = 16
CONV_K = 4
N_BRANCH = 4
BRANCH_W = 512
GDN_HEADS = 4
GDN_DK = 128
M2_HEADS = 8
M2_HEADDIM = 64
M2_GROUPS = 2
M2_DSTATE = 128
HG_HEADS = 4
HG_DK = 128
S5_GROUP = 16
S5_NG = 32
S5_P = 64
S5_STATE = S5_NG * S5_P
ALPHA = (2 * DEPTH) ** 0.25
LN_EPS = 1e-5
RMS_EPS = 1e-6

TILE = 256
CH = 64
NCH = TILE // CH
SUB = 16
HG_EXP_CLAMP = 80.0
VMEM_LIMIT = 56 * 1024 * 1024

OFF_QKV, OFF_GZ, OFF_XBC, OFF_MZ, OFF_MDT = 0, 1536, 2048, 3072, 3584
OFF_HG, OFF_BA, OFF_S5 = 4096, 6144, 6272
NP = 6272
N1 = 7296


def _sigmoid(x):
    return 0.5 * (jnp.tanh(0.5 * x) + 1.0)


def _silu(x):
    return x * _sigmoid(x)


def _softplus(x):
    return jnp.maximum(x, 0.0) + jnp.log1p(jnp.exp(-jnp.abs(x)))


def _mm(a, b):
    return jnp.dot(a.astype(BF), b.astype(BF), preferred_element_type=F32)


def _mm_nt(a, b):
    return lax.dot_general(a.astype(BF), b.astype(BF), (((1,), (1,)), ((), ())),
                           preferred_element_type=F32)


def _mm_tn(a, b):
    return lax.dot_general(a.astype(BF), b.astype(BF), (((0,), (0,)), ((), ())),
                           preferred_element_type=F32)


def _mm3_left(m_bf, x):
    hi = x.astype(BF)
    r1 = x - hi.astype(F32)
    mid = r1.astype(BF)
    lo = (r1 - mid.astype(F32)).astype(BF)
    d = lambda v: jnp.dot(m_bf, v, preferred_element_type=F32)
    return (d(lo) + d(mid)) + d(hi)


def _rms_norm(x):
    return x * lax.rsqrt(jnp.mean(x * x, axis=-1, keepdims=True) + RMS_EPS)


def _masks64():
    i = jnp.arange(CH)[:, None]
    j = jnp.arange(CH)[None, :]
    ms = [(i >= j), (i > j)]
    for s in (1, 2, 4, 8, 16, 32):
        ms.append((i // (2 * s) == j // (2 * s)) & (i % (2 * s) >= s) & (j % (2 * s) < s))
    ms.append(i == j)
    return jnp.stack(ms).astype(F32)


def _cparams():
    return pltpu.CompilerParams(dimension_semantics=("arbitrary", "arbitrary"),
                                vmem_limit_bytes=VMEM_LIMIT)


def _full(shape):
    n = len(shape)
    return pl.BlockSpec(shape, lambda b, t: (0,) * n, pipeline_mode=pl.Buffered(1))


def _ln_in_kernel(x_ref, m_ref, g_ref, b_ref, o_ref, *, pad):
    t = pl.program_id(1)
    x = jnp.where(t == 0, m_ref[...], x_ref[...])
    mu = jnp.mean(x, axis=-1, keepdims=True)
    xc = x - mu
    var = jnp.mean(xc * xc, axis=-1, keepdims=True)
    y = xc * lax.rsqrt(var + LN_EPS) * g_ref[...] + b_ref[...]
    row = t * TILE + lax.broadcasted_iota(jnp.int32, (TILE, 1), 0)
    o_ref[...] = jnp.where(row >= pad, y, 0.0)


def _ln_in_call(x, meta_tile, g, b, pad):
    bsz, seq, d = x.shape
    return pl.pallas_call(
        functools.partial(_ln_in_kernel, pad=pad),
        grid=(bsz, seq // TILE + 1),
        in_specs=[pl.BlockSpec((None, TILE, d), lambda b_, t: (b_, jnp.maximum(t - 1, 0), 0)),
                  _full((TILE, d)), _full((1, d)), _full((1, d))],
        out_specs=pl.BlockSpec((None, TILE, d), lambda b_, t: (b_, t, 0)),
        out_shape=jax.ShapeDtypeStruct((bsz, seq + TILE, d), F32),
        compiler_params=_cparams(),
        name="ln_in",
    )(x, meta_tile, g.reshape(1, d), b.reshape(1, d))


S5_CK = 8
S5_NCK = TILE // S5_CK
S5_PACK = 2 * S5_P
S5_LANES = S5_NG * S5_PACK
PROJ_CHUNK = 512
GATE_CHUNK = 256
MIX_PLAN = "gsh" + "gfgfgfgfh" * 4 + "gs" * 8 + "gg" + "hDhA" + "hhBhh" + "C"


def _transpose_pieces(arrs, lm_ref):
    for k, d in enumerate((4, 2, 1)):
        m = lm_ref[k:k + 1, :] > 0
        new = list(arrs)
        for s_ in range(8):
            if s_ & d == 0:
                lo, hi = arrs[s_], arrs[s_ + d]
                new[s_] = jnp.where(m, pltpu.roll(hi, 16 * d, 1), lo)
                new[s_ + d] = jnp.where(m, hi, pltpu.roll(lo, 128 - 16 * d, 1))
        arrs = new
    return arrs


def _s5_scan_block(b, cs, carry, tabs, st_ref):
    rows = slice(8 * b, 8 * b + 8)
    v = st_ref[rows, cs]
    for n, d in enumerate((1, 2, 4)):
        vsh = pltpu.roll(v, d, 0)
        v = v + (tabs[2 + 2 * n] * vsh + tabs[3 + 2 * n] * pltpu.roll(vsh, S5_P, 1))
    cb = jnp.broadcast_to(carry, (8, S5_PACK))
    v = v + (tabs[0] * cb + tabs[1] * pltpu.roll(cb, S5_P, 1))
    st_ref[rows, cs] = v
    return v[7:8, :]


def _proj_s5_kernel(h_ref, w_ref, m1_ref, q_ref, pw_ref, lm_ref, d_ref, g1_ref, g2_ref,
                    p_ref, yd_ref, u_ref, ug_ref, yq_ref, st_ref, y_ref, c_ref):
    t = pl.program_id(1)

    @pl.when(t == 0)
    def _():
        c_ref[...] = jnp.zeros_like(c_ref)

    hb = h_ref[...].astype(BF)
    u = jnp.dot(hb, w_ref[:, OFF_S5:OFF_S5 + BRANCH_W], preferred_element_type=F32)
    for j in range(4):
        u_ref[j] = u[:, j * 128:(j + 1) * 128]

    starts = iter(range(0, NP, PROJ_CHUNK))

    def proj_chunks(n):
        for j in itertools.islice(starts, n):
            w = min(PROJ_CHUNK, NP - j)
            p_ref[:, j:j + w] = jnp.dot(hb, w_ref[:, j:j + w], preferred_element_type=F32)

    for j in range(4):
        proj_chunks(1)
        arrs = [u_ref[j, pl.ds(s_, S5_NCK, stride=S5_CK), :] for s_ in range(S5_CK)]
        arrs = _transpose_pieces(arrs, lm_ref)
        for i in range(8):
            g = 8 * j + i
            ug_ref[:, g * 128:(g + 1) * 128] = arrs[i].astype(BF)
    row0 = lax.broadcasted_iota(jnp.int32, (S5_NCK, S5_PACK), 0) == 0
    last_inc = []
    for g in range(S5_NG):
        if g % 11 == 10:
            proj_chunks(1)
        cs = slice(g * S5_PACK, (g + 1) * S5_PACK)
        tw = jnp.dot(ug_ref[:, cs], m1_ref[g], preferred_element_type=F32)
        yq_ref[:, cs] = tw[:, :S5_PACK]
        inc = tw[:, S5_PACK:]
        last_inc.append(inc[S5_NCK - 1:S5_NCK, :])
        st_ref[:, cs] = jnp.where(row0, c_ref[:, cs], pltpu.roll(inc, 1, 0))

    for g in range(S5_NG):
        if g % 8 == 0:
            proj_chunks(1)
        cs = slice(g * S5_PACK, (g + 1) * S5_PACK)
        tabs = [pw_ref[8 * k:8 * k + 8, cs] for k in range(8)]
        x = jnp.zeros((1, S5_PACK), F32)
        for b in range(S5_NCK // 8):
            x = _s5_scan_block(b, cs, x, tabs, st_ref)
        c_ref[:, cs] = last_inc[g] + (tabs[0][0:1] * x + tabs[1][0:1] * pltpu.roll(
            jnp.broadcast_to(x, (8, S5_PACK)), S5_P, 1)[0:1])

    for g in range(S5_NG):
        if g % 16 == 0:
            proj_chunks(1)
        cs = slice(g * S5_PACK, (g + 1) * S5_PACK)
        yq_ref[:, cs] = yq_ref[:, cs] + _mm(st_ref[:, cs], q_ref[g])
    proj_chunks(NP)
    for j in range(4):
        arrs = _transpose_pieces([yq_ref[:, (8 * j + i) * 128:(8 * j + i + 1) * 128] for i in range(8)], lm_ref)
        for s_ in range(S5_CK):
            y_ref[j, pl.ds(s_, S5_NCK, stride=S5_CK), :] = arrs[s_]
    y = jnp.concatenate([y_ref[j] for j in range(4)], axis=1) + d_ref[...] * u
    y = 0.5 * y * (1.0 + jnp.tanh(math.sqrt(2.0 / math.pi) * (y + 0.044715 * (y * y * y))))
    yb = y.astype(BF)
    glu = jnp.dot(yb, g1_ref[...], preferred_element_type=F32) * _sigmoid(
        jnp.dot(yb, g2_ref[...], preferred_element_type=F32))
    z = jnp.dot(hb, w_ref[:, OFF_S5 + BRANCH_W:OFF_S5 + 2 * BRANCH_W], preferred_element_type=F32)
    yd_ref[...] = (glu * _silu(z)).astype(yd_ref.dtype)


def _proj_s5_call(h, w1, m1, q, pw, lm, dvec, g1, g2):
    bsz, lp, d = h.shape
    return pl.pallas_call(
        _proj_s5_kernel,
        grid=(bsz, lp // TILE),
        in_specs=[pl.BlockSpec((None, TILE, d), lambda b_, t: (b_, t, 0)), _full((d, N1)),
                  _full((S5_NG, 128, 256)), _full((S5_NG, 128, 128)), _full((64, S5_LANES)), _full((8, 128)),
                  _full((1, 512)), _full((512, 512)), _full((512, 512))],
        out_specs=[pl.BlockSpec((None, TILE, NP), lambda b_, t: (b_, t, 0)),
                   pl.BlockSpec((None, TILE, BRANCH_W), lambda b_, t: (b_, t, 0))],
        out_shape=[jax.ShapeDtypeStruct((bsz, lp, NP), F32), jax.ShapeDtypeStruct((bsz, lp, BRANCH_W), BF)],
        scratch_shapes=[pltpu.VMEM((4, TILE, 128), F32), pltpu.VMEM((S5_NCK, S5_LANES), BF),
                        pltpu.VMEM((S5_NCK, S5_LANES), F32), pltpu.VMEM((S5_NCK, S5_LANES), F32),
                        pltpu.VMEM((4, TILE, 128), F32), pltpu.VMEM((1, S5_LANES), F32)],
        compiler_params=_cparams(),
        name="proj_s5",
    )(h, w1, m1, q, pw, lm, dvec, g1, g2)


def _conv_chunk(xp_ref, cw, r0):
    x = xp_ref[r0 + 8:r0 + 8 + CH, :] * cw[CONV_K - 1:CONV_K, :]
    for j in range(CONV_K - 1):
        s = r0 + 8 - (CONV_K - 1) + j
        x = x + xp_ref[s:s + CH, :] * cw[j:j + 1, :]
    return x


def _gdn_steps(qkv_ref, z_ref, ba_ref, cw_ref, lane_ref, nw_ref, msk_ref, tri_ref, y_ref, xp_ref, s_ref):
    t = pl.program_id(1)

    @pl.when(t == 0)
    def _():
        xp_ref[0:8, :] = jnp.zeros((8, 3 * BRANCH_W), F32)
        s_ref[...] = jnp.zeros_like(s_ref)

    xp_ref[8:8 + TILE, :] = qkv_ref[...]
    yield
    cw = cw_ref[...]
    exp_a = lane_ref[0:1, :]
    dt_bias = lane_ref[1:2, :]
    nw = nw_ref[...]
    causal = msk_ref[0]
    strict = msk_ref[1]
    tri = tri_ref[...]
    items = []
    a_list = []
    for c in range(NCH):
        r0 = c * CH
        qkv = _silu(_conv_chunk(xp_ref, cw, r0))
        ba = ba_ref[r0:r0 + CH, :]
        beta_all = _sigmoid(ba)
        gl = -(exp_a * _softplus(ba + dt_bias))
        gc = _mm3_left(tri, gl)
        gc_t = gc.T
        for h in range(GDN_HEADS):
            sl = slice(h * GDN_DK, (h + 1) * GDN_DK)
            q = qkv[:, sl]
            k = qkv[:, BRANCH_W + h * GDN_DK:BRANCH_W + (h + 1) * GDN_DK]
            v = qkv[:, 2 * BRANCH_W + h * GDN_DK:2 * BRANCH_W + (h + 1) * GDN_DK]
            q = q * (lax.rsqrt(jnp.sum(q * q, axis=-1, keepdims=True) + RMS_EPS) * GDN_DK ** -0.5)
            k = k * lax.rsqrt(jnp.sum(k * k, axis=-1, keepdims=True) + RMS_EPS)
            beta = beta_all[:, h:h + 1]
            gcol = gc[:, 4 + h:5 + h]
            grow = gc_t[4 + h:5 + h, :]
            decay = jnp.exp(jnp.where(causal > 0, gcol - grow, -jnp.inf))
            kb = k * beta
            eg = jnp.exp(gcol)
            glast = gc[CH - 1:CH, 4 + h:5 + h]
            a_list.append(strict * (_mm_nt(kb, k) * decay))
            items.append(dict(
                rhs=jnp.concatenate([v * beta, kb * eg], axis=1),
                qk=causal * (_mm_nt(q, k) * decay),
                qd=q * eg,
                kd=k * jnp.exp(glast - gcol),
                gtot=jnp.exp(glast)))
            yield
    tinvs = [msk_ref[8] - a * msk_ref[2] for a in a_list]
    for lvl in range(1, 6):
        m = msk_ref[2 + lvl]
        t1 = [_mm(x, a * m) for x, a in zip(tinvs, a_list)]
        tinvs = [x - _mm(t_, x) for x, t_ in zip(tinvs, t1)]
        yield
    uws = [_mm(tinv, it["rhs"]) for tinv, it in zip(tinvs, items)]
    kd_uw = [_mm_tn(it["kd"], uw) for it, uw in zip(items, uws)]
    qk_uw = [_mm(it["qk"], uw) for it, uw in zip(items, uws)]
    yield
    states = [s_ref[h] for h in range(GDN_HEADS)]
    for c in range(NCH):
        r0 = c * CH
        for h in range(GDN_HEADS):
            sl = slice(h * GDN_DK, (h + 1) * GDN_DK)
            n = c * GDN_HEADS + h
            s = states[h]
            o = _mm(items[n]["qd"] - qk_uw[n][:, GDN_DK:], s) + qk_uw[n][:, :GDN_DK]
            states[h] = (s * items[n]["gtot"] - _mm(kd_uw[n][:, GDN_DK:], s)) + kd_uw[n][:, :GDN_DK]
            y = _rms_norm(o) * nw * _silu(z_ref[r0:r0 + CH, sl])
            y_ref[r0:r0 + CH, sl] = y.astype(y_ref.dtype)
        if c == NCH - 1:
            for h in range(GDN_HEADS):
                s_ref[h] = states[h]
            xp_ref[0:8, :] = xp_ref[TILE:TILE + 8, :]
        yield


def _ssd_steps(xbc_ref, zdt_ref, cw_ref, cb_ref, vec_ref, eyet_ref, maskt_ref, bd_ref, tri_ref, ones_ref,
               y_ref, xp_ref, s_ref, *, pad):
    t = pl.program_id(1)

    @pl.when(t == 0)
    def _():
        xp_ref[0:8, :] = jnp.zeros((8, 1024), F32)
        s_ref[...] = jnp.zeros_like(s_ref)

    xp_ref[8:8 + TILE, :] = xbc_ref[...]
    yield
    cw = cw_ref[...]
    cb = cb_ref[...]
    dt_bias = vec_ref[0:1, :]
    neg_a = vec_ref[1:2, :]
    d_skip = vec_ref[2:3, :]
    nw = vec_ref[3:4, :]
    tri = tri_ref[...]
    ones = ones_ref[...]
    gw = M2_HEADDIM * (M2_HEADS // M2_GROUPS)
    for c in range(NCH):
        r0 = c * CH
        xbc = _silu(_conv_chunk(xp_ref, cw, r0) + cb)
        xs = xbc[:, :BRANCH_W]
        row = t * TILE + r0 + lax.broadcasted_iota(jnp.int32, (CH, 1), 0)
        valid = (row >= pad).astype(F32)
        dt = _softplus(zdt_ref[r0:r0 + CH, BRANCH_W:] + dt_bias) * valid
        a = dt * neg_a
        acum = _mm3_left(tri, a)
        xdt = xs * dt
        z = zdt_ref[r0:r0 + CH, :BRANCH_W]
        for g in range(M2_GROUPS):
            gs = slice(g * gw, (g + 1) * gw)
            bg = xbc[:, BRANCH_W + g * M2_DSTATE:BRANCH_W + (g + 1) * M2_DSTATE]
            cg = xbc[:, BRANCH_W + M2_GROUPS * M2_DSTATE + g * M2_DSTATE:
                     BRANCH_W + M2_GROUPS * M2_DSTATE + (g + 1) * M2_DSTATE]
            ag = acum[:, gs]
            cbt = _mm_nt(cg, jnp.concatenate([bg] * 4, axis=0))
            arow = _mm3_left(ones, ag * eyet_ref[...])
            lm = jnp.exp(jnp.where(maskt_ref[...] > 0, ag - arow, -jnp.inf))
            xg = xdt[:, gs]
            xbd = jnp.concatenate([xg] * 4, axis=0) * bd_ref[...]
            y_diag = _mm(cbt * lm, xbd)
            alast = ag[CH - 1:CH, :]
            s = s_ref[g]
            y_off = _mm(cg, s) * jnp.exp(ag)
            s_ref[g] = s * jnp.exp(alast) + _mm_tn(bg, xg * jnp.exp(alast - ag))
            y = (y_diag + y_off + d_skip[:, gs] * xs[:, gs]) * _silu(z[:, gs])
            y = _rms_norm(y) * nw[:, gs]
            y_ref[r0:r0 + CH, gs] = y.astype(y_ref.dtype)
            if c == NCH - 1 and g == M2_GROUPS - 1:
                xp_ref[0:8, :] = xp_ref[TILE:TILE + 8, :]
            yield


def _hgrn_steps(p_ref, vec_ref, nw_ref, msk_ref, tri_ref, y_ref, s_ref):
    t = pl.program_id(1)

    @pl.when(t == 0)
    def _():
        s_ref[...] = jnp.zeros_like(s_ref)

    yield
    log_lb = vec_ref[0:1, :]
    log1m_lb = vec_ref[1:2, :]
    one_m_lb = vec_ref[2:3, :]
    nw = nw_ref[...]
    causal = msk_ref[0]
    tri = tri_ref[...]
    hsl = [slice(h * HG_DK, (h + 1) * HG_DK) for h in range(HG_HEADS)]
    nsub = CH // SUB
    prep = []
    for c in range(NCH):
        r0 = c * CH
        q = _silu(p_ref[r0:r0 + CH, 0:512])
        zf = p_ref[r0:r0 + CH, 512:1024]
        e = jnp.exp(-jnp.abs(zf))
        lsig = jnp.minimum(zf, 0.0) - jnp.log1p(e)
        bb = log1m_lb + lsig
        logf = jnp.maximum(log_lb, bb) + jnp.log1p(jnp.exp(-jnp.abs(log_lb - bb)))
        sig_neg = jnp.where(zf >= 0, e, 1.0) / (1.0 + e)
        k = one_m_lb * sig_neg
        gcum = _mm3_left(tri, logf)
        glast = gcum[CH - 1:CH, :]
        subs = []
        for i in range(nsub):
            rs = slice(i * SUB, (i + 1) * SUB)
            n = (i + 1) * SUB
            gref = gcum[i * SUB - 1:i * SUB, :] if i > 0 else jnp.zeros((1, BRANCH_W), F32)
            qt = q[rs] * jnp.exp(gcum[rs] - gref)
            kt = k[:n] * jnp.exp(jnp.minimum(gref - gcum[:n], HG_EXP_CLAMP))
            subs.append((qt, kt, qt * jnp.exp(gref)))
        prep.append(dict(kd=k * jnp.exp(glast - gcum), eg=jnp.exp(glast), subs=subs))
        yield
    incs = [[_mm_tn(p_ref[c * CH:(c + 1) * CH, 1024 + h * HG_DK:1024 + (h + 1) * HG_DK], prep[c]["kd"][:, hsl[h]])
             for h in range(HG_HEADS)] for c in range(NCH)]
    states = [[s_ref[h] for h in range(HG_HEADS)]]
    for c in range(NCH):
        states.append([states[c][h] * prep[c]["eg"][:, hsl[h]] + incs[c][h] for h in range(HG_HEADS)])
    for h in range(HG_HEADS):
        s_ref[h] = states[NCH][h]
    yield
    ams = [[[_mm_nt(prep[c]["subs"][i][0][:, hsl[h]], prep[c]["subs"][i][1][:, hsl[h]])
             * causal[i * SUB:(i + 1) * SUB, :(i + 1) * SUB]
             for h in range(HG_HEADS)] for i in range(nsub)] for c in range(NCH)]
    yield
    for c in range(NCH):
        r0 = c * CH
        for h in range(HG_HEADS):
            iv = p_ref[r0:r0 + CH, 1024 + h * HG_DK:1024 + (h + 1) * HG_DK]
            o = jnp.concatenate(
                [_mm(ams[c][i][h], iv[:(i + 1) * SUB]) + _mm_nt(prep[c]["subs"][i][2][:, hsl[h]], states[c][h])
                 for i in range(nsub)], axis=0)
            z = p_ref[r0:r0 + CH, 1536 + h * HG_DK:1536 + (h + 1) * HG_DK]
            y_ref[r0:r0 + CH, hsl[h]] = (_rms_norm(o) * nw * _silu(z)).astype(y_ref.dtype)
        yield


def _mix_merge_kernel(h_ref, qkv_ref, gz_ref, ba_ref, xbc_ref, zdt_ref, hg_ref, yd_ref,
                      gcw_ref, glane_ref, gnw_ref, scw_ref, scb_ref, svec_ref, eyet_ref, maskt_ref, bd_ref, ones_ref,
                      hvec_ref, hnw_ref, msk_ref, tri_ref, wg_ref, wb_ref, wo_ref, g_ref, b_ref,
                      o_ref, gxp_ref, gs_ref, sxp_ref, ss_ref, hs_ref, ya_ref, yb_ref, yc_ref, gate_ref, *, pad):
    t = pl.program_id(1)
    h = h_ref[...]
    hb = h.astype(BF)

    def gate_piece(j):
        cs = slice(j * GATE_CHUNK, (j + 1) * GATE_CHUNK)
        gate_ref[:, cs] = jnp.tanh(jnp.dot(hb, wg_ref[:, cs], preferred_element_type=F32)) + 1.0

    def branch_piece(br, y_ref):
        cs = slice(br * D_MODEL, (br + 1) * D_MODEL)
        gate_ref[:, cs] = gate_ref[:, cs] * jnp.dot(y_ref[...], wb_ref[br], preferred_element_type=F32)

    gdn = _gdn_steps(qkv_ref, gz_ref, ba_ref, gcw_ref, glane_ref, gnw_ref, msk_ref, tri_ref, ya_ref, gxp_ref, gs_ref)
    ssd = _ssd_steps(xbc_ref, zdt_ref, scw_ref, scb_ref, svec_ref, eyet_ref, maskt_ref, bd_ref, tri_ref, ones_ref,
                     yb_ref, sxp_ref, ss_ref, pad=pad)
    hgrn = _hgrn_steps(hg_ref, hvec_ref, hnw_ref, msk_ref, tri_ref, yc_ref, hs_ref)
    gates = iter([functools.partial(gate_piece, j) for j in range(N_BRANCH * D_MODEL // GATE_CHUNK)])
    emit = {"g": lambda: next(gdn), "s": lambda: next(ssd), "h": lambda: next(hgrn), "f": lambda: next(gates)(),
            "A": lambda: branch_piece(0, ya_ref), "B": lambda: branch_piece(1, yb_ref),
            "C": lambda: branch_piece(2, yc_ref), "D": lambda: branch_piece(3, yd_ref)}
    for tok in MIX_PLAN:
        emit[tok]()
    mixed = ((gate_ref[:, 0:D_MODEL] + gate_ref[:, D_MODEL:2 * D_MODEL])
             + (gate_ref[:, 2 * D_MODEL:3 * D_MODEL] + gate_ref[:, 3 * D_MODEL:4 * D_MODEL]))
    out = jnp.dot(mixed.astype(BF), wo_ref[...], preferred_element_type=F32)
    r = ALPHA * h + out
    mu = jnp.mean(r, axis=-1, keepdims=True)
    rc = r - mu
    var = jnp.mean(rc * rc, axis=-1, keepdims=True)
    y = rc * lax.rsqrt(var + LN_EPS) * g_ref[...] + b_ref[...]
    row = t * TILE + lax.broadcasted_iota(jnp.int32, (TILE, 1), 0)
    o_ref[...] = jnp.where(row >= pad, y, 0.0)


def _mix_merge_call(h, p, yd, gdn_prm, ssd_prm, hg_prm, msk, tri, wg, wb, wo, g, b, pad, drop_first_tile):
    bsz, lp, d = h.shape
    tok = lambda w, j=0: pl.BlockSpec((None, TILE, w), lambda b_, t: (b_, t, j))
    return pl.pallas_call(
        functools.partial(_mix_merge_kernel, pad=pad),
        grid=(bsz, lp // TILE),
        in_specs=[tok(d), tok(1536, OFF_QKV // 1536), tok(512, OFF_GZ // 512), tok(128, OFF_BA // 128),
                  tok(1024, OFF_XBC // 1024), tok(1024, OFF_MZ // 1024), tok(2048, OFF_HG // 2048), tok(BRANCH_W),
                  _full((CONV_K, 1536)), _full((8, 128)), _full((1, GDN_DK)),
                  _full((CONV_K, 1024)), _full((1, 1024)), _full((8, 512)),
                  _full((CH, 256)), _full((CH, 256)), _full((256, 256)), _full((CH, CH)),
                  _full((8, 512)), _full((1, HG_DK)), _full((9, CH, CH)), _full((CH, CH)),
                  _full((d, N_BRANCH * d)), _full((N_BRANCH, BRANCH_W, d)), _full((d, d)),
                  _full((1, d)), _full((1, d))],
        out_specs=(pl.BlockSpec((None, TILE, d), lambda b_, t: (b_, jnp.maximum(t - 1, 0), 0))
                   if drop_first_tile else tok(d)),
        out_shape=jax.ShapeDtypeStruct((bsz, lp - TILE if drop_first_tile else lp, d), F32),
        scratch_shapes=[pltpu.VMEM((TILE + 8, 1536), F32), pltpu.VMEM((GDN_HEADS, GDN_DK, GDN_DK), F32),
                        pltpu.VMEM((TILE + 8, 1024), F32), pltpu.VMEM((M2_GROUPS, M2_DSTATE, 256), F32),
                        pltpu.VMEM((HG_HEADS, HG_DK, HG_DK), F32),
                        pltpu.VMEM((TILE, BRANCH_W), BF), pltpu.VMEM((TILE, BRANCH_W), BF),
                        pltpu.VMEM((TILE, BRANCH_W), BF), pltpu.VMEM((TILE, N_BRANCH * D_MODEL), F32)],
        compiler_params=_cparams(),
        name="mix_merge",
    )(h, p, p, p, p, p, p, yd, *gdn_prm, *ssd_prm, *hg_prm, msk, tri, wg, wb, wo, g.reshape(1, d), b.reshape(1, d))


def _split_w_in(w):
    sizes = (1536, 512, 4, 4, 1024, 512, 8, 512, 512, 512, 512, 512, 512, 4096)
    offs = [0]
    for s in sizes:
        offs.append(offs[-1] + s)
    return [w[..., offs[i]:offs[i + 1]] for i in range(len(sizes))]


def _prep_w1(w):
    (qkv, gz, gb, ga, xbc, mz, mdt, hq, hf, hi, hz, su, sz, gates) = _split_w_in(w)
    ba = jnp.concatenate([gb, ga, jnp.zeros(w.shape[:-1] + (120,), w.dtype)], axis=-1)
    mdt_e = jnp.repeat(mdt, M2_HEADDIM, axis=-1)
    w1 = jnp.concatenate([qkv, gz, xbc, mz, mdt_e, hq, hf, hi, hz, ba, su, sz], axis=-1)
    return w1.astype(BF), (0.5 * gates).astype(BF)


def _lane_row(vals, off, width=128):
    return jnp.zeros((width,), F32).at[off:off + vals.shape[0]].set(vals.astype(F32))


def _prep_s5(a_re, a_im, b_re, b_im, c_re, c_im, log_dt):
    f = lambda v: v.astype(F32)
    a_re, a_im, b_re, b_im, c_re, c_im = map(f, (a_re, a_im, b_re, b_im, c_re, c_im))
    dt = jnp.exp(f(log_dt))[:, None]

    def lam_pow(k):
        mag = jnp.exp(a_re * dt * k)
        return mag * jnp.cos(a_im * dt * k), mag * jnp.sin(a_im * dt * k)

    lam_re, lam_im = lam_pow(1.0)
    den = jnp.square(a_re) + jnp.square(a_im)
    nr, ni = lam_re - 1.0, lam_im
    z_re, z_im = (nr * a_re + ni * a_im) / den, (ni * a_re - nr * a_im) / den
    bb_re = z_re[..., None] * b_re - z_im[..., None] * b_im
    bb_im = z_re[..., None] * b_im + z_im[..., None] * b_re
    ks = jnp.arange(S5_CK, dtype=F32)[:, None, None]
    pr, pi = lam_pow(ks)
    p1r, p1i = lam_pow(ks + 1.0)
    mr, mi = lam_pow(S5_CK * (ks + 1.0))

    def c_times(qr, qi):
        return (c_re[None] * qr[:, :, None, :] - c_im[None] * qi[:, :, None, :],
                c_re[None] * qi[:, :, None, :] + c_im[None] * qr[:, :, None, :])

    cl_re, cl_im = c_times(pr, pi)
    kd = (jnp.einsum('dgcp,gpe->dgce', cl_re, bb_re) - jnp.einsum('dgcp,gpe->dgce', cl_im, bb_im))
    s_i = jnp.arange(S5_CK)[:, None]
    t_i = jnp.arange(S5_CK)[None, :]
    kt = jnp.where((t_i >= s_i)[:, :, None, None, None], kd[jnp.clip(t_i - s_i, 0, S5_CK - 1)], 0.0)
    toep = jnp.transpose(kt, (2, 0, 4, 1, 3)).reshape(S5_NG, 128, 128)
    rr, ri = pr[::-1], pi[::-1]
    w_re = rr[..., None] * bb_re[None] - ri[..., None] * bb_im[None]
    w_im = rr[..., None] * bb_im[None] + ri[..., None] * bb_re[None]
    to_rows = lambda w: jnp.transpose(w, (1, 0, 3, 2)).reshape(S5_NG, 128, S5_P)
    m1 = jnp.concatenate([toep, to_rows(w_re), to_rows(w_im)], axis=2).astype(BF)
    q_re, q_im = c_times(p1r, p1i)
    to_cols = lambda v: jnp.transpose(v, (1, 3, 0, 2)).reshape(S5_NG, S5_P, 128)
    q = jnp.concatenate([to_cols(q_re), -to_cols(q_im)], axis=1).astype(BF)
    pack1 = lambda v: jnp.concatenate([v, v], axis=-1).reshape(v.shape[0], S5_LANES)
    pack2 = lambda v: jnp.concatenate([-v, v], axis=-1).reshape(v.shape[0], S5_LANES)
    row = jnp.arange(8)[:, None]
    tabs = [pack1(mr), pack2(mi)]
    for dd in (1, 2, 4):
        tabs.append(jnp.where(row >= dd, pack1(mr[dd - 1:dd]), 0.0))
        tabs.append(jnp.where(row >= dd, pack2(mi[dd - 1:dd]), 0.0))
    lane_piece = jnp.arange(128) // 16
    lm = jnp.stack([(lane_piece & dd) != 0 for dd in (4, 2, 1)] + [jnp.zeros((128,), bool)] * 5).astype(F32)
    return m1, q, jnp.concatenate(tabs, axis=0), lm


@jax.jit
def kernel(x, meta_tokens, ln_in_g, ln_in_b, w_in, gdn_conv_w, gdn_A_log, gdn_dt_bias, gdn_norm_w, m2_conv_w, m2_conv_b, m2_dt_bias, m2_A_log, m2_D, m2_norm_w, hg_lb_logits, hg_norm_w, s5_A_re, s5_A_im, s5_B_re, s5_B_im, s5_C_re, s5_C_im, s5_D, s5_log_dt, s5_glu_w1, s5_glu_w2, w_branch, w_out, ln_g, ln_b):
    bsz, seq, d = x.shape
    ltot = N_META + seq
    lp = -(-ltot // TILE) * TILE
    pad = lp - ltot
    assert pad + N_META == TILE, "real tokens must start on the second tile"
    meta_tile = jnp.concatenate([jnp.zeros((pad, d), F32), meta_tokens.astype(F32)], axis=0)
    h = _ln_in_call(x.astype(F32), meta_tile, ln_in_g.astype(F32), ln_in_b.astype(F32), pad)

    msk = _masks64()
    tri = msk[0].astype(BF)
    ones = jnp.ones((CH, CH), BF)
    jj = jnp.arange(256)
    eyet = (jnp.arange(CH)[:, None] == (jj % CH)[None, :]).astype(F32)
    maskt = (jnp.arange(CH)[:, None] >= (jj % CH)[None, :]).astype(F32)
    bd = ((jj // CH)[:, None] == (jj // CH)[None, :]).astype(F32)

    cum = jnp.cumsum(jax.nn.softmax(hg_lb_logits.astype(F32), axis=0), axis=0)
    lower_bounds = cum - cum[0:1]

    w1_all, wg_all = _prep_w1(w_in)
    m1_all, q_all, pw_all, lm_all = jax.vmap(_prep_s5)(s5_A_re, s5_A_im, s5_B_re, s5_B_im, s5_C_re, s5_C_im,
                                                       s5_log_dt)
    glu1_all, glu2_all = s5_glu_w1.astype(BF), s5_glu_w2.astype(BF)
    wb_all, wo_all = (0.5 * w_branch).astype(BF), w_out.astype(BF)

    for l in range(w_in.shape[0]):
        wg = wg_all[l]
        p, yd = _proj_s5_call(h, w1_all[l], m1_all[l], q_all[l], pw_all[l], lm_all[l],
                              s5_D[l].astype(F32).reshape(1, -1), glu1_all[l], glu2_all[l])

        lane = jnp.stack([_lane_row(jnp.exp(gdn_A_log[l].astype(F32)), 4), _lane_row(gdn_dt_bias[l], 4)]
                         + [jnp.zeros((128,), F32)] * 6)
        gdn_prm = (gdn_conv_w[l].astype(F32), lane, gdn_norm_w[l].astype(F32).reshape(1, GDN_DK))

        rep = lambda v: jnp.repeat(v.astype(F32), M2_HEADDIM)
        vec = jnp.stack([rep(m2_dt_bias[l]), rep(-jnp.exp(m2_A_log[l].astype(F32))), rep(m2_D[l]),
                         m2_norm_w[l].astype(F32)] + [jnp.zeros((BRANCH_W,), F32)] * 4)
        ssd_prm = (m2_conv_w[l].astype(F32), m2_conv_b[l].astype(F32).reshape(1, -1), vec, eyet, maskt, bd, ones)

        lb = lower_bounds[l]
        hvec = jnp.stack([jnp.log(lb), jnp.log1p(-lb), 1.0 - lb] + [jnp.zeros((BRANCH_W,), F32)] * 5)
        hg_prm = (hvec, hg_norm_w[l].astype(F32).reshape(1, HG_DK))

        h = _mix_merge_call(h, p, yd, gdn_prm, ssd_prm, hg_prm, msk, tri, wg, wb_all[l],
                            wo_all[l], ln_g[l].astype(F32), ln_b[l].astype(F32), pad,
                            drop_first_tile=(l == w_in.shape[0] - 1))
    return h.astype(x.dtype)
```

```python
import functools
import itertools
import math

import jax
import jax.numpy as jnp
from jax import lax
from jax.experimental import pallas as pl
from jax.experimental.pallas import tpu as pltpu

F32 = jnp.float32
BF = jnp.bfloat16

D_MODEL = 1024
DEPTH = 4
N_META = 16
CONV_K = 4
N_BRANCH = 4
BRANCH_W = 512
GDN_HEADS = 4
GDN_DK = 128
M2_HEADS = 8
M2_HEADDIM = 64
M2_GROUPS = 2
M2_DSTATE = 128
HG_HEADS = 4
HG_DK = 128
S5_GROUP = 16
S5_NG = 32
S5_P = 64
S5_STATE = S5_NG * S5_P
ALPHA = (2 * DEPTH) ** 0.25
LN_EPS = 1e-5
RMS_EPS = 1e-6

TILE = 256
CH = 64
NCH = TILE // CH
SUB = 16
HG_EXP_CLAMP = 80.0
VMEM_LIMIT = 56 * 1024 * 1024

OFF_QKV, OFF_GZ, OFF_XBC, OFF_MZ, OFF_MDT = 0, 1536, 2048, 3072, 3584
OFF_HG, OFF_BA, OFF_S5 = 4096, 6144, 6272
NP = 6272
N1 = 7296
W_PIECES = ((0, 2048), (OFF_XBC, 1024), (OFF_MZ, 512), (OFF_MDT, 512), (OFF_HG, 2048), (OFF_BA, 128), (OFF_S5, 1024))


def _sigmoid(x):
    return 0.5 * (jnp.tanh(0.5 * x) + 1.0)


def _silu(x):
    return x * _sigmoid(x)


def _softplus(x):
    return jnp.maximum(x, 0.0) + jnp.log1p(jnp.exp(-jnp.abs(x)))


def _mm(a, b):
    return jnp.dot(a.astype(BF), b.astype(BF), preferred_element_type=F32)


def _mm_nt(a, b):
    return lax.dot_general(a.astype(BF), b.astype(BF), (((1,), (1,)), ((), ())),
                           preferred_element_type=F32)


def _mm_tn(a, b):
    return lax.dot_general(a.astype(BF), b.astype(BF), (((0,), (0,)), ((), ())),
                           preferred_element_type=F32)


def _mm3_left(m_bf, x):
    hi = x.astype(BF)
    r1 = x - hi.astype(F32)
    mid = r1.astype(BF)
    lo = (r1 - mid.astype(F32)).astype(BF)
    d = lambda v: jnp.dot(m_bf, v, preferred_element_type=F32)
    return (d(lo) + d(mid)) + d(hi)


def _rms_norm(x):
    return x * lax.rsqrt(jnp.mean(x * x, axis=-1, keepdims=True) + RMS_EPS)


def _masks64():
    i = jnp.arange(CH)[:, None]
    j = jnp.arange(CH)[None, :]
    ms = [(i >= j), (i > j)]
    for s in (1, 2, 4, 8, 16, 32):
        ms.append((i // (2 * s) == j // (2 * s)) & (i % (2 * s) >= s) & (j % (2 * s) < s))
    ms.append(i == j)
    return jnp.stack(ms).astype(F32)


def _cparams():
    return pltpu.CompilerParams(dimension_semantics=("arbitrary", "arbitrary"),
                                vmem_limit_bytes=VMEM_LIMIT)


def _full(shape):
    n = len(shape)
    return pl.BlockSpec(shape, lambda b, t: (0,) * n, pipeline_mode=pl.Buffered(1))


def _layer_full(shape, l):
    n = len(shape)
    return pl.BlockSpec((None,) + tuple(shape), lambda b, t: (l,) + (0,) * n, pipeline_mode=pl.Buffered(1))


def _ln_in_kernel(x_ref, m_ref, g_ref, b_ref, o_ref, *, pad):
    t = pl.program_id(1)
    x = jnp.where(t == 0, m_ref[...], x_ref[...])
    mu = jnp.mean(x, axis=-1, keepdims=True)
    xc = x - mu
    var = jnp.mean(xc * xc, axis=-1, keepdims=True)
    y = xc * lax.rsqrt(var + LN_EPS) * g_ref[...] + b_ref[...]
    row = t * TILE + lax.broadcasted_iota(jnp.int32, (TILE, 1), 0)
    o_ref[...] = jnp.where(row >= pad, y, 0.0)


def _ln_in_call(x, meta_tile, g, b, pad):
    bsz, seq, d = x.shape
    return pl.pallas_call(
        functools.partial(_ln_in_kernel, pad=pad),
        grid=(bsz, seq // TILE + 1),
        in_specs=[pl.BlockSpec((None, TILE, d), lambda b_, t: (b_, jnp.maximum(t - 1, 0), 0)),
                  _full((TILE, d)), _full((1, d)), _full((1, d))],
        out_specs=pl.BlockSpec((None, TILE, d), lambda b_, t: (b_, t, 0)),
        out_shape=jax.ShapeDtypeStruct((bsz, seq + TILE, d), F32),
        compiler_params=_cparams(),
        name="ln_in",
    )(x, meta_tile, g.reshape(1, d), b.reshape(1, d))


S5_CK = 8
S5_NCK = TILE // S5_CK
S5_PACK = 2 * S5_P
S5_LANES = S5_NG * S5_PACK
PROJ_CHUNK = 512
GATE_CHUNK = 256
MIX_PLAN = "gsh" + "gfgfgfgfh" * 4 + "gs" * 8 + "gg" + "hDhA" + "hhBhh" + "C"


def _transpose_pieces(arrs, lm_ref):
    for k, d in enumerate((4, 2, 1)):
        m = lm_ref[k:k + 1, :] > 0
        new = list(arrs)
        for s_ in range(8):
            if s_ & d == 0:
                lo, hi = arrs[s_], arrs[s_ + d]
                new[s_] = jnp.where(m, pltpu.roll(hi, 16 * d, 1), lo)
                new[s_ + d] = jnp.where(m, hi, pltpu.roll(lo, 128 - 16 * d, 1))
        arrs = new
    return arrs


def _s5_scan_block(b, cs, carry, tabs, st_ref):
    rows = slice(8 * b, 8 * b + 8)
    v = st_ref[rows, cs]
    for n, d in enumerate((1, 2, 4)):
        vsh = pltpu.roll(v, d, 0)
        v = v + (tabs[2 + 2 * n] * vsh + tabs[3 + 2 * n] * pltpu.roll(vsh, S5_P, 1))
    cb = jnp.broadcast_to(carry, (8, S5_PACK))
    v = v + (tabs[0] * cb + tabs[1] * pltpu.roll(cb, S5_P, 1))
    st_ref[rows, cs] = v
    return v[7:8, :]


def _proj_s5_kernel(h_ref, w0_ref, w1_ref, w2_ref, w3_ref, w4_ref, w5_ref, w6_ref,
                    m1_ref, q_ref, pw_ref, lm_ref, d_ref, g1_ref, g2_ref,
                    p_ref, yd_ref, u_ref, ug_ref, yq_ref, st_ref, y_ref, c_ref):
    t = pl.program_id(1)

    @pl.when(t == 0)
    def _():
        c_ref[...] = jnp.zeros_like(c_ref)

    def wcols(j, w):
        for (start, width), ref in zip(W_PIECES, (w0_ref, w1_ref, w2_ref, w3_ref, w4_ref, w5_ref, w6_ref)):
            if start <= j and j + w <= start + width:
                return ref[:, j - start:j - start + w]
        raise ValueError((j, w))

    hb = h_ref[...].astype(BF)
    u = jnp.dot(hb, wcols(OFF_S5, BRANCH_W), preferred_element_type=F32)
    for j in range(4):
        u_ref[j] = u[:, j * 128:(j + 1) * 128]

    starts = iter(range(0, NP, PROJ_CHUNK))

    def proj_chunks(n):
        for j in itertools.islice(starts, n):
            w = min(PROJ_CHUNK, NP - j)
            p_ref[:, j:j + w] = jnp.dot(hb, wcols(j, w), preferred_element_type=F32)

    for j in range(4):
        proj_chunks(1)
        arrs = [u_ref[j, pl.ds(s_, S5_NCK, stride=S5_CK), :] for s_ in range(S5_CK)]
        arrs = _transpose_pieces(arrs, lm_ref)
        for i in range(8):
            g = 8 * j + i
            ug_ref[:, g * 128:(g + 1) * 128] = arrs[i].astype(BF)
    row0 = lax.broadcasted_iota(jnp.int32, (S5_NCK, S5_PACK), 0) == 0
    last_inc = []
    for g in range(S5_NG):
        if g % 11 == 10:
            proj_chunks(1)
        cs = slice(g * S5_PACK, (g + 1) * S5_PACK)
        tw = jnp.dot(ug_ref[:, cs], m1_ref[g], preferred_element_type=F32)
        yq_ref[:, cs] = tw[:, :S5_PACK]
        inc = tw[:, S5_PACK:]
        last_inc.append(inc[S5_NCK - 1:S5_NCK, :])
        st_ref[:, cs] = jnp.where(row0, c_ref[:, cs], pltpu.roll(inc, 1, 0))

    for g in range(S5_NG):
        if g % 8 == 0:
            proj_chunks(1)
        cs = slice(g * S5_PACK, (g + 1) * S5_PACK)
        tabs = [pw_ref[8 * k:8 * k + 8, cs] for k in range(8)]
        x = jnp.zeros((1, S5_PACK), F32)
        for b in range(S5_NCK // 8):
            x = _s5_scan_block(b, cs, x, tabs, st_ref)
        c_ref[:, cs] = last_inc[g] + (tabs[0][0:1] * x + tabs[1][0:1] * pltpu.roll(
            jnp.broadcast_to(x, (8, S5_PACK)), S5_P, 1)[0:1])

    for g in range(S5_NG):
        if g % 16 == 0:
            proj_chunks(1)
        cs = slice(g * S5_PACK, (g + 1) * S5_PACK)
        yq_ref[:, cs] = yq_ref[:, cs] + _mm(st_ref[:, cs], q_ref[g])
    proj_chunks(NP)
    for j in range(4):
        arrs = _transpose_pieces([yq_ref[:, (8 * j + i) * 128:(8 * j + i + 1) * 128] for i in range(8)], lm_ref)
        for s_ in range(S5_CK):
            y_ref[j, pl.ds(s_, S5_NCK, stride=S5_CK), :] = arrs[s_]
    y = jnp.concatenate([y_ref[j] for j in range(4)], axis=1) + d_ref[...] * u
    y = 0.5 * y * (1.0 + jnp.tanh(math.sqrt(2.0 / math.pi) * (y + 0.044715 * (y * y * y))))
    yb = y.astype(BF)
    glu = jnp.dot(yb, g1_ref[...], preferred_element_type=F32) * _sigmoid(
        jnp.dot(yb, g2_ref[...], preferred_element_type=F32))
    z = jnp.dot(hb, wcols(OFF_S5 + BRANCH_W, BRANCH_W), preferred_element_type=F32)
    yd_ref[...] = (glu * _silu(z)).astype(yd_ref.dtype)


def _proj_s5_call(h, l, w_pieces, m1, q, pw, lm, dvec, g1, g2):
    bsz, lp, d = h.shape
    return pl.pallas_call(
        _proj_s5_kernel,
        grid=(bsz, lp // TILE),
        in_specs=[pl.BlockSpec((None, TILE, d), lambda b_, t: (b_, t, 0))]
        + [_layer_full((d, width), l) for _, width in W_PIECES]
        + [_layer_full((S5_NG, 128, 256), l), _layer_full((S5_NG, 128, 128), l), _layer_full((64, S5_LANES), l),
           _layer_full((8, 128), l), _layer_full((1, 512), l), _layer_full((512, 512), l),
           _layer_full((512, 512), l)],
        out_specs=[pl.BlockSpec((None, TILE, NP), lambda b_, t: (b_, t, 0)),
                   pl.BlockSpec((None, TILE, BRANCH_W), lambda b_, t: (b_, t, 0))],
        out_shape=[jax.ShapeDtypeStruct((bsz, lp, NP), F32), jax.ShapeDtypeStruct((bsz, lp, BRANCH_W), BF)],
        scratch_shapes=[pltpu.VMEM((4, TILE, 128), F32), pltpu.VMEM((S5_NCK, S5_LANES), BF),
                        pltpu.VMEM((S5_NCK, S5_LANES), F32), pltpu.VMEM((S5_NCK, S5_LANES), F32),
                        pltpu.VMEM((4, TILE, 128), F32), pltpu.VMEM((1, S5_LANES), F32)],
        compiler_params=_cparams(),
        name="proj_s5",
    )(h, *w_pieces, m1, q, pw, lm, dvec, g1, g2)


def _conv_chunk(xp_ref, cw, r0):
    x = xp_ref[r0 + 8:r0 + 8 + CH, :] * cw[CONV_K - 1:CONV_K, :]
    for j in range(CONV_K - 1):
        s = r0 + 8 - (CONV_K - 1) + j
        x = x + xp_ref[s:s + CH, :] * cw[j:j + 1, :]
    return x


def _gdn_steps(qkv_ref, z_ref, ba_ref, cw_ref, lane_ref, nw_ref, msk_ref, tri_ref, y_ref, xp_ref, s_ref):
    t = pl.program_id(1)

    @pl.when(t == 0)
    def _():
        xp_ref[0:8, :] = jnp.zeros((8, 3 * BRANCH_W), F32)
        s_ref[...] = jnp.zeros_like(s_ref)

    xp_ref[8:8 + TILE, :] = qkv_ref[...]
    yield
    cw = cw_ref[...]
    exp_a = lane_ref[0:1, :]
    dt_bias = lane_ref[1:2, :]
    nw = nw_ref[...]
    causal = msk_ref[0]
    strict = msk_ref[1]
    tri = tri_ref[...]
    items = []
    a_list = []
    for c in range(NCH):
        r0 = c * CH
        qkv = _silu(_conv_chunk(xp_ref, cw, r0))
        ba = ba_ref[r0:r0 + CH, :]
        beta_all = _sigmoid(ba)
        gl = -(exp_a * _softplus(ba + dt_bias))
        gc = _mm3_left(tri, gl)
        gc_t = gc.T
        for h in range(GDN_HEADS):
            sl = slice(h * GDN_DK, (h + 1) * GDN_DK)
            q = qkv[:, sl]
            k = qkv[:, BRANCH_W + h * GDN_DK:BRANCH_W + (h + 1) * GDN_DK]
            v = qkv[:, 2 * BRANCH_W + h * GDN_DK:2 * BRANCH_W + (h + 1) * GDN_DK]
            q = q * (lax.rsqrt(jnp.sum(q * q, axis=-1, keepdims=True) + RMS_EPS) * GDN_DK ** -0.5)
            k = k * lax.rsqrt(jnp.sum(k * k, axis=-1, keepdims=True) + RMS_EPS)
            beta = beta_all[:, h:h + 1]
            gcol = gc[:, 4 + h:5 + h]
            grow = gc_t[4 + h:5 + h, :]
            decay = jnp.exp(jnp.where(causal > 0, gcol - grow, -jnp.inf))
            kb = k * beta
            eg = jnp.exp(gcol)
            glast = gc[CH - 1:CH, 4 + h:5 + h]
            a_list.append(strict * (_mm_nt(kb, k) * decay))
            items.append(dict(
                rhs=jnp.concatenate([v * beta, kb * eg], axis=1),
                qk=causal * (_mm_nt(q, k) * decay),
                qd=q * eg,
                kd=k * jnp.exp(glast - gcol),
                gtot=jnp.exp(glast)))
            yield
    tinvs = [msk_ref[8] - a * msk_ref[2] for a in a_list]
    for lvl in range(1, 6):
        m = msk_ref[2 + lvl]
        t1 = [_mm(x, a * m) for x, a in zip(tinvs, a_list)]
        tinvs = [x - _mm(t_, x) for x, t_ in zip(tinvs, t1)]
        yield
    uws = [_mm(tinv, it["rhs"]) for tinv, it in zip(tinvs, items)]
    kd_uw = [_mm_tn(it["kd"], uw) for it, uw in zip(items, uws)]
    qk_uw = [_mm(it["qk"], uw) for it, uw in zip(items, uws)]
    yield
    states = [s_ref[h] for h in range(GDN_HEADS)]
    for c in range(NCH):
        r0 = c * CH
        for h in range(GDN_HEADS):
            sl = slice(h * GDN_DK, (h + 1) * GDN_DK)
            n = c * GDN_HEADS + h
            s = states[h]
            o = _mm(items[n]["qd"] - qk_uw[n][:, GDN_DK:], s) + qk_uw[n][:, :GDN_DK]
            states[h] = (s * items[n]["gtot"] - _mm(kd_uw[n][:, GDN_DK:], s)) + kd_uw[n][:, :GDN_DK]
            y = _rms_norm(o) * nw * _silu(z_ref[r0:r0 + CH, sl])
            y_ref[r0:r0 + CH, sl] = y.astype(y_ref.dtype)
        if c == NCH - 1:
            for h in range(GDN_HEADS):
                s_ref[h] = states[h]
            xp_ref[0:8, :] = xp_ref[TILE:TILE + 8, :]
        yield


def _ssd_steps(xbc_ref, zdt_ref, cw_ref, cb_ref, vec_ref, eyet_ref, maskt_ref, bd_ref, tri_ref, ones_ref,
               y_ref, xp_ref, s_ref, *, pad):
    t = pl.program_id(1)

    @pl.when(t == 0)
    def _():
        xp_ref[0:8, :] = jnp.zeros((8, 1024), F32)
        s_ref[...] = jnp.zeros_like(s_ref)

    xp_ref[8:8 + TILE, :] = xbc_ref[...]
    yield
    cw = cw_ref[...]
    cb = cb_ref[...]
    dt_bias = vec_ref[0:1, :]
    neg_a = vec_ref[1:2, :]
    d_skip = vec_ref[2:3, :]
    nw = vec_ref[3:4, :]
    tri = tri_ref[...]
    ones = ones_ref[...]
    gw = M2_HEADDIM * (M2_HEADS // M2_GROUPS)
    for c in range(NCH):
        r0 = c * CH
        xbc = _silu(_conv_chunk(xp_ref, cw, r0) + cb)
        xs = xbc[:, :BRANCH_W]
        row = t * TILE + r0 + lax.broadcasted_iota(jnp.int32, (CH, 1), 0)
        valid = (row >= pad).astype(F32)
        dt = _softplus(zdt_ref[r0:r0 + CH, BRANCH_W:] + dt_bias) * valid
        a = dt * neg_a
        acum = _mm3_left(tri, a)
        xdt = xs * dt
        z = zdt_ref[r0:r0 + CH, :BRANCH_W]
        for g in range(M2_GROUPS):
            gs = slice(g * gw, (g + 1) * gw)
            bg = xbc[:, BRANCH_W + g * M2_DSTATE:BRANCH_W + (g + 1) * M2_DSTATE]
            cg = xbc[:, BRANCH_W + M2_GROUPS * M2_DSTATE + g * M2_DSTATE:
                     BRANCH_W + M2_GROUPS * M2_DSTATE + (g + 1) * M2_DSTATE]
            ag = acum[:, gs]
            cbt = _mm_nt(cg, jnp.concatenate([bg] * 4, axis=0))
            arow = _mm3_left(ones, ag * eyet_ref[...])
            lm = jnp.exp(jnp.where(maskt_ref[...] > 0, ag - arow, -jnp.inf))
            xg = xdt[:, gs]
            xbd = jnp.concatenate([xg] * 4, axis=0) * bd_ref[...]
            y_diag = _mm(cbt * lm, xbd)
            alast = ag[CH - 1:CH, :]
            s = s_ref[g]
            y_off = _mm(cg, s) * jnp.exp(ag)
            s_ref[g] = s * jnp.exp(alast) + _mm_tn(bg, xg * jnp.exp(alast - ag))
            y = (y_diag + y_off + d_skip[:, gs] * xs[:, gs]) * _silu(z[:, gs])
            y = _rms_norm(y) * nw[:, gs]
            y_ref[r0:r0 + CH, gs] = y.astype(y_ref.dtype)
            if c == NCH - 1 and g == M2_GROUPS - 1:
                xp_ref[0:8, :] = xp_ref[TILE:TILE + 8, :]
            yield


def _hgrn_steps(p_ref, vec_ref, nw_ref, msk_ref, tri_ref, y_ref, s_ref):
    t = pl.program_id(1)

    @pl.when(t == 0)
    def _():
        s_ref[...] = jnp.zeros_like(s_ref)

    yield
    log_lb = vec_ref[0:1, :]
    log1m_lb = vec_ref[1:2, :]
    one_m_lb = vec_ref[2:3, :]
    nw = nw_ref[...]
    causal = msk_ref[0]
    tri = tri_ref[...]
    hsl = [slice(h * HG_DK, (h + 1) * HG_DK) for h in range(HG_HEADS)]
    nsub = CH // SUB
    prep = []
    for c in range(NCH):
        r0 = c * CH
        q = _silu(p_ref[r0:r0 + CH, 0:512])
        zf = p_ref[r0:r0 + CH, 512:1024]
        e = jnp.exp(-jnp.abs(zf))
        lsig = jnp.minimum(zf, 0.0) - jnp.log1p(e)
        bb = log1m_lb + lsig
        logf = jnp.maximum(log_lb, bb) + jnp.log1p(jnp.exp(-jnp.abs(log_lb - bb)))
        sig_neg = jnp.where(zf >= 0, e, 1.0) / (1.0 + e)
        k = one_m_lb * sig_neg
        gcum = _mm3_left(tri, logf)
        glast = gcum[CH - 1:CH, :]
        subs = []
        for i in range(nsub):
            rs = slice(i * SUB, (i + 1) * SUB)
            n = (i + 1) * SUB
            gref = gcum[i * SUB - 1:i * SUB, :] if i > 0 else jnp.zeros((1, BRANCH_W), F32)
            qt = q[rs] * jnp.exp(gcum[rs] - gref)
            kt = k[:n] * jnp.exp(jnp.minimum(gref - gcum[:n], HG_EXP_CLAMP))
            subs.append((qt, kt, qt * jnp.exp(gref)))
        prep.append(dict(kd=k * jnp.exp(glast - gcum), eg=jnp.exp(glast), subs=subs))
        yield
    incs = [[_mm_tn(p_ref[c * CH:(c + 1) * CH, 1024 + h * HG_DK:1024 + (h + 1) * HG_DK], prep[c]["kd"][:, hsl[h]])
             for h in range(HG_HEADS)] for c in range(NCH)]
    states = [[s_ref[h] for h in range(HG_HEADS)]]
    for c in range(NCH):
        states.append([states[c][h] * prep[c]["eg"][:, hsl[h]] + incs[c][h] for h in range(HG_HEADS)])
    for h in range(HG_HEADS):
        s_ref[h] = states[NCH][h]
    yield
    ams = [[[_mm_nt(prep[c]["subs"][i][0][:, hsl[h]], prep[c]["subs"][i][1][:, hsl[h]])
             * causal[i * SUB:(i + 1) * SUB, :(i + 1) * SUB]
             for h in range(HG_HEADS)] for i in range(nsub)] for c in range(NCH)]
    yield
    for c in range(NCH):
        r0 = c * CH
        for h in range(HG_HEADS):
            iv = p_ref[r0:r0 + CH, 1024 + h * HG_DK:1024 + (h + 1) * HG_DK]
            o = jnp.concatenate(
                [_mm(ams[c][i][h], iv[:(i + 1) * SUB]) + _mm_nt(prep[c]["subs"][i][2][:, hsl[h]], states[c][h])
                 for i in range(nsub)], axis=0)
            z = p_ref[r0:r0 + CH, 1536 + h * HG_DK:1536 + (h + 1) * HG_DK]
            y_ref[r0:r0 + CH, hsl[h]] = (_rms_norm(o) * nw * _silu(z)).astype(y_ref.dtype)
        yield


def _mix_merge_kernel(h_ref, qkv_ref, gz_ref, ba_ref, xbc_ref, zdt_ref, hg_ref, yd_ref,
                      gcw_ref, glane_ref, gnw_ref, scw_ref, scb_ref, svec_ref, eyet_ref, maskt_ref, bd_ref, ones_ref,
                      hvec_ref, hnw_ref, msk_ref, tri_ref, wg_ref, wb_ref, wo_ref, g_ref, b_ref,
                      o_ref, gxp_ref, gs_ref, sxp_ref, ss_ref, hs_ref, ya_ref, yb_ref, yc_ref, gate_ref, *, pad):
    t = pl.program_id(1)
    h = h_ref[...]
    hb = h.astype(BF)

    def gate_piece(j):
        cs = slice(j * GATE_CHUNK, (j + 1) * GATE_CHUNK)
        gate_ref[:, cs] = jnp.tanh(jnp.dot(hb, wg_ref[:, cs], preferred_element_type=F32)) + 1.0

    def branch_piece(br, y_ref):
        cs = slice(br * D_MODEL, (br + 1) * D_MODEL)
        gate_ref[:, cs] = gate_ref[:, cs] * jnp.dot(y_ref[...], wb_ref[br], preferred_element_type=F32)

    gdn = _gdn_steps(qkv_ref, gz_ref, ba_ref, gcw_ref, glane_ref, gnw_ref, msk_ref, tri_ref, ya_ref, gxp_ref, gs_ref)
    ssd = _ssd_steps(xbc_ref, zdt_ref, scw_ref, scb_ref, svec_ref, eyet_ref, maskt_ref, bd_ref, tri_ref, ones_ref,
                     yb_ref, sxp_ref, ss_ref, pad=pad)
    hgrn = _hgrn_steps(hg_ref, hvec_ref, hnw_ref, msk_ref, tri_ref, yc_ref, hs_ref)
    gates = iter([functools.partial(gate_piece, j) for j in range(N_BRANCH * D_MODEL // GATE_CHUNK)])
    emit = {"g": lambda: next(gdn), "s": lambda: next(ssd), "h": lambda: next(hgrn), "f": lambda: next(gates)(),
            "A": lambda: branch_piece(0, ya_ref), "B": lambda: branch_piece(1, yb_ref),
            "C": lambda: branch_piece(2, yc_ref), "D": lambda: branch_piece(3, yd_ref)}
    for tok in MIX_PLAN:
        emit[tok]()
    mixed = ((gate_ref[:, 0:D_MODEL] + gate_ref[:, D_MODEL:2 * D_MODEL])
             + (gate_ref[:, 2 * D_MODEL:3 * D_MODEL] + gate_ref[:, 3 * D_MODEL:4 * D_MODEL]))
    out = jnp.dot(mixed.astype(BF), wo_ref[...], preferred_element_type=F32)
    r = ALPHA * h + out
    mu = jnp.mean(r, axis=-1, keepdims=True)
    rc = r - mu
    var = jnp.mean(rc * rc, axis=-1, keepdims=True)
    y = rc * lax.rsqrt(var + LN_EPS) * g_ref[...] + b_ref[...]
    row = t * TILE + lax.broadcasted_iota(jnp.int32, (TILE, 1), 0)
    o_ref[...] = jnp.where(row >= pad, y, 0.0)


def _mix_merge_call(h, l, p, yd, gdn_prm, ssd_prm, hg_prm, msk, tri, wg, wb, wo, g, b, pad, drop_first_tile):
    bsz, lp, d = h.shape
    tok = lambda w, j=0: pl.BlockSpec((None, TILE, w), lambda b_, t: (b_, t, j))
    return pl.pallas_call(
        functools.partial(_mix_merge_kernel, pad=pad),
        grid=(bsz, lp // TILE),
        in_specs=[tok(d), tok(1536, OFF_QKV // 1536), tok(512, OFF_GZ // 512), tok(128, OFF_BA // 128),
                  tok(1024, OFF_XBC // 1024), tok(1024, OFF_MZ // 1024), tok(2048, OFF_HG // 2048), tok(BRANCH_W),
                  _full((CONV_K, 1536)), _full((8, 128)), _full((1, GDN_DK)),
                  _full((CONV_K, 1024)), _full((1, 1024)), _full((8, 512)),
                  _full((CH, 256)), _full((CH, 256)), _full((256, 256)), _full((CH, CH)),
                  _full((8, 512)), _full((1, HG_DK)), _full((9, CH, CH)), _full((CH, CH)),
                  _layer_full((d, N_BRANCH * d), l), _layer_full((N_BRANCH, BRANCH_W, d), l),
                  _layer_full((d, d), l), _full((1, d)), _full((1, d))],
        out_specs=(pl.BlockSpec((None, TILE, d), lambda b_, t: (b_, jnp.maximum(t - 1, 0), 0))
                   if drop_first_tile else tok(d)),
        out_shape=jax.ShapeDtypeStruct((bsz, lp - TILE if drop_first_tile else lp, d), F32),
        scratch_shapes=[pltpu.VMEM((TILE + 8, 1536), F32), pltpu.VMEM((GDN_HEADS, GDN_DK, GDN_DK), F32),
                        pltpu.VMEM((TILE + 8, 1024), F32), pltpu.VMEM((M2_GROUPS, M2_DSTATE, 256), F32),
                        pltpu.VMEM((HG_HEADS, HG_DK, HG_DK), F32),
                        pltpu.VMEM((TILE, BRANCH_W), BF), pltpu.VMEM((TILE, BRANCH_W), BF),
                        pltpu.VMEM((TILE, BRANCH_W), BF), pltpu.VMEM((TILE, N_BRANCH * D_MODEL), F32)],
        compiler_params=_cparams(),
        name="mix_merge",
    )(h, p, p, p, p, p, p, yd, *gdn_prm, *ssd_prm, *hg_prm, msk, tri, wg, wb, wo, g.reshape(1, d), b.reshape(1, d))


def _prep_w(w):
    sizes = (1536, 512, 4, 4, 1024, 512, 8, 512, 512, 512, 512, 512, 512, 4096)
    o = [0]
    for n in sizes:
        o.append(o[-1] + n)
    cut = lambda a, b: w[..., o[a]:o[b]]
    ba = jnp.concatenate([cut(2, 4), jnp.zeros(w.shape[:-1] + (120,), w.dtype)], axis=-1)
    pieces = (cut(0, 2), cut(4, 5), cut(5, 6), jnp.repeat(cut(6, 7), M2_HEADDIM, axis=-1), cut(7, 11), ba, cut(11, 13))
    return tuple(p.astype(BF) for p in pieces), (0.5 * cut(13, 14)).astype(BF)


def _lane_row(vals, off, width=128):
    return jnp.zeros((width,), F32).at[off:off + vals.shape[0]].set(vals.astype(F32))


def _prep_s5(a_re, a_im, b_re, b_im, c_re, c_im, log_dt):
    f = lambda v: v.astype(F32)
    a_re, a_im, b_re, b_im, c_re, c_im = map(f, (a_re, a_im, b_re, b_im, c_re, c_im))
    dt = jnp.exp(f(log_dt))[:, None]

    def lam_pow(k):
        mag = jnp.exp(a_re * dt * k)
        return mag * jnp.cos(a_im * dt * k), mag * jnp.sin(a_im * dt * k)

    lam_re, lam_im = lam_pow(1.0)
    den = jnp.square(a_re) + jnp.square(a_im)
    nr, ni = lam_re - 1.0, lam_im
    z_re, z_im = (nr * a_re + ni * a_im) / den, (ni * a_re - nr * a_im) / den
    bb_re = z_re[..., None] * b_re - z_im[..., None] * b_im
    bb_im = z_re[..., None] * b_im + z_im[..., None] * b_re
    ks = jnp.arange(S5_CK, dtype=F32)[:, None, None]
    pr, pi = lam_pow(ks)
    p1r, p1i = lam_pow(ks + 1.0)
    mr, mi = lam_pow(S5_CK * (ks + 1.0))

    def c_times(qr, qi):
        return (c_re[None] * qr[:, :, None, :] - c_im[None] * qi[:, :, None, :],
                c_re[None] * qi[:, :, None, :] + c_im[None] * qr[:, :, None, :])

    cl_re, cl_im = c_times(pr, pi)
    kd = (jnp.einsum('dgcp,gpe->dgce', cl_re, bb_re) - jnp.einsum('dgcp,gpe->dgce', cl_im, bb_im))
    s_i = jnp.arange(S5_CK)[:, None]
    t_i = jnp.arange(S5_CK)[None, :]
    kt = jnp.where((t_i >= s_i)[:, :, None, None, None], kd[jnp.clip(t_i - s_i, 0, S5_CK - 1)], 0.0)
    toep = jnp.transpose(kt, (2, 0, 4, 1, 3)).reshape(S5_NG, 128, 128)
    rr, ri = pr[::-1], pi[::-1]
    w_re = rr[..., None] * bb_re[None] - ri[..., None] * bb_im[None]
    w_im = rr[..., None] * bb_im[None] + ri[..., None] * bb_re[None]
    to_rows = lambda w: jnp.transpose(w, (1, 0, 3, 2)).reshape(S5_NG, 128, S5_P)
    m1 = jnp.concatenate([toep, to_rows(w_re), to_rows(w_im)], axis=2).astype(BF)
    q_re, q_im = c_times(p1r, p1i)
    to_cols = lambda v: jnp.transpose(v, (1, 3, 0, 2)).reshape(S5_NG, S5_P, 128)
    q = jnp.concatenate([to_cols(q_re), -to_cols(q_im)], axis=1).astype(BF)
    pack1 = lambda v: jnp.concatenate([v, v], axis=-1).reshape(v.shape[0], S5_LANES)
    pack2 = lambda v: jnp.concatenate([-v, v], axis=-1).reshape(v.shape[0], S5_LANES)
    row = jnp.arange(8)[:, None]
    tabs = [pack1(mr), pack2(mi)]
    for dd in (1, 2, 4):
        tabs.append(jnp.where(row >= dd, pack1(mr[dd - 1:dd]), 0.0))
        tabs.append(jnp.where(row >= dd, pack2(mi[dd - 1:dd]), 0.0))
    lane_piece = jnp.arange(128) // 16
    lm = jnp.stack([(lane_piece & dd) != 0 for dd in (4, 2, 1)] + [jnp.zeros((128,), bool)] * 5).astype(F32)
    return m1, q, jnp.concatenate(tabs, axis=0), lm


@jax.jit
def kernel(x, meta_tokens, ln_in_g, ln_in_b, w_in, gdn_conv_w, gdn_A_log, gdn_dt_bias, gdn_norm_w, m2_conv_w, m2_conv_b, m2_dt_bias, m2_A_log, m2_D, m2_norm_w, hg_lb_logits, hg_norm_w, s5_A_re, s5_A_im, s5_B_re, s5_B_im, s5_C_re, s5_C_im, s5_D, s5_log_dt, s5_glu_w1, s5_glu_w2, w_branch, w_out, ln_g, ln_b):
    bsz, seq, d = x.shape
    ltot = N_META + seq
    lp = -(-ltot // TILE) * TILE
    pad = lp - ltot
    assert pad + N_META == TILE, "real tokens must start on the second tile"
    meta_tile = jnp.concatenate([jnp.zeros((pad, d), F32), meta_tokens.astype(F32)], axis=0)
    h = _ln_in_call(x.astype(F32), meta_tile, ln_in_g.astype(F32), ln_in_b.astype(F32), pad)

    msk = _masks64()
    tri = msk[0].astype(BF)
    ones = jnp.ones((CH, CH), BF)
    jj = jnp.arange(256)
    eyet = (jnp.arange(CH)[:, None] == (jj % CH)[None, :]).astype(F32)
    maskt = (jnp.arange(CH)[:, None] >= (jj % CH)[None, :]).astype(F32)
    bd = ((jj // CH)[:, None] == (jj // CH)[None, :]).astype(F32)

    cum = jnp.cumsum(jax.nn.softmax(hg_lb_logits.astype(F32), axis=0), axis=0)
    lower_bounds = cum - cum[0:1]

    w_pieces, wg_all = _prep_w(w_in)
    m1_all, q_all, pw_all, lm_all = jax.vmap(_prep_s5)(s5_A_re, s5_A_im, s5_B_re, s5_B_im, s5_C_re, s5_C_im,
                                                       s5_log_dt)
    glu1_all, glu2_all = s5_glu_w1.astype(BF), s5_glu_w2.astype(BF)
    wb_all, wo_all = (0.5 * w_branch).astype(BF), w_out.astype(BF)

    s5_d_all = s5_D.astype(F32)[:, None, :]
    for l in range(w_in.shape[0]):
        p, yd = _proj_s5_call(h, l, w_pieces, m1_all, q_all, pw_all, lm_all, s5_d_all, glu1_all, glu2_all)

        lane = jnp.stack([_lane_row(jnp.exp(gdn_A_log[l].astype(F32)), 4), _lane_row(gdn_dt_bias[l], 4)]
                         + [jnp.zeros((128,), F32)] * 6)
        gdn_prm = (gdn_conv_w[l].astype(F32), lane, gdn_norm_w[l].astype(F32).reshape(1, GDN_DK))

        rep = lambda v: jnp.repeat(v.astype(F32), M2_HEADDIM)
        vec = jnp.stack([rep(m2_dt_bias[l]), rep(-jnp.exp(m2_A_log[l].astype(F32))), rep(m2_D[l]),
                         m2_norm_w[l].astype(F32)] + [jnp.zeros((BRANCH_W,), F32)] * 4)
        ssd_prm = (m2_conv_w[l].astype(F32), m2_conv_b[l].astype(F32).reshape(1, -1), vec, eyet, maskt, bd, ones)

        lb = lower_bounds[l]
        hvec = jnp.stack([jnp.log(lb), jnp.log1p(-lb), 1.0 - lb] + [jnp.zeros((BRANCH_W,), F32)] * 5)
        hg_prm = (hvec, hg_norm_w[l].astype(F32).reshape(1, HG_DK))

        h = _mix_merge_call(h, l, p, yd, gdn_prm, ssd_prm, hg_prm, msk, tri, wg_all, wb_all,
                            wo_all, ln_g[l].astype(F32), ln_b[l].astype(F32), pad,
                            drop_first_tile=(l == w_in.shape[0] - 1))
    return h.astype(x.dtype)
```

```python
import functools
import itertools
import math

import jax
import jax.numpy as jnp
from jax import lax
from jax.experimental import pallas as pl
from jax.experimental.pallas import tpu as pltpu

F32 = jnp.float32
BF = jnp.bfloat16

D_MODEL = 1024
DEPTH = 4
N_META = 16
CONV_K = 4
N_BRANCH = 4
BRANCH_W = 512
GDN_HEADS = 4
GDN_DK = 128
M2_HEADS = 8
M2_HEADDIM = 64
M2_GROUPS = 2
M2_DSTATE = 128
HG_HEADS = 4
HG_DK = 128
S5_GROUP = 16
S5_NG = 32
S5_P = 64
S5_STATE = S5_NG * S5_P
ALPHA = (2 * DEPTH) ** 0.25
LN_EPS = 1e-5
RMS_EPS = 1e-6

TILE = 256
CH = 64
NCH = TILE // CH
SUB = 16
HG_EXP_CLAMP = 80.0
VMEM_LIMIT = 56 * 1024 * 1024

OFF_QKV, OFF_GZ, OFF_XBC, OFF_MZ, OFF_MDT = 0, 1536, 2048, 3072, 3584
OFF_HG, OFF_BA, OFF_S5 = 4096, 6144, 6272
NP = 6272
N1 = 7296
W_PIECES = ((0, 2048), (OFF_XBC, 1024), (OFF_MZ, 512), (OFF_MDT, 512), (OFF_HG, 2048), (OFF_BA, 128), (OFF_S5, 1024))


def _sigmoid(x):
    return 0.5 * (jnp.tanh(0.5 * x) + 1.0)


def _silu(x):
    return x * _sigmoid(x)


def _softplus(x):
    return jnp.maximum(x, 0.0) + jnp.log1p(jnp.exp(-jnp.abs(x)))


def _mm(a, b):
    return jnp.dot(a.astype(BF), b.astype(BF), preferred_element_type=F32)


def _mm_nt(a, b):
    return lax.dot_general(a.astype(BF), b.astype(BF), (((1,), (1,)), ((), ())),
                           preferred_element_type=F32)


def _mm_tn(a, b):
    return lax.dot_general(a.astype(BF), b.astype(BF), (((0,), (0,)), ((), ())),
                           preferred_element_type=F32)


def _mm3_left(m_bf, x):
    hi = x.astype(BF)
    r1 = x - hi.astype(F32)
    mid = r1.astype(BF)
    lo = (r1 - mid.astype(F32)).astype(BF)
    d = lambda v: jnp.dot(m_bf, v, preferred_element_type=F32)
    return (d(lo) + d(mid)) + d(hi)


def _rms_norm(x):
    return x * lax.rsqrt(jnp.mean(x * x, axis=-1, keepdims=True) + RMS_EPS)


def _masks64():
    i = jnp.arange(CH)[:, None]
    j = jnp.arange(CH)[None, :]
    ms = [(i >= j), (i > j)]
    for s in (1, 2, 4, 8, 16, 32):
        ms.append((i // (2 * s) == j // (2 * s)) & (i % (2 * s) >= s) & (j % (2 * s) < s))
    ms.append(i == j)
    return jnp.stack(ms).astype(F32)


def _cparams():
    return pltpu.CompilerParams(dimension_semantics=("arbitrary", "arbitrary"),
                                vmem_limit_bytes=VMEM_LIMIT)


def _full(shape):
    n = len(shape)
    return pl.BlockSpec(shape, lambda b, t: (0,) * n, pipeline_mode=pl.Buffered(1))


def _layer_full(shape, l):
    n = len(shape)
    return pl.BlockSpec((None,) + tuple(shape), lambda b, t: (l,) + (0,) * n, pipeline_mode=pl.Buffered(1))


def _ln_in_kernel(x_ref, m_ref, g_ref, b_ref, o_ref, *, pad):
    t = pl.program_id(1)
    x = jnp.where(t == 0, m_ref[...], x_ref[...])
    mu = jnp.mean(x, axis=-1, keepdims=True)
    xc = x - mu
    var = jnp.mean(xc * xc, axis=-1, keepdims=True)
    y = xc * lax.rsqrt(var + LN_EPS) * g_ref[...] + b_ref[...]
    row = t * TILE + lax.broadcasted_iota(jnp.int32, (TILE, 1), 0)
    o_ref[...] = jnp.where(row >= pad, y, 0.0)


def _ln_in_call(x, meta_tile, g, b, pad):
    bsz, seq, d = x.shape
    return pl.pallas_call(
        functools.partial(_ln_in_kernel, pad=pad),
        grid=(bsz, seq // TILE + 1),
        in_specs=[pl.BlockSpec((None, TILE, d), lambda b_, t: (b_, jnp.maximum(t - 1, 0), 0)),
                  _full((TILE, d)), _full((1, d)), _full((1, d))],
        out_specs=pl.BlockSpec((None, TILE, d), lambda b_, t: (b_, t, 0)),
        out_shape=jax.ShapeDtypeStruct((bsz, seq + TILE, d), F32),
        compiler_params=_cparams(),
        name="ln_in",
    )(x, meta_tile, g.reshape(1, d), b.reshape(1, d))


S5_CK = 8
S5_NCK = TILE // S5_CK
S5_PACK = 2 * S5_P
S5_LANES = S5_NG * S5_PACK
PROJ_CHUNK = 512
GATE_CHUNK = 256
MIX_PLAN = "gsh" + "gfgfgfgfh" * 4 + "gs" * 8 + "gg" + "hDhA" + "hhBhh" + "C"


def _transpose_pieces(arrs, lm_ref):
    for k, d in enumerate((4, 2, 1)):
        m = lm_ref[k:k + 1, :] > 0
        new = list(arrs)
        for s_ in range(8):
            if s_ & d == 0:
                lo, hi = arrs[s_], arrs[s_ + d]
                new[s_] = jnp.where(m, pltpu.roll(hi, 16 * d, 1), lo)
                new[s_ + d] = jnp.where(m, hi, pltpu.roll(lo, 128 - 16 * d, 1))
        arrs = new
    return arrs


def _s5_scan_block(b, cs, carry, tabs, st_ref):
    rows = slice(8 * b, 8 * b + 8)
    v = st_ref[rows, cs]
    for n, d in enumerate((1, 2, 4)):
        vsh = pltpu.roll(v, d, 0)
        v = v + (tabs[2 + 2 * n] * vsh + tabs[3 + 2 * n] * pltpu.roll(vsh, S5_P, 1))
    cb = jnp.broadcast_to(carry, (8, S5_PACK))
    v = v + (tabs[0] * cb + tabs[1] * pltpu.roll(cb, S5_P, 1))
    st_ref[rows, cs] = v
    return v[7:8, :]


def _proj_s5_kernel(h_ref, w0_ref, w1_ref, w2_ref, w3_ref, w4_ref, w5_ref, w6_ref,
                    m1_ref, q_ref, pw_ref, lm_ref, d_ref, g1_ref, g2_ref,
                    p_ref, yd_ref, u_ref, ug_ref, yq_ref, st_ref, y_ref, c_ref):
    t = pl.program_id(1)

    @pl.when(t == 0)
    def _():
        c_ref[...] = jnp.zeros_like(c_ref)

    def wcols(j, w):
        for (start, width), ref in zip(W_PIECES, (w0_ref, w1_ref, w2_ref, w3_ref, w4_ref, w5_ref, w6_ref)):
            if start <= j and j + w <= start + width:
                return ref[:, j - start:j - start + w]
        raise ValueError((j, w))

    hb = h_ref[...].astype(BF)
    u = jnp.dot(hb, wcols(OFF_S5, BRANCH_W), preferred_element_type=F32)
    for j in range(4):
        u_ref[j] = u[:, j * 128:(j + 1) * 128]

    starts = iter(range(0, NP, PROJ_CHUNK))

    def proj_chunks(n):
        for j in itertools.islice(starts, n):
            w = min(PROJ_CHUNK, NP - j)
            p_ref[:, j:j + w] = jnp.dot(hb, wcols(j, w), preferred_element_type=F32)

    for j in range(4):
        proj_chunks(1)
        arrs = [u_ref[j, pl.ds(s_, S5_NCK, stride=S5_CK), :] for s_ in range(S5_CK)]
        arrs = _transpose_pieces(arrs, lm_ref)
        for i in range(8):
            g = 8 * j + i
            ug_ref[:, g * 128:(g + 1) * 128] = arrs[i].astype(BF)
    row0 = lax.broadcasted_iota(jnp.int32, (S5_NCK, S5_PACK), 0) == 0
    last_inc = []
    for g in range(S5_NG):
        if g % 11 == 10:
            proj_chunks(1)
        cs = slice(g * S5_PACK, (g + 1) * S5_PACK)
        tw = jnp.dot(ug_ref[:, cs], m1_ref[g], preferred_element_type=F32)
        yq_ref[:, cs] = tw[:, :S5_PACK]
        inc = tw[:, S5_PACK:]
        last_inc.append(inc[S5_NCK - 1:S5_NCK, :])
        st_ref[:, cs] = jnp.where(row0, c_ref[:, cs], pltpu.roll(inc, 1, 0))

    for g in range(S5_NG):
        if g % 8 == 0:
            proj_chunks(1)
        cs = slice(g * S5_PACK, (g + 1) * S5_PACK)
        tabs = [pw_ref[8 * k:8 * k + 8, cs] for k in range(8)]
        x = jnp.zeros((1, S5_PACK), F32)
        for b in range(S5_NCK // 8):
            x = _s5_scan_block(b, cs, x, tabs, st_ref)
        c_ref[:, cs] = last_inc[g] + (tabs[0][0:1] * x + tabs[1][0:1] * pltpu.roll(
            jnp.broadcast_to(x, (8, S5_PACK)), S5_P, 1)[0:1])

    for g in range(S5_NG):
        if g % 16 == 0:
            proj_chunks(1)
        cs = slice(g * S5_PACK, (g + 1) * S5_PACK)
        yq_ref[:, cs] = yq_ref[:, cs] + _mm(st_ref[:, cs], q_ref[g])
    proj_chunks(NP)
    for j in range(4):
        arrs = _transpose_pieces([yq_ref[:, (8 * j + i) * 128:(8 * j + i + 1) * 128] for i in range(8)], lm_ref)
        for s_ in range(S5_CK):
            y_ref[j, pl.ds(s_, S5_NCK, stride=S5_CK), :] = arrs[s_]
    y = jnp.concatenate([y_ref[j] for j in range(4)], axis=1) + d_ref[...] * u
    y = 0.5 * y * (1.0 + jnp.tanh(math.sqrt(2.0 / math.pi) * (y + 0.044715 * (y * y * y))))
    yb = y.astype(BF)
    glu = jnp.dot(yb, g1_ref[...], preferred_element_type=F32) * _sigmoid(
        jnp.dot(yb, g2_ref[...], preferred_element_type=F32))
    z = jnp.dot(hb, wcols(OFF_S5 + BRANCH_W, BRANCH_W), preferred_element_type=F32)
    yd_ref[...] = (glu * _silu(z)).astype(yd_ref.dtype)


def _proj_s5_call(h, l, w_pieces, m1, q, pw, lm, dvec, g1, g2):
    bsz, lp, d = h.shape
    return pl.pallas_call(
        _proj_s5_kernel,
        grid=(bsz, lp // TILE),
        in_specs=[pl.BlockSpec((None, TILE, d), lambda b_, t: (b_, t, 0))]
        + [_layer_full((d, width), l) for _, width in W_PIECES]
        + [_layer_full((S5_NG, 128, 256), l), _layer_full((S5_NG, 128, 128), l), _layer_full((64, S5_LANES), l),
           _layer_full((8, 128), l), _layer_full((1, 512), l), _layer_full((512, 512), l),
           _layer_full((512, 512), l)],
        out_specs=[pl.BlockSpec((None, TILE, NP), lambda b_, t: (b_, t, 0)),
                   pl.BlockSpec((None, TILE, BRANCH_W), lambda b_, t: (b_, t, 0))],
        out_shape=[jax.ShapeDtypeStruct((bsz, lp, NP), F32), jax.ShapeDtypeStruct((bsz, lp, BRANCH_W), BF)],
        scratch_shapes=[pltpu.VMEM((4, TILE, 128), F32), pltpu.VMEM((S5_NCK, S5_LANES), BF),
                        pltpu.VMEM((S5_NCK, S5_LANES), F32), pltpu.VMEM((S5_NCK, S5_LANES), F32),
                        pltpu.VMEM((4, TILE, 128), F32), pltpu.VMEM((1, S5_LANES), F32)],
        compiler_params=_cparams(),
        name="proj_s5",
    )(h, *w_pieces, m1, q, pw, lm, dvec, g1, g2)


def _conv_chunk(xp_ref, cw, r0):
    x = xp_ref[r0 + 8:r0 + 8 + CH, :] * cw[CONV_K - 1:CONV_K, :]
    for j in range(CONV_K - 1):
        s = r0 + 8 - (CONV_K - 1) + j
        x = x + xp_ref[s:s + CH, :] * cw[j:j + 1, :]
    return x


def _gdn_steps(qkv_ref, z_ref, ba_ref, cw_ref, lane_ref, nw_ref, msk_ref, tri_ref, y_ref, xp_ref, s_ref):
    t = pl.program_id(1)

    @pl.when(t == 0)
    def _():
        xp_ref[0:8, :] = jnp.zeros((8, 3 * BRANCH_W), F32)
        s_ref[...] = jnp.zeros_like(s_ref)

    xp_ref[8:8 + TILE, :] = qkv_ref[...]
    yield
    cw = cw_ref[...]
    exp_a = lane_ref[0:1, :]
    dt_bias = lane_ref[1:2, :]
    nw = nw_ref[...]
    causal = msk_ref[0]
    strict = msk_ref[1]
    tri = tri_ref[...]
    items = []
    a_list = []
    for c in range(NCH):
        r0 = c * CH
        qkv = _silu(_conv_chunk(xp_ref, cw, r0))
        ba = ba_ref[r0:r0 + CH, :]
        beta_all = _sigmoid(ba)
        gl = -(exp_a * _softplus(ba + dt_bias))
        gc = _mm3_left(tri, gl)
        gc_t = gc.T
        for h in range(GDN_HEADS):
            sl = slice(h * GDN_DK, (h + 1) * GDN_DK)
            q = qkv[:, sl]
            k = qkv[:, BRANCH_W + h * GDN_DK:BRANCH_W + (h + 1) * GDN_DK]
            v = qkv[:, 2 * BRANCH_W + h * GDN_DK:2 * BRANCH_W + (h + 1) * GDN_DK]
            q = q * (lax.rsqrt(jnp.sum(q * q, axis=-1, keepdims=True) + RMS_EPS) * GDN_DK ** -0.5)
            k = k * lax.rsqrt(jnp.sum(k * k, axis=-1, keepdims=True) + RMS_EPS)
            beta = beta_all[:, h:h + 1]
            gcol = gc[:, 4 + h:5 + h]
            grow = gc_t[4 + h:5 + h, :]
            decay = jnp.exp(jnp.where(causal > 0, gcol - grow, -jnp.inf))
            kb = k * beta
            eg = jnp.exp(gcol)
            glast = gc[CH - 1:CH, 4 + h:5 + h]
            a_list.append(strict * (_mm_nt(kb, k) * decay))
            items.append(dict(
                rhs=jnp.concatenate([v * beta, kb * eg], axis=1),
                qk=causal * (_mm_nt(q, k) * decay),
                qd=q * eg,
                kd=k * jnp.exp(glast - gcol),
                gtot=jnp.exp(glast)))
            yield
    tinvs = [msk_ref[8] - a * msk_ref[2] for a in a_list]
    for lvl in range(1, 6):
        m = msk_ref[2 + lvl]
        t1 = [_mm(x, a * m) for x, a in zip(tinvs, a_list)]
        tinvs = [x - _mm(t_, x) for x, t_ in zip(tinvs, t1)]
        yield
    uws = [_mm(tinv, it["rhs"]) for tinv, it in zip(tinvs, items)]
    kd_uw = [_mm_tn(it["kd"], uw) for it, uw in zip(items, uws)]
    qk_uw = [_mm(it["qk"], uw) for it, uw in zip(items, uws)]
    yield
    states = [s_ref[h] for h in range(GDN_HEADS)]
    for c in range(NCH):
        r0 = c * CH
        for h in range(GDN_HEADS):
            sl = slice(h * GDN_DK, (h + 1) * GDN_DK)
            n = c * GDN_HEADS + h
            s = states[h]
            o = _mm(items[n]["qd"] - qk_uw[n][:, GDN_DK:], s) + qk_uw[n][:, :GDN_DK]
            states[h] = (s * items[n]["gtot"] - _mm(kd_uw[n][:, GDN_DK:], s)) + kd_uw[n][:, :GDN_DK]
            y = _rms_norm(o) * nw * _silu(z_ref[r0:r0 + CH, sl])
            y_ref[r0:r0 + CH, sl] = y.astype(y_ref.dtype)
        if c == NCH - 1:
            for h in range(GDN_HEADS):
                s_ref[h] = states[h]
            xp_ref[0:8, :] = xp_ref[TILE:TILE + 8, :]
        yield


def _ssd_steps(xbc_ref, zdt_ref, cw_ref, cb_ref, vec_ref, eyet_ref, maskt_ref, bd_ref, tri_ref, ones_ref,
               y_ref, xp_ref, s_ref, *, pad):
    t = pl.program_id(1)

    @pl.when(t == 0)
    def _():
        xp_ref[0:8, :] = jnp.zeros((8, 1024), F32)
        s_ref[...] = jnp.zeros_like(s_ref)

    xp_ref[8:8 + TILE, :] = xbc_ref[...]
    yield
    cw = cw_ref[...]
    cb = cb_ref[...]
    dt_bias = vec_ref[0:1, :]
    neg_a = vec_ref[1:2, :]
    d_skip = vec_ref[2:3, :]
    nw = vec_ref[3:4, :]
    tri = tri_ref[...]
    ones = ones_ref[...]
    gw = M2_HEADDIM * (M2_HEADS // M2_GROUPS)
    for c in range(NCH):
        r0 = c * CH
        xbc = _silu(_conv_chunk(xp_ref, cw, r0) + cb)
        xs = xbc[:, :BRANCH_W]
        row = t * TILE + r0 + lax.broadcasted_iota(jnp.int32, (CH, 1), 0)
        valid = (row >= pad).astype(F32)
        dt = _softplus(zdt_ref[r0:r0 + CH, BRANCH_W:] + dt_bias) * valid
        a = dt * neg_a
        acum = _mm3_left(tri, a)
        xdt = xs * dt
        z = zdt_ref[r0:r0 + CH, :BRANCH_W]
        for g in range(M2_GROUPS):
            gs = slice(g * gw, (g + 1) * gw)
            bg = xbc[:, BRANCH_W + g * M2_DSTATE:BRANCH_W + (g + 1) * M2_DSTATE]
            cg = xbc[:, BRANCH_W + M2_GROUPS * M2_DSTATE + g * M2_DSTATE:
                     BRANCH_W + M2_GROUPS * M2_DSTATE + (g + 1) * M2_DSTATE]
            ag = acum[:, gs]
            cbt = _mm_nt(cg, jnp.concatenate([bg] * 4, axis=0))
            arow = _mm3_left(ones, ag * eyet_ref[...])
            lm = jnp.exp(jnp.where(maskt_ref[...] > 0, ag - arow, -jnp.inf))
            xg = xdt[:, gs]
            xbd = jnp.concatenate([xg] * 4, axis=0) * bd_ref[...]
            y_diag = _mm(cbt * lm, xbd)
            alast = ag[CH - 1:CH, :]
            s = s_ref[g]
            y_off = _mm(cg, s) * jnp.exp(ag)
            s_ref[g] = s * jnp.exp(alast) + _mm_tn(bg, xg * jnp.exp(alast - ag))
            y = (y_diag + y_off + d_skip[:, gs] * xs[:, gs]) * _silu(z[:, gs])
            y = _rms_norm(y) * nw[:, gs]
            y_ref[r0:r0 + CH, gs] = y.astype(y_ref.dtype)
            if c == NCH - 1 and g == M2_GROUPS - 1:
                xp_ref[0:8, :] = xp_ref[TILE:TILE + 8, :]
            yield


def _hgrn_steps(p_ref, vec_ref, nw_ref, msk_ref, tri_ref, y_ref, s_ref):
    t = pl.program_id(1)

    @pl.when(t == 0)
    def _():
        s_ref[...] = jnp.zeros_like(s_ref)

    yield
    log_lb = vec_ref[0:1, :]
    log1m_lb = vec_ref[1:2, :]
    one_m_lb = vec_ref[2:3, :]
    nw = nw_ref[...]
    causal = msk_ref[0]
    tri = tri_ref[...]
    hsl = [slice(h * HG_DK, (h + 1) * HG_DK) for h in range(HG_HEADS)]
    nsub = CH // SUB
    prep = []
    for c in range(NCH):
        r0 = c * CH
        q = _silu(p_ref[r0:r0 + CH, 0:512])
        zf = p_ref[r0:r0 + CH, 512:1024]
        e = jnp.exp(-jnp.abs(zf))
        lsig = jnp.minimum(zf, 0.0) - jnp.log1p(e)
        bb = log1m_lb + lsig
        logf = jnp.maximum(log_lb, bb) + jnp.log1p(jnp.exp(-jnp.abs(log_lb - bb)))
        sig_neg = jnp.where(zf >= 0, e, 1.0) / (1.0 + e)
        k = one_m_lb * sig_neg
        gcum = _mm3_left(tri, logf)
        glast = gcum[CH - 1:CH, :]
        subs = []
        for i in range(nsub):
            rs = slice(i * SUB, (i + 1) * SUB)
            n = (i + 1) * SUB
            gref = gcum[i * SUB - 1:i * SUB, :] if i > 0 else jnp.zeros((1, BRANCH_W), F32)
            qt = q[rs] * jnp.exp(gcum[rs] - gref)
            kt = k[:n] * jnp.exp(jnp.minimum(gref - gcum[:n], HG_EXP_CLAMP))
            subs.append((qt, kt, qt * jnp.exp(gref)))
        prep.append(dict(kd=k * jnp.exp(glast - gcum), eg=jnp.exp(glast), subs=subs))
        yield
    incs = [[_mm_tn(p_ref[c * CH:(c + 1) * CH, 1024 + h * HG_DK:1024 + (h + 1) * HG_DK], prep[c]["kd"][:, hsl[h]])
             for h in range(HG_HEADS)] for c in range(NCH)]
    states = [[s_ref[h] for h in range(HG_HEADS)]]
    for c in range(NCH):
        states.append([states[c][h] * prep[c]["eg"][:, hsl[h]] + incs[c][h] for h in range(HG_HEADS)])
    for h in range(HG_HEADS):
        s_ref[h] = states[NCH][h]
    yield
    ams = [[[_mm_nt(prep[c]["subs"][i][0][:, hsl[h]], prep[c]["subs"][i][1][:, hsl[h]])
             * causal[i * SUB:(i + 1) * SUB, :(i + 1) * SUB]
             for h in range(HG_HEADS)] for i in range(nsub)] for c in range(NCH)]
    yield
    for c in range(NCH):
        r0 = c * CH
        for h in range(HG_HEADS):
            iv = p_ref[r0:r0 + CH, 1024 + h * HG_DK:1024 + (h + 1) * HG_DK]
            o = jnp.concatenate(
                [_mm(ams[c][i][h], iv[:(i + 1) * SUB]) + _mm_nt(prep[c]["subs"][i][2][:, hsl[h]], states[c][h])
                 for i in range(nsub)], axis=0)
            z = p_ref[r0:r0 + CH, 1536 + h * HG_DK:1536 + (h + 1) * HG_DK]
            y_ref[r0:r0 + CH, hsl[h]] = (_rms_norm(o) * nw * _silu(z)).astype(y_ref.dtype)
        yield


def _mix_merge_kernel(h_ref, qkv_ref, gz_ref, ba_ref, xbc_ref, zdt_ref, hg_ref, yd_ref,
                      gcw_ref, glane_ref, gnw_ref, scw_ref, scb_ref, svec_ref, eyet_ref, maskt_ref, bd_ref, ones_ref,
                      hvec_ref, hnw_ref, msk_ref, tri_ref, wg_ref, wb_ref, wo_ref, g_ref, b_ref,
                      o_ref, gxp_ref, gs_ref, sxp_ref, ss_ref, hs_ref, ya_ref, yb_ref, yc_ref, gate_ref, *, pad):
    t = pl.program_id(1)
    h = h_ref[...]
    hb = h.astype(BF)

    def gate_piece(j):
        cs = slice(j * GATE_CHUNK, (j + 1) * GATE_CHUNK)
        gate_ref[:, cs] = jnp.tanh(jnp.dot(hb, wg_ref[:, cs], preferred_element_type=F32)) + 1.0

    def branch_piece(br, y_ref):
        cs = slice(br * D_MODEL, (br + 1) * D_MODEL)
        gate_ref[:, cs] = gate_ref[:, cs] * jnp.dot(y_ref[...], wb_ref[br], preferred_element_type=F32)

    gdn = _gdn_steps(qkv_ref, gz_ref, ba_ref, gcw_ref, glane_ref, gnw_ref, msk_ref, tri_ref, ya_ref, gxp_ref, gs_ref)
    ssd = _ssd_steps(xbc_ref, zdt_ref, scw_ref, scb_ref, svec_ref, eyet_ref, maskt_ref, bd_ref, tri_ref, ones_ref,
                     yb_ref, sxp_ref, ss_ref, pad=pad)
    hgrn = _hgrn_steps(hg_ref, hvec_ref, hnw_ref, msk_ref, tri_ref, yc_ref, hs_ref)
    gates = iter([functools.partial(gate_piece, j) for j in range(N_BRANCH * D_MODEL // GATE_CHUNK)])
    emit = {"g": lambda: next(gdn), "s": lambda: next(ssd), "h": lambda: next(hgrn), "f": lambda: next(gates)(),
            "A": lambda: branch_piece(0, ya_ref), "B": lambda: branch_piece(1, yb_ref),
            "C": lambda: branch_piece(2, yc_ref), "D": lambda: branch_piece(3, yd_ref)}
    for tok in MIX_PLAN:
        emit[tok]()
    mixed = ((gate_ref[:, 0:D_MODEL] + gate_ref[:, D_MODEL:2 * D_MODEL])
             + (gate_ref[:, 2 * D_MODEL:3 * D_MODEL] + gate_ref[:, 3 * D_MODEL:4 * D_MODEL]))
    out = jnp.dot(mixed.astype(BF), wo_ref[...], preferred_element_type=F32)
    r = ALPHA * h + out
    mu = jnp.mean(r, axis=-1, keepdims=True)
    rc = r - mu
    var = jnp.mean(rc * rc, axis=-1, keepdims=True)
    y = rc * lax.rsqrt(var + LN_EPS) * g_ref[...] + b_ref[...]
    row = t * TILE + lax.broadcasted_iota(jnp.int32, (TILE, 1), 0)
    o_ref[...] = jnp.where(row >= pad, y, 0.0)


def _mix_merge_call(h, l, p, yd, gdn_prm, ssd_prm, hg_prm, msk, tri, wg, wb, wo, g, b, pad, drop_first_tile):
    bsz, lp, d = h.shape
    tok = lambda w, j=0: pl.BlockSpec((None, TILE, w), lambda b_, t: (b_, t, j))
    return pl.pallas_call(
        functools.partial(_mix_merge_kernel, pad=pad),
        grid=(bsz, lp // TILE),
        in_specs=[tok(d), tok(1536, OFF_QKV // 1536), tok(512, OFF_GZ // 512), tok(128, OFF_BA // 128),
                  tok(1024, OFF_XBC // 1024), tok(1024, OFF_MZ // 1024), tok(2048, OFF_HG // 2048), tok(BRANCH_W),
                  _full((CONV_K, 1536)), _full((8, 128)), _full((1, GDN_DK)),
                  _full((CONV_K, 1024)), _full((1, 1024)), _full((8, 512)),
                  _full((CH, 256)), _full((CH, 256)), _full((256, 256)), _full((CH, CH)),
                  _full((8, 512)), _full((1, HG_DK)), _full((9, CH, CH)), _full((CH, CH)),
                  _layer_full((d, N_BRANCH * d), l), _layer_full((N_BRANCH, BRANCH_W, d), l),
                  _layer_full((d, d), l), _full((1, d)), _full((1, d))],
        out_specs=(pl.BlockSpec((None, TILE, d), lambda b_, t: (b_, jnp.maximum(t - 1, 0), 0))
                   if drop_first_tile else tok(d)),
        out_shape=jax.ShapeDtypeStruct((bsz, lp - TILE if drop_first_tile else lp, d), F32),
        scratch_shapes=[pltpu.VMEM((TILE + 8, 1536), F32), pltpu.VMEM((GDN_HEADS, GDN_DK, GDN_DK), F32),
                        pltpu.VMEM((TILE + 8, 1024), F32), pltpu.VMEM((M2_GROUPS, M2_DSTATE, 256), F32),
                        pltpu.VMEM((HG_HEADS, HG_DK, HG_DK), F32),
                        pltpu.VMEM((TILE, BRANCH_W), BF), pltpu.VMEM((TILE, BRANCH_W), BF),
                        pltpu.VMEM((TILE, BRANCH_W), BF), pltpu.VMEM((TILE, N_BRANCH * D_MODEL), F32)],
        compiler_params=_cparams(),
        name="mix_merge",
    )(h, p, p, p, p, p, p, yd, *gdn_prm, *ssd_prm, *hg_prm, msk, tri, wg, wb, wo, g.reshape(1, d), b.reshape(1, d))


W_IN_SIZES = (1536, 512, 4, 4, 1024, 512, 8, 512, 512, 512, 512, 512, 512, 4096)
W_IN_OFF = tuple(itertools.accumulate((0,) + W_IN_SIZES))
W_RUNS = ((0, 2), (4, 5), (5, 6), (7, 11), (11, 13), (13, 14))
W_ROWS = 256


def _prep_w_kernel(w_ref, *o_refs):
    for (a, b), o_ref in zip(W_RUNS, o_refs):
        x = w_ref[:, W_IN_OFF[a]:W_IN_OFF[b]]
        if (a, b) == W_RUNS[-1]:
            x = 0.5 * x
        o_ref[...] = x.astype(BF)


def _prep_w(w):
    nl, d, n_in = w.shape
    widths = [W_IN_OFF[b] - W_IN_OFF[a] for a, b in W_RUNS]
    runs = pl.pallas_call(
        _prep_w_kernel,
        grid=(nl, d // W_ROWS),
        in_specs=[pl.BlockSpec((None, W_ROWS, n_in), lambda l, r: (l, r, 0))],
        out_specs=[pl.BlockSpec((None, W_ROWS, n), lambda l, r: (l, r, 0)) for n in widths],
        out_shape=[jax.ShapeDtypeStruct((nl, d, n), BF) for n in widths],
        compiler_params=_cparams(),
        name="prep_w",
    )(w)
    cut = lambda a, b: w[..., W_IN_OFF[a]:W_IN_OFF[b]]
    ba = jnp.concatenate([cut(2, 4), jnp.zeros(w.shape[:-1] + (120,), w.dtype)], axis=-1).astype(BF)
    mdt_e = jnp.repeat(cut(6, 7), M2_HEADDIM, axis=-1).astype(BF)
    qkv_gz, xbc, mz, hg, s5, gates = runs
    return (qkv_gz, xbc, mz, mdt_e, hg, ba, s5), gates


def _lane_row(vals, off, width=128):
    return jnp.zeros((width,), F32).at[off:off + vals.shape[0]].set(vals.astype(F32))


def _prep_s5(a_re, a_im, b_re, b_im, c_re, c_im, log_dt):
    f = lambda v: v.astype(F32)
    a_re, a_im, b_re, b_im, c_re, c_im = map(f, (a_re, a_im, b_re, b_im, c_re, c_im))
    dt = jnp.exp(f(log_dt))[:, None]

    def lam_pow(k):
        mag = jnp.exp(a_re * dt * k)
        return mag * jnp.cos(a_im * dt * k), mag * jnp.sin(a_im * dt * k)

    lam_re, lam_im = lam_pow(1.0)
    den = jnp.square(a_re) + jnp.square(a_im)
    nr, ni = lam_re - 1.0, lam_im
    z_re, z_im = (nr * a_re + ni * a_im) / den, (ni * a_re - nr * a_im) / den
    bb_re = z_re[..., None] * b_re - z_im[..., None] * b_im
    bb_im = z_re[..., None] * b_im + z_im[..., None] * b_re
    ks = jnp.arange(S5_CK, dtype=F32)[:, None, None]
    pr, pi = lam_pow(ks)
    p1r, p1i = lam_pow(ks + 1.0)
    mr, mi = lam_pow(S5_CK * (ks + 1.0))

    def c_times(qr, qi):
        return (c_re[None] * qr[:, :, None, :] - c_im[None] * qi[:, :, None, :],
                c_re[None] * qi[:, :, None, :] + c_im[None] * qr[:, :, None, :])

    cl_re, cl_im = c_times(pr, pi)
    kd = (jnp.einsum('dgcp,gpe->dgce', cl_re, bb_re) - jnp.einsum('dgcp,gpe->dgce', cl_im, bb_im))
    s_i = jnp.arange(S5_CK)[:, None]
    t_i = jnp.arange(S5_CK)[None, :]
    kt = jnp.where((t_i >= s_i)[:, :, None, None, None], kd[jnp.clip(t_i - s_i, 0, S5_CK - 1)], 0.0)
    toep = jnp.transpose(kt, (2, 0, 4, 1, 3)).reshape(S5_NG, 128, 128)
    rr, ri = pr[::-1], pi[::-1]
    w_re = rr[..., None] * bb_re[None] - ri[..., None] * bb_im[None]
    w_im = rr[..., None] * bb_im[None] + ri[..., None] * bb_re[None]
    to_rows = lambda w: jnp.transpose(w, (1, 0, 3, 2)).reshape(S5_NG, 128, S5_P)
    m1 = jnp.concatenate([toep, to_rows(w_re), to_rows(w_im)], axis=2).astype(BF)
    q_re, q_im = c_times(p1r, p1i)
    to_cols = lambda v: jnp.transpose(v, (1, 3, 0, 2)).reshape(S5_NG, S5_P, 128)
    q = jnp.concatenate([to_cols(q_re), -to_cols(q_im)], axis=1).astype(BF)
    pack1 = lambda v: jnp.concatenate([v, v], axis=-1).reshape(v.shape[0], S5_LANES)
    pack2 = lambda v: jnp.concatenate([-v, v], axis=-1).reshape(v.shape[0], S5_LANES)
    row = jnp.arange(8)[:, None]
    tabs = [pack1(mr), pack2(mi)]
    for dd in (1, 2, 4):
        tabs.append(jnp.where(row >= dd, pack1(mr[dd - 1:dd]), 0.0))
        tabs.append(jnp.where(row >= dd, pack2(mi[dd - 1:dd]), 0.0))
    lane_piece = jnp.arange(128) // 16
    lm = jnp.stack([(lane_piece & dd) != 0 for dd in (4, 2, 1)] + [jnp.zeros((128,), bool)] * 5).astype(F32)
    return m1, q, jnp.concatenate(tabs, axis=0), lm


@jax.jit
def kernel(x, meta_tokens, ln_in_g, ln_in_b, w_in, gdn_conv_w, gdn_A_log, gdn_dt_bias, gdn_norm_w, m2_conv_w, m2_conv_b, m2_dt_bias, m2_A_log, m2_D, m2_norm_w, hg_lb_logits, hg_norm_w, s5_A_re, s5_A_im, s5_B_re, s5_B_im, s5_C_re, s5_C_im, s5_D, s5_log_dt, s5_glu_w1, s5_glu_w2, w_branch, w_out, ln_g, ln_b):
    bsz, seq, d = x.shape
    ltot = N_META + seq
    lp = -(-ltot // TILE) * TILE
    pad = lp - ltot
    assert pad + N_META == TILE, "real tokens must start on the second tile"
    meta_tile = jnp.concatenate([jnp.zeros((pad, d), F32), meta_tokens.astype(F32)], axis=0)
    h = _ln_in_call(x.astype(F32), meta_tile, ln_in_g.astype(F32), ln_in_b.astype(F32), pad)

    msk = _masks64()
    tri = msk[0].astype(BF)
    ones = jnp.ones((CH, CH), BF)
    jj = jnp.arange(256)
    eyet = (jnp.arange(CH)[:, None] == (jj % CH)[None, :]).astype(F32)
    maskt = (jnp.arange(CH)[:, None] >= (jj % CH)[None, :]).astype(F32)
    bd = ((jj // CH)[:, None] == (jj // CH)[None, :]).astype(F32)

    cum = jnp.cumsum(jax.nn.softmax(hg_lb_logits.astype(F32), axis=0), axis=0)
    lower_bounds = cum - cum[0:1]

    w_pieces, wg_all = _prep_w(w_in)
    m1_all, q_all, pw_all, lm_all = jax.vmap(_prep_s5)(s5_A_re, s5_A_im, s5_B_re, s5_B_im, s5_C_re, s5_C_im,
                                                       s5_log_dt)
    glu1_all, glu2_all = s5_glu_w1.astype(BF), s5_glu_w2.astype(BF)
    wb_all, wo_all = (0.5 * w_branch).astype(BF), w_out.astype(BF)

    s5_d_all = s5_D.astype(F32)[:, None, :]
    for l in range(w_in.shape[0]):
        p, yd = _proj_s5_call(h, l, w_pieces, m1_all, q_all, pw_all, lm_all, s5_d_all, glu1_all, glu2_all)

        lane = jnp.stack([_lane_row(jnp.exp(gdn_A_log[l].astype(F32)), 4), _lane_row(gdn_dt_bias[l], 4)]
                         + [jnp.zeros((128,), F32)] * 6)
        gdn_prm = (gdn_conv_w[l].astype(F32), lane, gdn_norm_w[l].astype(F32).reshape(1, GDN_DK))

        rep = lambda v: jnp.repeat(v.astype(F32), M2_HEADDIM)
        vec = jnp.stack([rep(m2_dt_bias[l]), rep(-jnp.exp(m2_A_log[l].astype(F32))), rep(m2_D[l]),
                         m2_norm_w[l].astype(F32)] + [jnp.zeros((BRANCH_W,), F32)] * 4)
        ssd_prm = (m2_conv_w[l].astype(F32), m2_conv_b[l].astype(F32).reshape(1, -1), vec, eyet, maskt, bd, ones)

        lb = lower_bounds[l]
        hvec = jnp.stack([jnp.log(lb), jnp.log1p(-lb), 1.0 - lb] + [jnp.zeros((BRANCH_W,), F32)] * 5)
        hg_prm = (hvec, hg_norm_w[l].astype(F32).reshape(1, HG_DK))

        h = _mix_merge_call(h, l, p, yd, gdn_prm, ssd_prm, hg_prm, msk, tri, wg_all, wb_all,
                            wo_all, ln_g[l].astype(F32), ln_b[l].astype(F32), pad,
                            drop_first_tile=(l == w_in.shape[0] - 1))
    return h.astype(x.dtype)
```

```python
import functools
import itertools
import math

import jax
import jax.numpy as jnp
from jax import lax
from jax.experimental import pallas as pl
from jax.experimental.pallas import tpu as pltpu

F32 = jnp.float32
BF = jnp.bfloat16

D_MODEL = 1024
DEPTH = 4
N_META = 16
CONV_K = 4
N_BRANCH = 4
BRANCH_W = 512
GDN_HEADS = 4
GDN_DK = 128
M2_HEADS = 8
M2_HEADDIM = 64
M2_GROUPS = 2
M2_DSTATE = 128
HG_HEADS = 4
HG_DK = 128
S5_GROUP = 16
S5_NG = 32
S5_P = 64
S5_STATE = S5_NG * S5_P
ALPHA = (2 * DEPTH) ** 0.25
LN_EPS = 1e-5
RMS_EPS = 1e-6

TILE = 256
CH = 64
NCH = TILE // CH
SUB = 16
HG_EXP_CLAMP = 80.0
VMEM_LIMIT = 56 * 1024 * 1024

OFF_QKV, OFF_GZ, OFF_XBC, OFF_MZ, OFF_MDT = 0, 1536, 2048, 3072, 3584
OFF_HG, OFF_BA, OFF_S5 = 4096, 6144, 6272
NP = 6272
N1 = 7296
W_PIECES = ((0, 2048), (OFF_XBC, 1024), (OFF_MZ, 512), (OFF_MDT, 512), (OFF_HG, 2048), (OFF_BA, 128), (OFF_S5, 1024))


def _sigmoid(x):
    return 0.5 * (jnp.tanh(0.5 * x) + 1.0)


def _silu(x):
    return x * _sigmoid(x)


def _softplus(x):
    return jnp.maximum(x, 0.0) + jnp.log1p(jnp.exp(-jnp.abs(x)))


def _mm(a, b):
    return jnp.dot(a.astype(BF), b.astype(BF), preferred_element_type=F32)


def _mm_nt(a, b):
    return lax.dot_general(a.astype(BF), b.astype(BF), (((1,), (1,)), ((), ())),
                           preferred_element_type=F32)


def _mm_tn(a, b):
    return lax.dot_general(a.astype(BF), b.astype(BF), (((0,), (0,)), ((), ())),
                           preferred_element_type=F32)


def _mm3_left(m_bf, x):
    hi = x.astype(BF)
    r1 = x - hi.astype(F32)
    mid = r1.astype(BF)
    lo = (r1 - mid.astype(F32)).astype(BF)
    d = lambda v: jnp.dot(m_bf, v, preferred_element_type=F32)
    return (d(lo) + d(mid)) + d(hi)


def _rms_norm(x):
    return x * lax.rsqrt(jnp.mean(x * x, axis=-1, keepdims=True) + RMS_EPS)


def _masks64():
    i = jnp.arange(CH)[:, None]
    j = jnp.arange(CH)[None, :]
    ms = [(i >= j), (i > j)]
    for s in (1, 2, 4, 8, 16, 32):
        ms.append((i // (2 * s) == j // (2 * s)) & (i % (2 * s) >= s) & (j % (2 * s) < s))
    ms.append(i == j)
    return jnp.stack(ms).astype(F32)


def _cparams():
    return pltpu.CompilerParams(dimension_semantics=("arbitrary", "arbitrary"),
                                vmem_limit_bytes=VMEM_LIMIT)


def _full(shape):
    n = len(shape)
    return pl.BlockSpec(shape, lambda b, t: (0,) * n, pipeline_mode=pl.Buffered(1))


def _layer_full(shape, l):
    n = len(shape)
    return pl.BlockSpec((None,) + tuple(shape), lambda b, t: (l,) + (0,) * n, pipeline_mode=pl.Buffered(1))


def _ln_in_kernel(x_ref, m_ref, g_ref, b_ref, o_ref, *, pad):
    t = pl.program_id(1)
    x = jnp.where(t == 0, m_ref[...], x_ref[...])
    mu = jnp.mean(x, axis=-1, keepdims=True)
    xc = x - mu
    var = jnp.mean(xc * xc, axis=-1, keepdims=True)
    y = xc * lax.rsqrt(var + LN_EPS) * g_ref[...] + b_ref[...]
    row = t * TILE + lax.broadcasted_iota(jnp.int32, (TILE, 1), 0)
    o_ref[...] = jnp.where(row >= pad, y, 0.0)


def _ln_in_call(x, meta_tile, g, b, pad):
    bsz, seq, d = x.shape
    return pl.pallas_call(
        functools.partial(_ln_in_kernel, pad=pad),
        grid=(bsz, seq // TILE + 1),
        in_specs=[pl.BlockSpec((None, TILE, d), lambda b_, t: (b_, jnp.maximum(t - 1, 0), 0)),
                  _full((TILE, d)), _full((1, d)), _full((1, d))],
        out_specs=pl.BlockSpec((None, TILE, d), lambda b_, t: (b_, t, 0)),
        out_shape=jax.ShapeDtypeStruct((bsz, seq + TILE, d), F32),
        compiler_params=_cparams(),
        name="ln_in",
    )(x, meta_tile, g.reshape(1, d), b.reshape(1, d))


S5_CK = 8
S5_NCK = TILE // S5_CK
S5_PACK = 2 * S5_P
S5_LANES = S5_NG * S5_PACK
PROJ_CHUNK = 512
GATE_CHUNK = 256
MIX_PLAN = "gsh" + "gfgfgfgfh" * 4 + "gs" * 8 + "gg" + "hDhA" + "hhBhh" + "C"


def _transpose_pieces(arrs, lm_ref):
    for k, d in enumerate((4, 2, 1)):
        m = lm_ref[k:k + 1, :] > 0
        new = list(arrs)
        for s_ in range(8):
            if s_ & d == 0:
                lo, hi = arrs[s_], arrs[s_ + d]
                new[s_] = jnp.where(m, pltpu.roll(hi, 16 * d, 1), lo)
                new[s_ + d] = jnp.where(m, hi, pltpu.roll(lo, 128 - 16 * d, 1))
        arrs = new
    return arrs


def _s5_scan_block(b, cs, carry, tabs, st_ref):
    rows = slice(8 * b, 8 * b + 8)
    v = st_ref[rows, cs]
    for n, d in enumerate((1, 2, 4)):
        vsh = pltpu.roll(v, d, 0)
        v = v + (tabs[2 + 2 * n] * vsh + tabs[3 + 2 * n] * pltpu.roll(vsh, S5_P, 1))
    cb = jnp.broadcast_to(carry, (8, S5_PACK))
    v = v + (tabs[0] * cb + tabs[1] * pltpu.roll(cb, S5_P, 1))
    st_ref[rows, cs] = v
    return v[7:8, :]


def _proj_s5_kernel(h_ref, w0_ref, w1_ref, w2_ref, w3_ref, w4_ref, w5_ref, w6_ref,
                    m1_ref, q_ref, pw_ref, lm_ref, d_ref, g1_ref, g2_ref,
                    p_ref, yd_ref, u_ref, ug_ref, yq_ref, st_ref, y_ref, c_ref):
    t = pl.program_id(1)

    @pl.when(t == 0)
    def _():
        c_ref[...] = jnp.zeros_like(c_ref)

    def wcols(j, w):
        for (start, width), ref in zip(W_PIECES, (w0_ref, w1_ref, w2_ref, w3_ref, w4_ref, w5_ref, w6_ref)):
            if start <= j and j + w <= start + width:
                return ref[:, j - start:j - start + w]
        raise ValueError((j, w))

    hb = h_ref[...].astype(BF)
    u = jnp.dot(hb, wcols(OFF_S5, BRANCH_W), preferred_element_type=F32)
    for j in range(4):
        u_ref[j] = u[:, j * 128:(j + 1) * 128]

    starts = iter(range(0, NP, PROJ_CHUNK))

    def proj_chunks(n):
        for j in itertools.islice(starts, n):
            w = min(PROJ_CHUNK, NP - j)
            p_ref[:, j:j + w] = jnp.dot(hb, wcols(j, w), preferred_element_type=F32)

    for j in range(4):
        proj_chunks(1)
        arrs = [u_ref[j, pl.ds(s_, S5_NCK, stride=S5_CK), :] for s_ in range(S5_CK)]
        arrs = _transpose_pieces(arrs, lm_ref)
        for i in range(8):
            g = 8 * j + i
            ug_ref[:, g * 128:(g + 1) * 128] = arrs[i].astype(BF)
    row0 = lax.broadcasted_iota(jnp.int32, (S5_NCK, S5_PACK), 0) == 0
    last_inc = []
    for g in range(S5_NG):
        if g % 11 == 10:
            proj_chunks(1)
        cs = slice(g * S5_PACK, (g + 1) * S5_PACK)
        tw = jnp.dot(ug_ref[:, cs], m1_ref[g], preferred_element_type=F32)
        yq_ref[:, cs] = tw[:, :S5_PACK]
        inc = tw[:, S5_PACK:]
        last_inc.append(inc[S5_NCK - 1:S5_NCK, :])
        st_ref[:, cs] = jnp.where(row0, c_ref[:, cs], pltpu.roll(inc, 1, 0))

    for g in range(S5_NG):
        if g % 8 == 0:
            proj_chunks(1)
        cs = slice(g * S5_PACK, (g + 1) * S5_PACK)
        tabs = [pw_ref[8 * k:8 * k + 8, cs] for k in range(8)]
        x = jnp.zeros((1, S5_PACK), F32)
        for b in range(S5_NCK // 8):
            x = _s5_scan_block(b, cs, x, tabs, st_ref)
        c_ref[:, cs] = last_inc[g] + (tabs[0][0:1] * x + tabs[1][0:1] * pltpu.roll(
            jnp.broadcast_to(x, (8, S5_PACK)), S5_P, 1)[0:1])

    for g in range(S5_NG):
        if g % 16 == 0:
            proj_chunks(1)
        cs = slice(g * S5_PACK, (g + 1) * S5_PACK)
        yq_ref[:, cs] = yq_ref[:, cs] + _mm(st_ref[:, cs], q_ref[g])
    proj_chunks(NP)
    for j in range(4):
        arrs = _transpose_pieces([yq_ref[:, (8 * j + i) * 128:(8 * j + i + 1) * 128] for i in range(8)], lm_ref)
        for s_ in range(S5_CK):
            y_ref[j, pl.ds(s_, S5_NCK, stride=S5_CK), :] = arrs[s_]
    y = jnp.concatenate([y_ref[j] for j in range(4)], axis=1) + d_ref[...] * u
    y = 0.5 * y * (1.0 + jnp.tanh(math.sqrt(2.0 / math.pi) * (y + 0.044715 * (y * y * y))))
    yb = y.astype(BF)
    glu = jnp.dot(yb, g1_ref[...], preferred_element_type=F32) * _sigmoid(
        jnp.dot(yb, g2_ref[...], preferred_element_type=F32))
    z = jnp.dot(hb, wcols(OFF_S5 + BRANCH_W, BRANCH_W), preferred_element_type=F32)
    yd_ref[...] = (glu * _silu(z)).astype(yd_ref.dtype)


def _proj_s5_call(h, l, w_pieces, m1, q, pw, lm, dvec, g1, g2):
    bsz, lp, d = h.shape
    return pl.pallas_call(
        _proj_s5_kernel,
        grid=(bsz, lp // TILE),
        in_specs=[pl.BlockSpec((None, TILE, d), lambda b_, t: (b_, t, 0))]
        + [_layer_full((d, width), l) for _, width in W_PIECES]
        + [_layer_full((S5_NG, 128, 256), l), _layer_full((S5_NG, 128, 128), l), _layer_full((64, S5_LANES), l),
           _layer_full((8, 128), l), _layer_full((1, 512), l), _layer_full((512, 512), l),
           _layer_full((512, 512), l)],
        out_specs=[pl.BlockSpec((None, TILE, NP), lambda b_, t: (b_, t, 0)),
                   pl.BlockSpec((None, TILE, BRANCH_W), lambda b_, t: (b_, t, 0))],
        out_shape=[jax.ShapeDtypeStruct((bsz, lp, NP), F32), jax.ShapeDtypeStruct((bsz, lp, BRANCH_W), BF)],
        scratch_shapes=[pltpu.VMEM((4, TILE, 128), F32), pltpu.VMEM((S5_NCK, S5_LANES), BF),
                        pltpu.VMEM((S5_NCK, S5_LANES), F32), pltpu.VMEM((S5_NCK, S5_LANES), F32),
                        pltpu.VMEM((4, TILE, 128), F32), pltpu.VMEM((1, S5_LANES), F32)],
        compiler_params=_cparams(),
        name="proj_s5",
    )(h, *w_pieces, m1, q, pw, lm, dvec, g1, g2)


def _conv_chunk(xp_ref, cw, r0):
    x = xp_ref[r0 + 8:r0 + 8 + CH, :] * cw[CONV_K - 1:CONV_K, :]
    for j in range(CONV_K - 1):
        s = r0 + 8 - (CONV_K - 1) + j
        x = x + xp_ref[s:s + CH, :] * cw[j:j + 1, :]
    return x


def _gdn_steps(qkv_ref, z_ref, ba_ref, cw_ref, lane_ref, nw_ref, msk_ref, tri_ref, y_ref, xp_ref, s_ref):
    t = pl.program_id(1)

    @pl.when(t == 0)
    def _():
        xp_ref[0:8, :] = jnp.zeros((8, 3 * BRANCH_W), F32)
        s_ref[...] = jnp.zeros_like(s_ref)

    xp_ref[8:8 + TILE, :] = qkv_ref[...]
    yield
    cw = cw_ref[...]
    exp_a = lane_ref[0:1, :]
    dt_bias = lane_ref[1:2, :]
    nw = nw_ref[...]
    causal = msk_ref[0]
    strict = msk_ref[1]
    tri = tri_ref[...]
    items = []
    a_list = []
    for c in range(NCH):
        r0 = c * CH
        qkv = _silu(_conv_chunk(xp_ref, cw, r0))
        ba = ba_ref[r0:r0 + CH, :]
        beta_all = _sigmoid(ba)
        gl = -(exp_a * _softplus(ba + dt_bias))
        gc = _mm3_left(tri, gl)
        gc_t = gc.T
        for h in range(GDN_HEADS):
            sl = slice(h * GDN_DK, (h + 1) * GDN_DK)
            q = qkv[:, sl]
            k = qkv[:, BRANCH_W + h * GDN_DK:BRANCH_W + (h + 1) * GDN_DK]
            v = qkv[:, 2 * BRANCH_W + h * GDN_DK:2 * BRANCH_W + (h + 1) * GDN_DK]
            q = q * (lax.rsqrt(jnp.sum(q * q, axis=-1, keepdims=True) + RMS_EPS) * GDN_DK ** -0.5)
            k = k * lax.rsqrt(jnp.sum(k * k, axis=-1, keepdims=True) + RMS_EPS)
            beta = beta_all[:, h:h + 1]
            gcol = gc[:, 4 + h:5 + h]
            grow = gc_t[4 + h:5 + h, :]
            decay = jnp.exp(jnp.where(causal > 0, gcol - grow, -jnp.inf))
            kb = k * beta
            eg = jnp.exp(gcol)
            glast = gc[CH - 1:CH, 4 + h:5 + h]
            a_list.append(strict * (_mm_nt(kb, k) * decay))
            items.append(dict(
                rhs=jnp.concatenate([v * beta, kb * eg], axis=1),
                qk=causal * (_mm_nt(q, k) * decay),
                qd=q * eg,
                kd=k * jnp.exp(glast - gcol),
                gtot=jnp.exp(glast)))
            yield
    tinvs = [msk_ref[8] - a * msk_ref[2] for a in a_list]
    for lvl in range(1, 6):
        m = msk_ref[2 + lvl]
        t1 = [_mm(x, a * m) for x, a in zip(tinvs, a_list)]
        tinvs = [x - _mm(t_, x) for x, t_ in zip(tinvs, t1)]
        yield
    uws = [_mm(tinv, it["rhs"]) for tinv, it in zip(tinvs, items)]
    kd_uw = [_mm_tn(it["kd"], uw) for it, uw in zip(items, uws)]
    qk_uw = [_mm(it["qk"], uw) for it, uw in zip(items, uws)]
    yield
    states = [s_ref[h] for h in range(GDN_HEADS)]
    for c in range(NCH):
        r0 = c * CH
        for h in range(GDN_HEADS):
            sl = slice(h * GDN_DK, (h + 1) * GDN_DK)
            n = c * GDN_HEADS + h
            s = states[h]
            o = _mm(items[n]["qd"] - qk_uw[n][:, GDN_DK:], s) + qk_uw[n][:, :GDN_DK]
            states[h] = (s * items[n]["gtot"] - _mm(kd_uw[n][:, GDN_DK:], s)) + kd_uw[n][:, :GDN_DK]
            y = _rms_norm(o) * nw * _silu(z_ref[r0:r0 + CH, sl])
            y_ref[r0:r0 + CH, sl] = y.astype(y_ref.dtype)
        if c == NCH - 1:
            for h in range(GDN_HEADS):
                s_ref[h] = states[h]
            xp_ref[0:8, :] = xp_ref[TILE:TILE + 8, :]
        yield


def _ssd_steps(xbc_ref, zdt_ref, cw_ref, cb_ref, vec_ref, eyet_ref, maskt_ref, bd_ref, tri_ref, ones_ref,
               y_ref, xp_ref, s_ref, *, pad):
    t = pl.program_id(1)

    @pl.when(t == 0)
    def _():
        xp_ref[0:8, :] = jnp.zeros((8, 1024), F32)
        s_ref[...] = jnp.zeros_like(s_ref)

    xp_ref[8:8 + TILE, :] = xbc_ref[...]
    yield
    cw = cw_ref[...]
    cb = cb_ref[...]
    dt_bias = vec_ref[0:1, :]
    neg_a = vec_ref[1:2, :]
    d_skip = vec_ref[2:3, :]
    nw = vec_ref[3:4, :]
    tri = tri_ref[...]
    ones = ones_ref[...]
    gw = M2_HEADDIM * (M2_HEADS // M2_GROUPS)
    for c in range(NCH):
        r0 = c * CH
        xbc = _silu(_conv_chunk(xp_ref, cw, r0) + cb)
        xs = xbc[:, :BRANCH_W]
        row = t * TILE + r0 + lax.broadcasted_iota(jnp.int32, (CH, 1), 0)
        valid = (row >= pad).astype(F32)
        dt = _softplus(zdt_ref[r0:r0 + CH, BRANCH_W:] + dt_bias) * valid
        a = dt * neg_a
        acum = _mm3_left(tri, a)
        xdt = xs * dt
        z = zdt_ref[r0:r0 + CH, :BRANCH_W]
        for g in range(M2_GROUPS):
            gs = slice(g * gw, (g + 1) * gw)
            bg = xbc[:, BRANCH_W + g * M2_DSTATE:BRANCH_W + (g + 1) * M2_DSTATE]
            cg = xbc[:, BRANCH_W + M2_GROUPS * M2_DSTATE + g * M2_DSTATE:
                     BRANCH_W + M2_GROUPS * M2_DSTATE + (g + 1) * M2_DSTATE]
            ag = acum[:, gs]
            cbt = _mm_nt(cg, jnp.concatenate([bg] * 4, axis=0))
            arow = _mm3_left(ones, ag * eyet_ref[...])
            lm = jnp.exp(jnp.where(maskt_ref[...] > 0, ag - arow, -jnp.inf))
            xg = xdt[:, gs]
            xbd = jnp.concatenate([xg] * 4, axis=0) * bd_ref[...]
            y_diag = _mm(cbt * lm, xbd)
            alast = ag[CH - 1:CH, :]
            s = s_ref[g]
            y_off = _mm(cg, s) * jnp.exp(ag)
            s_ref[g] = s * jnp.exp(alast) + _mm_tn(bg, xg * jnp.exp(alast - ag))
            y = (y_diag + y_off + d_skip[:, gs] * xs[:, gs]) * _silu(z[:, gs])
            y = _rms_norm(y) * nw[:, gs]
            y_ref[r0:r0 + CH, gs] = y.astype(y_ref.dtype)
            if c == NCH - 1 and g == M2_GROUPS - 1:
                xp_ref[0:8, :] = xp_ref[TILE:TILE + 8, :]
            yield


def _hgrn_steps(p_ref, vec_ref, nw_ref, msk_ref, tri_ref, y_ref, s_ref):
    t = pl.program_id(1)

    @pl.when(t == 0)
    def _():
        s_ref[...] = jnp.zeros_like(s_ref)

    yield
    log_lb = vec_ref[0:1, :]
    log1m_lb = vec_ref[1:2, :]
    one_m_lb = vec_ref[2:3, :]
    nw = nw_ref[...]
    causal = msk_ref[0]
    tri = tri_ref[...]
    hsl = [slice(h * HG_DK, (h + 1) * HG_DK) for h in range(HG_HEADS)]
    nsub = CH // SUB
    prep = []
    for c in range(NCH):
        r0 = c * CH
        q = _silu(p_ref[r0:r0 + CH, 0:512])
        zf = p_ref[r0:r0 + CH, 512:1024]
        e = jnp.exp(-jnp.abs(zf))
        lsig = jnp.minimum(zf, 0.0) - jnp.log1p(e)
        bb = log1m_lb + lsig
        logf = jnp.maximum(log_lb, bb) + jnp.log1p(jnp.exp(-jnp.abs(log_lb - bb)))
        sig_neg = jnp.where(zf >= 0, e, 1.0) / (1.0 + e)
        k = one_m_lb * sig_neg
        gcum = _mm3_left(tri, logf)
        glast = gcum[CH - 1:CH, :]
        subs = []
        for i in range(nsub):
            rs = slice(i * SUB, (i + 1) * SUB)
            n = (i + 1) * SUB
            gref = gcum[i * SUB - 1:i * SUB, :] if i > 0 else jnp.zeros((1, BRANCH_W), F32)
            qt = q[rs] * jnp.exp(gcum[rs] - gref)
            kt = k[:n] * jnp.exp(jnp.minimum(gref - gcum[:n], HG_EXP_CLAMP))
            subs.append((qt, kt, qt * jnp.exp(gref)))
        prep.append(dict(kd=k * jnp.exp(glast - gcum), eg=jnp.exp(glast), subs=subs))
        yield
    incs = [[_mm_tn(p_ref[c * CH:(c + 1) * CH, 1024 + h * HG_DK:1024 + (h + 1) * HG_DK], prep[c]["kd"][:, hsl[h]])
             for h in range(HG_HEADS)] for c in range(NCH)]
    states = [[s_ref[h] for h in range(HG_HEADS)]]
    for c in range(NCH):
        states.append([states[c][h] * prep[c]["eg"][:, hsl[h]] + incs[c][h] for h in range(HG_HEADS)])
    for h in range(HG_HEADS):
        s_ref[h] = states[NCH][h]
    yield
    ams = [[[_mm_nt(prep[c]["subs"][i][0][:, hsl[h]], prep[c]["subs"][i][1][:, hsl[h]])
             * causal[i * SUB:(i + 1) * SUB, :(i + 1) * SUB]
             for h in range(HG_HEADS)] for i in range(nsub)] for c in range(NCH)]
    yield
    for c in range(NCH):
        r0 = c * CH
        for h in range(HG_HEADS):
            iv = p_ref[r0:r0 + CH, 1024 + h * HG_DK:1024 + (h + 1) * HG_DK]
            o = jnp.concatenate(
                [_mm(ams[c][i][h], iv[:(i + 1) * SUB]) + _mm_nt(prep[c]["subs"][i][2][:, hsl[h]], states[c][h])
                 for i in range(nsub)], axis=0)
            z = p_ref[r0:r0 + CH, 1536 + h * HG_DK:1536 + (h + 1) * HG_DK]
            y_ref[r0:r0 + CH, hsl[h]] = (_rms_norm(o) * nw * _silu(z)).astype(y_ref.dtype)
        yield


def _mix_merge_kernel(h_ref, qkv_ref, gz_ref, ba_ref, xbc_ref, zdt_ref, hg_ref, yd_ref,
                      gcw_ref, glane_ref, gnw_ref, scw_ref, scb_ref, svec_ref, eyet_ref, maskt_ref, bd_ref, ones_ref,
                      hvec_ref, hnw_ref, msk_ref, tri_ref, wg_ref, wb_ref, wo_ref, g_ref, b_ref,
                      o_ref, gxp_ref, gs_ref, sxp_ref, ss_ref, hs_ref, ya_ref, yb_ref, yc_ref, gate_ref, *, pad):
    t = pl.program_id(1)
    h = h_ref[...]
    hb = h.astype(BF)

    def gate_piece(j):
        cs = slice(j * GATE_CHUNK, (j + 1) * GATE_CHUNK)
        gate_ref[:, cs] = jnp.tanh(jnp.dot(hb, wg_ref[:, cs], preferred_element_type=F32)) + 1.0

    def branch_piece(br, y_ref):
        cs = slice(br * D_MODEL, (br + 1) * D_MODEL)
        gate_ref[:, cs] = gate_ref[:, cs] * jnp.dot(y_ref[...], wb_ref[br], preferred_element_type=F32)

    gdn = _gdn_steps(qkv_ref, gz_ref, ba_ref, gcw_ref, glane_ref, gnw_ref, msk_ref, tri_ref, ya_ref, gxp_ref, gs_ref)
    ssd = _ssd_steps(xbc_ref, zdt_ref, scw_ref, scb_ref, svec_ref, eyet_ref, maskt_ref, bd_ref, tri_ref, ones_ref,
                     yb_ref, sxp_ref, ss_ref, pad=pad)
    hgrn = _hgrn_steps(hg_ref, hvec_ref, hnw_ref, msk_ref, tri_ref, yc_ref, hs_ref)
    gates = iter([functools.partial(gate_piece, j) for j in range(N_BRANCH * D_MODEL // GATE_CHUNK)])
    emit = {"g": lambda: next(gdn), "s": lambda: next(ssd), "h": lambda: next(hgrn), "f": lambda: next(gates)(),
            "A": lambda: branch_piece(0, ya_ref), "B": lambda: branch_piece(1, yb_ref),
            "C": lambda: branch_piece(2, yc_ref), "D": lambda: branch_piece(3, yd_ref)}
    for tok in MIX_PLAN:
        emit[tok]()
    mixed = ((gate_ref[:, 0:D_MODEL] + gate_ref[:, D_MODEL:2 * D_MODEL])
             + (gate_ref[:, 2 * D_MODEL:3 * D_MODEL] + gate_ref[:, 3 * D_MODEL:4 * D_MODEL]))
    out = jnp.dot(mixed.astype(BF), wo_ref[...], preferred_element_type=F32)
    r = ALPHA * h + out
    mu = jnp.mean(r, axis=-1, keepdims=True)
    rc = r - mu
    var = jnp.mean(rc * rc, axis=-1, keepdims=True)
    y = rc * lax.rsqrt(var + LN_EPS) * g_ref[...] + b_ref[...]
    row = t * TILE + lax.broadcasted_iota(jnp.int32, (TILE, 1), 0)
    o_ref[...] = jnp.where(row >= pad, y, 0.0)


def _mix_merge_call(h, l, p, yd, gdn_prm, ssd_prm, hg_prm, msk, tri, wg, wb, wo, g, b, pad, drop_first_tile):
    bsz, lp, d = h.shape
    tok = lambda w, j=0: pl.BlockSpec((None, TILE, w), lambda b_, t: (b_, t, j))
    return pl.pallas_call(
        functools.partial(_mix_merge_kernel, pad=pad),
        grid=(bsz, lp // TILE),
        in_specs=[tok(d), tok(1536, OFF_QKV // 1536), tok(512, OFF_GZ // 512), tok(128, OFF_BA // 128),
                  tok(1024, OFF_XBC // 1024), tok(1024, OFF_MZ // 1024), tok(2048, OFF_HG // 2048), tok(BRANCH_W),
                  _full((CONV_K, 1536)), _full((8, 128)), _full((1, GDN_DK)),
                  _full((CONV_K, 1024)), _full((1, 1024)), _full((8, 512)),
                  _full((CH, 256)), _full((CH, 256)), _full((256, 256)), _full((CH, CH)),
                  _full((8, 512)), _full((1, HG_DK)), _full((9, CH, CH)), _full((CH, CH)),
                  _layer_full((d, N_BRANCH * d), l), _layer_full((N_BRANCH, BRANCH_W, d), l),
                  _layer_full((d, d), l), _full((1, d)), _full((1, d))],
        out_specs=(pl.BlockSpec((None, TILE, d), lambda b_, t: (b_, jnp.maximum(t - 1, 0), 0))
                   if drop_first_tile else tok(d)),
        out_shape=jax.ShapeDtypeStruct((bsz, lp - TILE if drop_first_tile else lp, d), F32),
        scratch_shapes=[pltpu.VMEM((TILE + 8, 1536), F32), pltpu.VMEM((GDN_HEADS, GDN_DK, GDN_DK), F32),
                        pltpu.VMEM((TILE + 8, 1024), F32), pltpu.VMEM((M2_GROUPS, M2_DSTATE, 256), F32),
                        pltpu.VMEM((HG_HEADS, HG_DK, HG_DK), F32),
                        pltpu.VMEM((TILE, BRANCH_W), BF), pltpu.VMEM((TILE, BRANCH_W), BF),
                        pltpu.VMEM((TILE, BRANCH_W), BF), pltpu.VMEM((TILE, N_BRANCH * D_MODEL), F32)],
        compiler_params=_cparams(),
        name="mix_merge",
    )(h, p, p, p, p, p, p, yd, *gdn_prm, *ssd_prm, *hg_prm, msk, tri, wg, wb, wo, g.reshape(1, d), b.reshape(1, d))


W_IN_SIZES = (1536, 512, 4, 4, 1024, 512, 8, 512, 512, 512, 512, 512, 512, 4096)
W_IN_OFF = tuple(itertools.accumulate((0,) + W_IN_SIZES))
W_RUNS = ((W_IN_OFF[0], W_IN_OFF[2]), (W_IN_OFF[4], W_IN_OFF[5]), (W_IN_OFF[5], W_IN_OFF[6]),
          (W_IN_OFF[7], W_IN_OFF[11]), (W_IN_OFF[11], W_IN_OFF[13]), (W_IN_OFF[13], W_IN_OFF[14]),
          (W_IN_OFF[2], W_IN_OFF[2] + 128), (W_IN_OFF[6] - 8, W_IN_OFF[6] + 120))
W_ROWS = 256


def _prep_w_kernel(w_ref, *o_refs):
    for n, ((a, b), o_ref) in enumerate(zip(W_RUNS, o_refs)):
        x = w_ref[:, a:b]
        if n == 5:
            x = 0.5 * x
        o_ref[...] = x.astype(BF)


def _prep_w(w):
    nl, d, n_in = w.shape
    widths = [b - a for a, b in W_RUNS]
    qkv_gz, xbc, mz, hg, s5, gates, win_ba, win_dt = pl.pallas_call(
        _prep_w_kernel,
        grid=(nl, d // W_ROWS),
        in_specs=[pl.BlockSpec((None, W_ROWS, n_in), lambda l, r: (l, r, 0))],
        out_specs=[pl.BlockSpec((None, W_ROWS, n), lambda l, r: (l, r, 0)) for n in widths],
        out_shape=[jax.ShapeDtypeStruct((nl, d, n), BF) for n in widths],
        compiler_params=_cparams(),
        name="prep_w",
    )(w)
    ba = jnp.where(jnp.arange(128) < 8, win_ba, jnp.zeros((), BF))
    mdt_e = jnp.repeat(win_dt[..., 8:8 + M2_HEADS], M2_HEADDIM, axis=-1)
    return (qkv_gz, xbc, mz, mdt_e, hg, ba, s5), gates


def _lane_row(vals, off, width=128):
    return jnp.zeros((width,), F32).at[off:off + vals.shape[0]].set(vals.astype(F32))


def _prep_s5(a_re, a_im, b_re, b_im, c_re, c_im, log_dt):
    f = lambda v: v.astype(F32)
    a_re, a_im, b_re, b_im, c_re, c_im = map(f, (a_re, a_im, b_re, b_im, c_re, c_im))
    dt = jnp.exp(f(log_dt))[:, None]

    def lam_pow(k):
        mag = jnp.exp(a_re * dt * k)
        return mag * jnp.cos(a_im * dt * k), mag * jnp.sin(a_im * dt * k)

    lam_re, lam_im = lam_pow(1.0)
    den = jnp.square(a_re) + jnp.square(a_im)
    nr, ni = lam_re - 1.0, lam_im
    z_re, z_im = (nr * a_re + ni * a_im) / den, (ni * a_re - nr * a_im) / den
    bb_re = z_re[..., None] * b_re - z_im[..., None] * b_im
    bb_im = z_re[..., None] * b_im + z_im[..., None] * b_re
    ks = jnp.arange(S5_CK, dtype=F32)[:, None, None]
    pr, pi = lam_pow(ks)
    p1r, p1i = lam_pow(ks + 1.0)
    mr, mi = lam_pow(S5_CK * (ks + 1.0))

    def c_times(qr, qi):
        return (c_re[None] * qr[:, :, None, :] - c_im[None] * qi[:, :, None, :],
                c_re[None] * qi[:, :, None, :] + c_im[None] * qr[:, :, None, :])

    cl_re, cl_im = c_times(pr, pi)
    kd = (jnp.einsum('dgcp,gpe->dgce', cl_re, bb_re) - jnp.einsum('dgcp,gpe->dgce', cl_im, bb_im))
    s_i = jnp.arange(S5_CK)[:, None]
    t_i = jnp.arange(S5_CK)[None, :]
    kt = jnp.where((t_i >= s_i)[:, :, None, None, None], kd[jnp.clip(t_i - s_i, 0, S5_CK - 1)], 0.0)
    toep = jnp.transpose(kt, (2, 0, 4, 1, 3)).reshape(S5_NG, 128, 128)
    rr, ri = pr[::-1], pi[::-1]
    w_re = rr[..., None] * bb_re[None] - ri[..., None] * bb_im[None]
    w_im = rr[..., None] * bb_im[None] + ri[..., None] * bb_re[None]
    to_rows = lambda w: jnp.transpose(w, (1, 0, 3, 2)).reshape(S5_NG, 128, S5_P)
    m1 = jnp.concatenate([toep, to_rows(w_re), to_rows(w_im)], axis=2).astype(BF)
    q_re, q_im = c_times(p1r, p1i)
    to_cols = lambda v: jnp.transpose(v, (1, 3, 0, 2)).reshape(S5_NG, S5_P, 128)
    q = jnp.concatenate([to_cols(q_re), -to_cols(q_im)], axis=1).astype(BF)
    pack1 = lambda v: jnp.concatenate([v, v], axis=-1).reshape(v.shape[0], S5_LANES)
    pack2 = lambda v: jnp.concatenate([-v, v], axis=-1).reshape(v.shape[0], S5_LANES)
    row = jnp.arange(8)[:, None]
    tabs = [pack1(mr), pack2(mi)]
    for dd in (1, 2, 4):
        tabs.append(jnp.where(row >= dd, pack1(mr[dd - 1:dd]), 0.0))
        tabs.append(jnp.where(row >= dd, pack2(mi[dd - 1:dd]), 0.0))
    lane_piece = jnp.arange(128) // 16
    lm = jnp.stack([(lane_piece & dd) != 0 for dd in (4, 2, 1)] + [jnp.zeros((128,), bool)] * 5).astype(F32)
    return m1, q, jnp.concatenate(tabs, axis=0), lm


@jax.jit
def kernel(x, meta_tokens, ln_in_g, ln_in_b, w_in, gdn_conv_w, gdn_A_log, gdn_dt_bias, gdn_norm_w, m2_conv_w, m2_conv_b, m2_dt_bias, m2_A_log, m2_D, m2_norm_w, hg_lb_logits, hg_norm_w, s5_A_re, s5_A_im, s5_B_re, s5_B_im, s5_C_re, s5_C_im, s5_D, s5_log_dt, s5_glu_w1, s5_glu_w2, w_branch, w_out, ln_g, ln_b):
    bsz, seq, d = x.shape
    ltot = N_META + seq
    lp = -(-ltot // TILE) * TILE
    pad = lp - ltot
    assert pad + N_META == TILE, "real tokens must start on the second tile"
    meta_tile = jnp.concatenate([jnp.zeros((pad, d), F32), meta_tokens.astype(F32)], axis=0)
    h = _ln_in_call(x.astype(F32), meta_tile, ln_in_g.astype(F32), ln_in_b.astype(F32), pad)

    msk = _masks64()
    tri = msk[0].astype(BF)
    ones = jnp.ones((CH, CH), BF)
    jj = jnp.arange(256)
    eyet = (jnp.arange(CH)[:, None] == (jj % CH)[None, :]).astype(F32)
    maskt = (jnp.arange(CH)[:, None] >= (jj % CH)[None, :]).astype(F32)
    bd = ((jj // CH)[:, None] == (jj // CH)[None, :]).astype(F32)

    cum = jnp.cumsum(jax.nn.softmax(hg_lb_logits.astype(F32), axis=0), axis=0)
    lower_bounds = cum - cum[0:1]

    w_pieces, wg_all = _prep_w(w_in)
    m1_all, q_all, pw_all, lm_all = jax.vmap(_prep_s5)(s5_A_re, s5_A_im, s5_B_re, s5_B_im, s5_C_re, s5_C_im,
                                                       s5_log_dt)
    glu1_all, glu2_all = s5_glu_w1.astype(BF), s5_glu_w2.astype(BF)
    wb_all, wo_all = (0.5 * w_branch).astype(BF), w_out.astype(BF)

    s5_d_all = s5_D.astype(F32)[:, None, :]
    for l in range(w_in.shape[0]):
        p, yd = _proj_s5_call(h, l, w_pieces, m1_all, q_all, pw_all, lm_all, s5_d_all, glu1_all, glu2_all)

        lane = jnp.stack([_lane_row(jnp.exp(gdn_A_log[l].astype(F32)), 4), _lane_row(gdn_dt_bias[l], 4)]
                         + [jnp.zeros((128,), F32)] * 6)
        gdn_prm = (gdn_conv_w[l].astype(F32), lane, gdn_norm_w[l].astype(F32).reshape(1, GDN_DK))

        rep = lambda v: jnp.repeat(v.astype(F32), M2_HEADDIM)
        vec = jnp.stack([rep(m2_dt_bias[l]), rep(-jnp.exp(m2_A_log[l].astype(F32))), rep(m2_D[l]),
                         m2_norm_w[l].astype(F32)] + [jnp.zeros((BRANCH_W,), F32)] * 4)
        ssd_prm = (m2_conv_w[l].astype(F32), m2_conv_b[l].astype(F32).reshape(1, -1), vec, eyet, maskt, bd, ones)

        lb = lower_bounds[l]
        hvec = jnp.stack([jnp.log(lb), jnp.log1p(-lb), 1.0 - lb] + [jnp.zeros((BRANCH_W,), F32)] * 5)
        hg_prm = (hvec, hg_norm_w[l].astype(F32).reshape(1, HG_DK))

        h = _mix_merge_call(h, l, p, yd, gdn_prm, ssd_prm, hg_prm, msk, tri, wg_all, wb_all,
                            wo_all, ln_g[l].astype(F32), ln_b[l].astype(F32), pad,
                            drop_first_tile=(l == w_in.shape[0] - 1))
    return h.astype(x.dtype)
```

```python
import functools
import itertools
import math

import jax
import jax.numpy as jnp
from jax import lax
from jax.experimental import pallas as pl
from jax.experimental.pallas import tpu as pltpu

F32 = jnp.float32
BF = jnp.bfloat16

D_MODEL = 1024
DEPTH = 4
N_META = 16
CONV_K = 4
N_BRANCH = 4
BRANCH_W = 512
GDN_HEADS = 4
GDN_DK = 128
M2_HEADS = 8
M2_HEADDIM = 64
M2_GROUPS = 2
M2_DSTATE = 128
HG_HEADS = 4
HG_DK = 128
S5_GROUP = 16
S5_NG = 32
S5_P = 64
S5_STATE = S5_NG * S5_P
ALPHA = (2 * DEPTH) ** 0.25
LN_EPS = 1e-5
RMS_EPS = 1e-6

TILE = 256
CH = 64
NCH = TILE // CH
SUB = 16
HG_EXP_CLAMP = 80.0
VMEM_LIMIT = 56 * 1024 * 1024

OFF_QKV, OFF_GZ, OFF_XBC, OFF_MZ, OFF_MDT = 0, 1536, 2048, 3072, 3584
OFF_HG, OFF_BA, OFF_S5 = 4096, 6144, 6272
NP = 6272
N1 = 7296
W_PIECES = ((0, 2048), (OFF_XBC, 1024), (OFF_MZ, 512), (OFF_MDT, 512), (OFF_HG, 2048), (OFF_BA, 128), (OFF_S5, 1024))


def _sigmoid(x):
    return 0.5 * (jnp.tanh(0.5 * x) + 1.0)


def _silu(x):
    return x * _sigmoid(x)


def _softplus(x):
    return jnp.maximum(x, 0.0) + jnp.log1p(jnp.exp(-jnp.abs(x)))


def _mm(a, b):
    return jnp.dot(a.astype(BF), b.astype(BF), preferred_element_type=F32)


def _mm_nt(a, b):
    return lax.dot_general(a.astype(BF), b.astype(BF), (((1,), (1,)), ((), ())),
                           preferred_element_type=F32)


def _mm_tn(a, b):
    return lax.dot_general(a.astype(BF), b.astype(BF), (((0,), (0,)), ((), ())),
                           preferred_element_type=F32)


def _mm3_left(m_bf, x):
    hi = x.astype(BF)
    r1 = x - hi.astype(F32)
    mid = r1.astype(BF)
    lo = (r1 - mid.astype(F32)).astype(BF)
    d = lambda v: jnp.dot(m_bf, v, preferred_element_type=F32)
    return (d(lo) + d(mid)) + d(hi)


def _rms_norm(x):
    return x * lax.rsqrt(jnp.mean(x * x, axis=-1, keepdims=True) + RMS_EPS)


def _masks64():
    i = jnp.arange(CH)[:, None]
    j = jnp.arange(CH)[None, :]
    ms = [(i >= j), (i > j)]
    for s in (1, 2, 4, 8, 16, 32):
        ms.append((i // (2 * s) == j // (2 * s)) & (i % (2 * s) >= s) & (j % (2 * s) < s))
    ms.append(i == j)
    return jnp.stack(ms).astype(F32)


def _cparams():
    return pltpu.CompilerParams(dimension_semantics=("arbitrary", "arbitrary"),
                                vmem_limit_bytes=VMEM_LIMIT)


def _full(shape):
    n = len(shape)
    return pl.BlockSpec(shape, lambda b, t: (0,) * n, pipeline_mode=pl.Buffered(1))


def _layer_full(shape, l):
    n = len(shape)
    return pl.BlockSpec((None,) + tuple(shape), lambda b, t: (l,) + (0,) * n, pipeline_mode=pl.Buffered(1))


def _ln_in_kernel(x_ref, m_ref, g_ref, b_ref, o_ref, *, pad):
    t = pl.program_id(1)
    x = jnp.where(t == 0, m_ref[...], x_ref[...])
    mu = jnp.mean(x, axis=-1, keepdims=True)
    xc = x - mu
    var = jnp.mean(xc * xc, axis=-1, keepdims=True)
    y = xc * lax.rsqrt(var + LN_EPS) * g_ref[...] + b_ref[...]
    row = t * TILE + lax.broadcasted_iota(jnp.int32, (TILE, 1), 0)
    o_ref[...] = jnp.where(row >= pad, y, 0.0)


def _ln_in_call(x, meta_tile, g, b, pad):
    bsz, seq, d = x.shape
    return pl.pallas_call(
        functools.partial(_ln_in_kernel, pad=pad),
        grid=(bsz, seq // TILE + 1),
        in_specs=[pl.BlockSpec((None, TILE, d), lambda b_, t: (b_, jnp.maximum(t - 1, 0), 0)),
                  _full((TILE, d)), _full((1, d)), _full((1, d))],
        out_specs=pl.BlockSpec((None, TILE, d), lambda b_, t: (b_, t, 0)),
        out_shape=jax.ShapeDtypeStruct((bsz, seq + TILE, d), F32),
        compiler_params=_cparams(),
        name="ln_in",
    )(x, meta_tile, g.reshape(1, d), b.reshape(1, d))


S5_CK = 8
S5_NCK = TILE // S5_CK
S5_PACK = 2 * S5_P
S5_LANES = S5_NG * S5_PACK
PROJ_CHUNK = 512
GATE_CHUNK = 256
MIX_PLAN = "gsh" + "gfgfgfgfh" * 4 + "gs" * 8 + "gg" + "hDhA" + "hhBhh" + "C"


def _transpose_pieces(arrs, lm_ref):
    for k, d in enumerate((4, 2, 1)):
        m = lm_ref[k:k + 1, :] > 0
        new = list(arrs)
        for s_ in range(8):
            if s_ & d == 0:
                lo, hi = arrs[s_], arrs[s_ + d]
                new[s_] = jnp.where(m, pltpu.roll(hi, 16 * d, 1), lo)
                new[s_ + d] = jnp.where(m, hi, pltpu.roll(lo, 128 - 16 * d, 1))
        arrs = new
    return arrs


def _s5_scan_block(b, cs, carry, tabs, st_ref):
    rows = slice(8 * b, 8 * b + 8)
    v = st_ref[rows, cs]
    for n, d in enumerate((1, 2, 4)):
        vsh = pltpu.roll(v, d, 0)
        v = v + (tabs[2 + 2 * n] * vsh + tabs[3 + 2 * n] * pltpu.roll(vsh, S5_P, 1))
    cb = jnp.broadcast_to(carry, (8, S5_PACK))
    v = v + (tabs[0] * cb + tabs[1] * pltpu.roll(cb, S5_P, 1))
    st_ref[rows, cs] = v
    return v[7:8, :]


def _proj_s5_kernel(h_ref, w0_ref, w1_ref, w2_ref, w3_ref, w4_ref, w5_ref, w6_ref,
                    m1_ref, q_ref, pw_ref, lm_ref, d_ref, g1_ref, g2_ref,
                    p_ref, yd_ref, u_ref, ug_ref, yq_ref, st_ref, y_ref, c_ref):
    t = pl.program_id(1)

    @pl.when(t == 0)
    def _():
        c_ref[...] = jnp.zeros_like(c_ref)

    def wcols(j, w):
        for (start, width), ref in zip(W_PIECES, (w0_ref, w1_ref, w2_ref, w3_ref, w4_ref, w5_ref, w6_ref)):
            if start <= j and j + w <= start + width:
                return ref[:, j - start:j - start + w]
        raise ValueError((j, w))

    hb = h_ref[...].astype(BF)
    u = jnp.dot(hb, wcols(OFF_S5, BRANCH_W), preferred_element_type=F32)
    for j in range(4):
        u_ref[j] = u[:, j * 128:(j + 1) * 128]

    starts = iter(range(0, NP, PROJ_CHUNK))

    def proj_chunks(n):
        for j in itertools.islice(starts, n):
            w = min(PROJ_CHUNK, NP - j)
            p_ref[:, j:j + w] = jnp.dot(hb, wcols(j, w), preferred_element_type=F32)

    for j in range(4):
        proj_chunks(1)
        arrs = [u_ref[j, pl.ds(s_, S5_NCK, stride=S5_CK), :] for s_ in range(S5_CK)]
        arrs = _transpose_pieces(arrs, lm_ref)
        for i in range(8):
            g = 8 * j + i
            ug_ref[:, g * 128:(g + 1) * 128] = arrs[i].astype(BF)
    row0 = lax.broadcasted_iota(jnp.int32, (S5_NCK, S5_PACK), 0) == 0
    last_inc = []
    for g in range(S5_NG):
        if g % 11 == 10:
            proj_chunks(1)
        cs = slice(g * S5_PACK, (g + 1) * S5_PACK)
        tw = jnp.dot(ug_ref[:, cs], m1_ref[g], preferred_element_type=F32)
        yq_ref[:, cs] = tw[:, :S5_PACK]
        inc = tw[:, S5_PACK:]
        last_inc.append(inc[S5_NCK - 1:S5_NCK, :])
        st_ref[:, cs] = jnp.where(row0, c_ref[:, cs], pltpu.roll(inc, 1, 0))

    for g in range(S5_NG):
        if g % 8 == 0:
            proj_chunks(1)
        cs = slice(g * S5_PACK, (g + 1) * S5_PACK)
        tabs = [pw_ref[8 * k:8 * k + 8, cs] for k in range(8)]
        x = jnp.zeros((1, S5_PACK), F32)
        for b in range(S5_NCK // 8):
            x = _s5_scan_block(b, cs, x, tabs, st_ref)
        c_ref[:, cs] = last_inc[g] + (tabs[0][0:1] * x + tabs[1][0:1] * pltpu.roll(
            jnp.broadcast_to(x, (8, S5_PACK)), S5_P, 1)[0:1])

    for g in range(S5_NG):
        if g % 16 == 0:
            proj_chunks(1)
        cs = slice(g * S5_PACK, (g + 1) * S5_PACK)
        yq_ref[:, cs] = yq_ref[:, cs] + _mm(st_ref[:, cs], q_ref[g])
    proj_chunks(NP)
    for j in range(4):
        arrs = _transpose_pieces([yq_ref[:, (8 * j + i) * 128:(8 * j + i + 1) * 128] for i in range(8)], lm_ref)
        for s_ in range(S5_CK):
            y_ref[j, pl.ds(s_, S5_NCK, stride=S5_CK), :] = arrs[s_]
    y = jnp.concatenate([y_ref[j] for j in range(4)], axis=1) + d_ref[...] * u
    y = 0.5 * y * (1.0 + jnp.tanh(math.sqrt(2.0 / math.pi) * (y + 0.044715 * (y * y * y))))
    yb = y.astype(BF)
    glu = jnp.dot(yb, g1_ref[...], preferred_element_type=F32) * _sigmoid(
        jnp.dot(yb, g2_ref[...], preferred_element_type=F32))
    z = jnp.dot(hb, wcols(OFF_S5 + BRANCH_W, BRANCH_W), preferred_element_type=F32)
    yd_ref[...] = (glu * _silu(z)).astype(yd_ref.dtype)


def _proj_s5_call(h, l, w_pieces, m1, q, pw, lm, dvec, g1, g2):
    bsz, lp, d = h.shape
    return pl.pallas_call(
        _proj_s5_kernel,
        grid=(bsz, lp // TILE),
        in_specs=[pl.BlockSpec((None, TILE, d), lambda b_, t: (b_, t, 0))]
        + [_layer_full((d, width), l) for _, width in W_PIECES]
        + [_layer_full((S5_NG, 128, 256), l), _layer_full((S5_NG, 128, 128), l), _layer_full((64, S5_LANES), l),
           _layer_full((8, 128), l), _layer_full((1, 512), l), _layer_full((512, 512), l),
           _layer_full((512, 512), l)],
        out_specs=[pl.BlockSpec((None, TILE, NP), lambda b_, t: (b_, t, 0)),
                   pl.BlockSpec((None, TILE, BRANCH_W), lambda b_, t: (b_, t, 0))],
        out_shape=[jax.ShapeDtypeStruct((bsz, lp, NP), F32), jax.ShapeDtypeStruct((bsz, lp, BRANCH_W), BF)],
        scratch_shapes=[pltpu.VMEM((4, TILE, 128), F32), pltpu.VMEM((S5_NCK, S5_LANES), BF),
                        pltpu.VMEM((S5_NCK, S5_LANES), F32), pltpu.VMEM((S5_NCK, S5_LANES), F32),
                        pltpu.VMEM((4, TILE, 128), F32), pltpu.VMEM((1, S5_LANES), F32)],
        compiler_params=_cparams(),
        name="proj_s5",
    )(h, *w_pieces, m1, q, pw, lm, dvec, g1, g2)


def _conv_chunk(xp_ref, cw, r0):
    x = xp_ref[r0 + 8:r0 + 8 + CH, :] * cw[CONV_K - 1:CONV_K, :]
    for j in range(CONV_K - 1):
        s = r0 + 8 - (CONV_K - 1) + j
        x = x + xp_ref[s:s + CH, :] * cw[j:j + 1, :]
    return x


def _gdn_steps(qkv_ref, z_ref, ba_ref, cw_ref, lane_ref, nw_ref, msk_ref, tri_ref, y_ref, xp_ref, s_ref):
    t = pl.program_id(1)

    @pl.when(t == 0)
    def _():
        xp_ref[0:8, :] = jnp.zeros((8, 3 * BRANCH_W), F32)
        s_ref[...] = jnp.zeros_like(s_ref)

    xp_ref[8:8 + TILE, :] = qkv_ref[...]
    yield
    cw = cw_ref[...]
    exp_a = lane_ref[0:1, :]
    dt_bias = lane_ref[1:2, :]
    nw = nw_ref[...]
    causal = msk_ref[0]
    strict = msk_ref[1]
    tri = tri_ref[...]
    items = []
    a_list = []
    for c in range(NCH):
        r0 = c * CH
        qkv = _silu(_conv_chunk(xp_ref, cw, r0))
        ba = ba_ref[r0:r0 + CH, :]
        beta_all = _sigmoid(ba)
        gl = -(exp_a * _softplus(ba + dt_bias))
        gc = _mm3_left(tri, gl)
        gc_t = gc.T
        for h in range(GDN_HEADS):
            sl = slice(h * GDN_DK, (h + 1) * GDN_DK)
            q = qkv[:, sl]
            k = qkv[:, BRANCH_W + h * GDN_DK:BRANCH_W + (h + 1) * GDN_DK]
            v = qkv[:, 2 * BRANCH_W + h * GDN_DK:2 * BRANCH_W + (h + 1) * GDN_DK]
            q = q * (lax.rsqrt(jnp.sum(q * q, axis=-1, keepdims=True) + RMS_EPS) * GDN_DK ** -0.5)
            k = k * lax.rsqrt(jnp.sum(k * k, axis=-1, keepdims=True) + RMS_EPS)
            beta = beta_all[:, h:h + 1]
            gcol = gc[:, 4 + h:5 + h]
            grow = gc_t[4 + h:5 + h, :]
            decay = jnp.exp(jnp.where(causal > 0, gcol - grow, -jnp.inf))
            kb = k * beta
            eg = jnp.exp(gcol)
            glast = gc[CH - 1:CH, 4 + h:5 + h]
            a_list.append(strict * (_mm_nt(kb, k) * decay))
            items.append(dict(
                rhs=jnp.concatenate([v * beta, kb * eg], axis=1),
                qk=causal * (_mm_nt(q, k) * decay),
                qd=q * eg,
                kd=k * jnp.exp(glast - gcol),
                gtot=jnp.exp(glast)))
            yield
    tinvs = [msk_ref[8] - a * msk_ref[2] for a in a_list]
    for lvl in range(1, 6):
        m = msk_ref[2 + lvl]
        t1 = [_mm(x, a * m) for x, a in zip(tinvs, a_list)]
        tinvs = [x - _mm(t_, x) for x, t_ in zip(tinvs, t1)]
        yield
    uws = [_mm(tinv, it["rhs"]) for tinv, it in zip(tinvs, items)]
    kd_uw = [_mm_tn(it["kd"], uw) for it, uw in zip(items, uws)]
    qk_uw = [_mm(it["qk"], uw) for it, uw in zip(items, uws)]
    yield
    states = [s_ref[h] for h in range(GDN_HEADS)]
    for c in range(NCH):
        r0 = c * CH
        for h in range(GDN_HEADS):
            sl = slice(h * GDN_DK, (h + 1) * GDN_DK)
            n = c * GDN_HEADS + h
            s = states[h]
            o = _mm(items[n]["qd"] - qk_uw[n][:, GDN_DK:], s) + qk_uw[n][:, :GDN_DK]
            states[h] = (s * items[n]["gtot"] - _mm(kd_uw[n][:, GDN_DK:], s)) + kd_uw[n][:, :GDN_DK]
            y = _rms_norm(o) * nw * _silu(z_ref[r0:r0 + CH, sl])
            y_ref[r0:r0 + CH, sl] = y.astype(y_ref.dtype)
        if c == NCH - 1:
            for h in range(GDN_HEADS):
                s_ref[h] = states[h]
            xp_ref[0:8, :] = xp_ref[TILE:TILE + 8, :]
        yield


def _ssd_steps(xbc_ref, zdt_ref, cw_ref, cb_ref, vec_ref, eyet_ref, maskt_ref, bd_ref, tri_ref, ones_ref,
               y_ref, xp_ref, s_ref, *, pad):
    t = pl.program_id(1)

    @pl.when(t == 0)
    def _():
        xp_ref[0:8, :] = jnp.zeros((8, 1024), F32)
        s_ref[...] = jnp.zeros_like(s_ref)

    xp_ref[8:8 + TILE, :] = xbc_ref[...]
    yield
    cw = cw_ref[...]
    cb = cb_ref[...]
    dt_bias = vec_ref[0:1, :]
    neg_a = vec_ref[1:2, :]
    d_skip = vec_ref[2:3, :]
    nw = vec_ref[3:4, :]
    tri = tri_ref[...]
    ones = ones_ref[...]
    gw = M2_HEADDIM * (M2_HEADS // M2_GROUPS)
    for c in range(NCH):
        r0 = c * CH
        xbc = _silu(_conv_chunk(xp_ref, cw, r0) + cb)
        xs = xbc[:, :BRANCH_W]
        row = t * TILE + r0 + lax.broadcasted_iota(jnp.int32, (CH, 1), 0)
        valid = (row >= pad).astype(F32)
        dt = _softplus(zdt_ref[r0:r0 + CH, BRANCH_W:] + dt_bias) * valid
        a = dt * neg_a
        acum = _mm3_left(tri, a)
        xdt = xs * dt
        z = zdt_ref[r0:r0 + CH, :BRANCH_W]
        for g in range(M2_GROUPS):
            gs = slice(g * gw, (g + 1) * gw)
            bg = xbc[:, BRANCH_W + g * M2_DSTATE:BRANCH_W + (g + 1) * M2_DSTATE]
            cg = xbc[:, BRANCH_W + M2_GROUPS * M2_DSTATE + g * M2_DSTATE:
                     BRANCH_W + M2_GROUPS * M2_DSTATE + (g + 1) * M2_DSTATE]
            ag = acum[:, gs]
            cbt = _mm_nt(cg, jnp.concatenate([bg] * 4, axis=0))
            arow = _mm3_left(ones, ag * eyet_ref[...])
            lm = jnp.exp(jnp.where(maskt_ref[...] > 0, ag - arow, -jnp.inf))
            xg = xdt[:, gs]
            xbd = jnp.concatenate([xg] * 4, axis=0) * bd_ref[...]
            y_diag = _mm(cbt * lm, xbd)
            alast = ag[CH - 1:CH, :]
            s = s_ref[g]
            y_off = _mm(cg, s) * jnp.exp(ag)
            s_ref[g] = s * jnp.exp(alast) + _mm_tn(bg, xg * jnp.exp(alast - ag))
            y = (y_diag + y_off + d_skip[:, gs] * xs[:, gs]) * _silu(z[:, gs])
            y = _rms_norm(y) * nw[:, gs]
            y_ref[r0:r0 + CH, gs] = y.astype(y_ref.dtype)
            if c == NCH - 1 and g == M2_GROUPS - 1:
                xp_ref[0:8, :] = xp_ref[TILE:TILE + 8, :]
            yield


def _hgrn_steps(p_ref, vec_ref, nw_ref, msk_ref, tri_ref, y_ref, s_ref):
    t = pl.program_id(1)

    @pl.when(t == 0)
    def _():
        s_ref[...] = jnp.zeros_like(s_ref)

    yield
    log_lb = vec_ref[0:1, :]
    log1m_lb = vec_ref[1:2, :]
    one_m_lb = vec_ref[2:3, :]
    nw = nw_ref[...]
    causal = msk_ref[0]
    tri = tri_ref[...]
    hsl = [slice(h * HG_DK, (h + 1) * HG_DK) for h in range(HG_HEADS)]
    nsub = CH // SUB
    prep = []
    for c in range(NCH):
        r0 = c * CH
        q = _silu(p_ref[r0:r0 + CH, 0:512])
        zf = p_ref[r0:r0 + CH, 512:1024]
        e = jnp.exp(-jnp.abs(zf))
        lsig = jnp.minimum(zf, 0.0) - jnp.log1p(e)
        bb = log1m_lb + lsig
        logf = jnp.maximum(log_lb, bb) + jnp.log1p(jnp.exp(-jnp.abs(log_lb - bb)))
        sig_neg = jnp.where(zf >= 0, e, 1.0) / (1.0 + e)
        k = one_m_lb * sig_neg
        gcum = _mm3_left(tri, logf)
        glast = gcum[CH - 1:CH, :]
        subs = []
        for i in range(nsub):
            rs = slice(i * SUB, (i + 1) * SUB)
            n = (i + 1) * SUB
            gref = gcum[i * SUB - 1:i * SUB, :] if i > 0 else jnp.zeros((1, BRANCH_W), F32)
            qt = q[rs] * jnp.exp(gcum[rs] - gref)
            kt = k[:n] * jnp.exp(jnp.minimum(gref - gcum[:n], HG_EXP_CLAMP))
            subs.append((qt, kt, qt * jnp.exp(gref)))
        prep.append(dict(kd=k * jnp.exp(glast - gcum), eg=jnp.exp(glast), subs=subs))
        yield
    incs = [[_mm_tn(p_ref[c * CH:(c + 1) * CH, 1024 + h * HG_DK:1024 + (h + 1) * HG_DK], prep[c]["kd"][:, hsl[h]])
             for h in range(HG_HEADS)] for c in range(NCH)]
    states = [[s_ref[h] for h in range(HG_HEADS)]]
    for c in range(NCH):
        states.append([states[c][h] * prep[c]["eg"][:, hsl[h]] + incs[c][h] for h in range(HG_HEADS)])
    for h in range(HG_HEADS):
        s_ref[h] = states[NCH][h]
    yield
    ams = [[[_mm_nt(prep[c]["subs"][i][0][:, hsl[h]], prep[c]["subs"][i][1][:, hsl[h]])
             * causal[i * SUB:(i + 1) * SUB, :(i + 1) * SUB]
             for h in range(HG_HEADS)] for i in range(nsub)] for c in range(NCH)]
    yield
    for c in range(NCH):
        r0 = c * CH
        for h in range(HG_HEADS):
            iv = p_ref[r0:r0 + CH, 1024 + h * HG_DK:1024 + (h + 1) * HG_DK]
            o = jnp.concatenate(
                [_mm(ams[c][i][h], iv[:(i + 1) * SUB]) + _mm_nt(prep[c]["subs"][i][2][:, hsl[h]], states[c][h])
                 for i in range(nsub)], axis=0)
            z = p_ref[r0:r0 + CH, 1536 + h * HG_DK:1536 + (h + 1) * HG_DK]
            y_ref[r0:r0 + CH, hsl[h]] = (_rms_norm(o) * nw * _silu(z)).astype(y_ref.dtype)
        yield


def _mix_merge_kernel(h_ref, qkv_ref, gz_ref, ba_ref, xbc_ref, zdt_ref, hg_ref, yd_ref,
                      gcw_ref, glane_ref, gnw_ref, scw_ref, scb_ref, svec_ref, eyet_ref, maskt_ref, bd_ref, ones_ref,
                      hvec_ref, hnw_ref, msk_ref, tri_ref, wg_ref, wb_ref, wo_ref, g_ref, b_ref,
                      o_ref, gxp_ref, gs_ref, sxp_ref, ss_ref, hs_ref, ya_ref, yb_ref, yc_ref, gate_ref, *, pad):
    t = pl.program_id(1)
    h = h_ref[...]
    hb = h.astype(BF)

    def gate_piece(j):
        cs = slice(j * GATE_CHUNK, (j + 1) * GATE_CHUNK)
        gate_ref[:, cs] = jnp.tanh(jnp.dot(hb, wg_ref[:, cs], preferred_element_type=F32)) + 1.0

    def branch_piece(br, y_ref):
        cs = slice(br * D_MODEL, (br + 1) * D_MODEL)
        gate_ref[:, cs] = gate_ref[:, cs] * jnp.dot(y_ref[...], wb_ref[br], preferred_element_type=F32)

    gdn = _gdn_steps(qkv_ref, gz_ref, ba_ref, gcw_ref, glane_ref, gnw_ref, msk_ref, tri_ref, ya_ref, gxp_ref, gs_ref)
    ssd = _ssd_steps(xbc_ref, zdt_ref, scw_ref, scb_ref, svec_ref, eyet_ref, maskt_ref, bd_ref, tri_ref, ones_ref,
                     yb_ref, sxp_ref, ss_ref, pad=pad)
    hgrn = _hgrn_steps(hg_ref, hvec_ref, hnw_ref, msk_ref, tri_ref, yc_ref, hs_ref)
    gates = iter([functools.partial(gate_piece, j) for j in range(N_BRANCH * D_MODEL // GATE_CHUNK)])
    emit = {"g": lambda: next(gdn), "s": lambda: next(ssd), "h": lambda: next(hgrn), "f": lambda: next(gates)(),
            "A": lambda: branch_piece(0, ya_ref), "B": lambda: branch_piece(1, yb_ref),
            "C": lambda: branch_piece(2, yc_ref), "D": lambda: branch_piece(3, yd_ref)}
    for tok in MIX_PLAN:
        emit[tok]()
    mixed = ((gate_ref[:, 0:D_MODEL] + gate_ref[:, D_MODEL:2 * D_MODEL])
             + (gate_ref[:, 2 * D_MODEL:3 * D_MODEL] + gate_ref[:, 3 * D_MODEL:4 * D_MODEL]))
    out = jnp.dot(mixed.astype(BF), wo_ref[...], preferred_element_type=F32)
    r = ALPHA * h + out
    mu = jnp.mean(r, axis=-1, keepdims=True)
    rc = r - mu
    var = jnp.mean(rc * rc, axis=-1, keepdims=True)
    y = rc * lax.rsqrt(var + LN_EPS) * g_ref[...] + b_ref[...]
    row = t * TILE + lax.broadcasted_iota(jnp.int32, (TILE, 1), 0)
    o_ref[...] = jnp.where(row >= pad, y, 0.0)


def _mix_merge_call(h, l, p, yd, gdn_prm, ssd_prm, hg_prm, msk, tri, wg, wb, wo, g, b, pad, drop_first_tile):
    bsz, lp, d = h.shape
    tok = lambda w, j=0: pl.BlockSpec((None, TILE, w), lambda b_, t: (b_, t, j))
    return pl.pallas_call(
        functools.partial(_mix_merge_kernel, pad=pad),
        grid=(bsz, lp // TILE),
        in_specs=[tok(d), tok(1536, OFF_QKV // 1536), tok(512, OFF_GZ // 512), tok(128, OFF_BA // 128),
                  tok(1024, OFF_XBC // 1024), tok(1024, OFF_MZ // 1024), tok(2048, OFF_HG // 2048), tok(BRANCH_W),
                  _full((CONV_K, 1536)), _full((8, 128)), _full((1, GDN_DK)),
                  _full((CONV_K, 1024)), _full((1, 1024)), _full((8, 512)),
                  _full((CH, 256)), _full((CH, 256)), _full((256, 256)), _full((CH, CH)),
                  _full((8, 512)), _full((1, HG_DK)), _full((9, CH, CH)), _full((CH, CH)),
                  _layer_full((d, N_BRANCH * d), l), _layer_full((N_BRANCH, BRANCH_W, d), l),
                  _layer_full((d, d), l), _full((1, d)), _full((1, d))],
        out_specs=(pl.BlockSpec((None, TILE, d), lambda b_, t: (b_, jnp.maximum(t - 1, 0), 0))
                   if drop_first_tile else tok(d)),
        out_shape=jax.ShapeDtypeStruct((bsz, lp - TILE if drop_first_tile else lp, d), F32),
        scratch_shapes=[pltpu.VMEM((TILE + 8, 1536), F32), pltpu.VMEM((GDN_HEADS, GDN_DK, GDN_DK), F32),
                        pltpu.VMEM((TILE + 8, 1024), F32), pltpu.VMEM((M2_GROUPS, M2_DSTATE, 256), F32),
                        pltpu.VMEM((HG_HEADS, HG_DK, HG_DK), F32),
                        pltpu.VMEM((TILE, BRANCH_W), BF), pltpu.VMEM((TILE, BRANCH_W), BF),
                        pltpu.VMEM((TILE, BRANCH_W), BF), pltpu.VMEM((TILE, N_BRANCH * D_MODEL), F32)],
        compiler_params=_cparams(),
        name="mix_merge",
    )(h, p, p, p, p, p, p, yd, *gdn_prm, *ssd_prm, *hg_prm, msk, tri, wg, wb, wo, g.reshape(1, d), b.reshape(1, d))


W_IN_SIZES = (1536, 512, 4, 4, 1024, 512, 8, 512, 512, 512, 512, 512, 512, 4096)
W_IN_OFF = tuple(itertools.accumulate((0,) + W_IN_SIZES))
W_RUNS = ((W_IN_OFF[0], W_IN_OFF[2]), (W_IN_OFF[4], W_IN_OFF[5]), (W_IN_OFF[5], W_IN_OFF[6]),
          (W_IN_OFF[7], W_IN_OFF[11]), (W_IN_OFF[11], W_IN_OFF[13]), (W_IN_OFF[13], W_IN_OFF[14]),
          (W_IN_OFF[2], W_IN_OFF[2] + 128), (W_IN_OFF[6] - 8, W_IN_OFF[6] + 120))
W_ROWS = 256


def _prep_w_kernel(wt_ref, *o_refs):
    for n, ((a, b), o_ref) in enumerate(zip(W_RUNS, o_refs)):
        x = wt_ref[a:b, :].T
        if n == 5:
            x = 0.5 * x
        o_ref[...] = x.astype(BF)


def _prep_w(w):
    nl, d, n_in = w.shape
    widths = [b - a for a, b in W_RUNS]
    qkv_gz, xbc, mz, hg, s5, gates, win_ba, win_dt = pl.pallas_call(
        _prep_w_kernel,
        grid=(nl, d // W_ROWS),
        in_specs=[pl.BlockSpec((None, n_in, W_ROWS), lambda l, r: (l, 0, r))],
        out_specs=[pl.BlockSpec((None, W_ROWS, n), lambda l, r: (l, r, 0)) for n in widths],
        out_shape=[jax.ShapeDtypeStruct((nl, d, n), BF) for n in widths],
        compiler_params=_cparams(),
        name="prep_w",
    )(jnp.swapaxes(w, 1, 2))
    ba = jnp.where(jnp.arange(128) < 8, win_ba, jnp.zeros((), BF))
    mdt_e = jnp.repeat(win_dt[..., 8:8 + M2_HEADS], M2_HEADDIM, axis=-1)
    return (qkv_gz, xbc, mz, mdt_e, hg, ba, s5), gates


def _lane_row(vals, off, width=128):
    return jnp.zeros((width,), F32).at[off:off + vals.shape[0]].set(vals.astype(F32))


def _prep_s5(a_re, a_im, b_re, b_im, c_re, c_im, log_dt):
    f = lambda v: v.astype(F32)
    a_re, a_im, b_re, b_im, c_re, c_im = map(f, (a_re, a_im, b_re, b_im, c_re, c_im))
    dt = jnp.exp(f(log_dt))[:, None]

    def lam_pow(k):
        mag = jnp.exp(a_re * dt * k)
        return mag * jnp.cos(a_im * dt * k), mag * jnp.sin(a_im * dt * k)

    lam_re, lam_im = lam_pow(1.0)
    den = jnp.square(a_re) + jnp.square(a_im)
    nr, ni = lam_re - 1.0, lam_im
    z_re, z_im = (nr * a_re + ni * a_im) / den, (ni * a_re - nr * a_im) / den
    bb_re = z_re[..., None] * b_re - z_im[..., None] * b_im
    bb_im = z_re[..., None] * b_im + z_im[..., None] * b_re
    ks = jnp.arange(S5_CK, dtype=F32)[:, None, None]
    pr, pi = lam_pow(ks)
    p1r, p1i = lam_pow(ks + 1.0)
    mr, mi = lam_pow(S5_CK * (ks + 1.0))

    def c_times(qr, qi):
        return (c_re[None] * qr[:, :, None, :] - c_im[None] * qi[:, :, None, :],
                c_re[None] * qi[:, :, None, :] + c_im[None] * qr[:, :, None, :])

    cl_re, cl_im = c_times(pr, pi)
    kd = (jnp.einsum('dgcp,gpe->dgce', cl_re, bb_re) - jnp.einsum('dgcp,gpe->dgce', cl_im, bb_im))
    s_i = jnp.arange(S5_CK)[:, None]
    t_i = jnp.arange(S5_CK)[None, :]
    kt = jnp.where((t_i >= s_i)[:, :, None, None, None], kd[jnp.clip(t_i - s_i, 0, S5_CK - 1)], 0.0)
    toep = jnp.transpose(kt, (2, 0, 4, 1, 3)).reshape(S5_NG, 128, 128)
    rr, ri = pr[::-1], pi[::-1]
    w_re = rr[..., None] * bb_re[None] - ri[..., None] * bb_im[None]
    w_im = rr[..., None] * bb_im[None] + ri[..., None] * bb_re[None]
    to_rows = lambda w: jnp.transpose(w, (1, 0, 3, 2)).reshape(S5_NG, 128, S5_P)
    m1 = jnp.concatenate([toep, to_rows(w_re), to_rows(w_im)], axis=2).astype(BF)
    q_re, q_im = c_times(p1r, p1i)
    to_cols = lambda v: jnp.transpose(v, (1, 3, 0, 2)).reshape(S5_NG, S5_P, 128)
    q = jnp.concatenate([to_cols(q_re), -to_cols(q_im)], axis=1).astype(BF)
    pack1 = lambda v: jnp.concatenate([v, v], axis=-1).reshape(v.shape[0], S5_LANES)
    pack2 = lambda v: jnp.concatenate([-v, v], axis=-1).reshape(v.shape[0], S5_LANES)
    row = jnp.arange(8)[:, None]
    tabs = [pack1(mr), pack2(mi)]
    for dd in (1, 2, 4):
        tabs.append(jnp.where(row >= dd, pack1(mr[dd - 1:dd]), 0.0))
        tabs.append(jnp.where(row >= dd, pack2(mi[dd - 1:dd]), 0.0))
    lane_piece = jnp.arange(128) // 16
    lm = jnp.stack([(lane_piece & dd) != 0 for dd in (4, 2, 1)] + [jnp.zeros((128,), bool)] * 5).astype(F32)
    return m1, q, jnp.concatenate(tabs, axis=0), lm


@jax.jit
def kernel(x, meta_tokens, ln_in_g, ln_in_b, w_in, gdn_conv_w, gdn_A_log, gdn_dt_bias, gdn_norm_w, m2_conv_w, m2_conv_b, m2_dt_bias, m2_A_log, m2_D, m2_norm_w, hg_lb_logits, hg_norm_w, s5_A_re, s5_A_im, s5_B_re, s5_B_im, s5_C_re, s5_C_im, s5_D, s5_log_dt, s5_glu_w1, s5_glu_w2, w_branch, w_out, ln_g, ln_b):
    bsz, seq, d = x.shape
    ltot = N_META + seq
    lp = -(-ltot // TILE) * TILE
    pad = lp - ltot
    assert pad + N_META == TILE, "real tokens must start on the second tile"
    meta_tile = jnp.concatenate([jnp.zeros((pad, d), F32), meta_tokens.astype(F32)], axis=0)
    h = _ln_in_call(x.astype(F32), meta_tile, ln_in_g.astype(F32), ln_in_b.astype(F32), pad)

    msk = _masks64()
    tri = msk[0].astype(BF)
    ones = jnp.ones((CH, CH), BF)
    jj = jnp.arange(256)
    eyet = (jnp.arange(CH)[:, None] == (jj % CH)[None, :]).astype(F32)
    maskt = (jnp.arange(CH)[:, None] >= (jj % CH)[None, :]).astype(F32)
    bd = ((jj // CH)[:, None] == (jj // CH)[None, :]).astype(F32)

    cum = jnp.cumsum(jax.nn.softmax(hg_lb_logits.astype(F32), axis=0), axis=0)
    lower_bounds = cum - cum[0:1]

    w_pieces, wg_all = _prep_w(w_in)
    m1_all, q_all, pw_all, lm_all = jax.vmap(_prep_s5)(s5_A_re, s5_A_im, s5_B_re, s5_B_im, s5_C_re, s5_C_im,
                                                       s5_log_dt)
    glu1_all, glu2_all = s5_glu_w1.astype(BF), s5_glu_w2.astype(BF)
    wb_all, wo_all = (0.5 * w_branch).astype(BF), w_out.astype(BF)

    s5_d_all = s5_D.astype(F32)[:, None, :]
    for l in range(w_in.shape[0]):
        p, yd = _proj_s5_call(h, l, w_pieces, m1_all, q_all, pw_all, lm_all, s5_d_all, glu1_all, glu2_all)

        lane = jnp.stack([_lane_row(jnp.exp(gdn_A_log[l].astype(F32)), 4), _lane_row(gdn_dt_bias[l], 4)]
                         + [jnp.zeros((128,), F32)] * 6)
        gdn_prm = (gdn_conv_w[l].astype(F32), lane, gdn_norm_w[l].astype(F32).reshape(1, GDN_DK))

        rep = lambda v: jnp.repeat(v.astype(F32), M2_HEADDIM)
        vec = jnp.stack([rep(m2_dt_bias[l]), rep(-jnp.exp(m2_A_log[l].astype(F32))), rep(m2_D[l]),
                         m2_norm_w[l].astype(F32)] + [jnp.zeros((BRANCH_W,), F32)] * 4)
        ssd_prm = (m2_conv_w[l].astype(F32), m2_conv_b[l].astype(F32).reshape(1, -1), vec, eyet, maskt, bd, ones)

        lb = lower_bounds[l]
        hvec = jnp.stack([jnp.log(lb), jnp.log1p(-lb), 1.0 - lb] + [jnp.zeros((BRANCH_W,), F32)] * 5)
        hg_prm = (hvec, hg_norm_w[l].astype(F32).reshape(1, HG_DK))

        h = _mix_merge_call(h, l, p, yd, gdn_prm, ssd_prm, hg_prm, msk, tri, wg_all, wb_all,
                            wo_all, ln_g[l].astype(F32), ln_b[l].astype(F32), pad,
                            drop_first_tile=(l == w_in.shape[0] - 1))
    return h.astype(x.dtype)
```
